```python
import jax, jax.numpy as jnp
from jax import lax
import numpy as np

D_MODEL = 2048
BATCH = 8
SEQ = 4096
DEPTH = 1
DEC_BATCH = 4
DEC_SEQ = 8192
PAST_LEN = 128

N_META = 16
GRID_W = 64
HEAD_DIM = 128
N_Q_HEADS = (D_MODEL // 2) // HEAD_DIM
N_KV_HEADS = N_Q_HEADS // 4
ATTN_WIDTH = N_Q_HEADS * HEAD_DIM
KV_WIDTH = N_KV_HEADS * HEAD_DIM
LRU_WIDTH = D_MODEL - ATTN_WIDTH
LRU_BLOCKS = 8
LRU_BLOCK = LRU_WIDTH // LRU_BLOCKS
LRU_CONV_W = 4
LRU_C = 8.0
MIX_WIDTH = ATTN_WIDTH + LRU_WIDTH
IN_WIDTH = ATTN_WIDTH + 2 * KV_WIDTH + 2 * LRU_WIDTH
FFN_DIM = 5632
FFN_CONV_W = 3
Q_BLOCK = 128
ROPE_THETA = 10000.0
EPS = 1e-6

kernel_name = 'hymba_axial_gqa_rglru_encoder'


def rms_norm(x, g):
    xf = x.astype(jnp.float32)
    y = xf * lax.rsqrt(jnp.mean(xf * xf, axis=-1, keepdims=True) + EPS)
    return (y * g.astype(jnp.float32)).astype(x.dtype)


def dwconv(x, w, b, pad_left):
    K = w.shape[0]
    L = x.shape[1]
    xp = jnp.pad(x, ((0, 0), (pad_left, K - 1 - pad_left), (0, 0)))
    out = xp[:, 0:L] * w[0]
    for k in range(1, K):
        out = out + xp[:, k:k + L] * w[k]
    return out + b


def axial_rope_tables(n_tokens):
    rows = n_tokens // GRID_W
    t_row = jnp.repeat(jnp.arange(rows), GRID_W).astype(jnp.float32)
    t_col = jnp.tile(jnp.arange(GRID_W), rows).astype(jnp.float32)
    half = HEAD_DIM // 2
    inv = ROPE_THETA ** (-jnp.arange(0, half, 2, dtype=jnp.float32) / half)
    ang = jnp.concatenate([t_row[:, None] * inv, t_col[:, None] * inv], axis=-1)
    ang = jnp.concatenate([jnp.zeros((N_META, half), jnp.float32), ang], axis=0)
    return jnp.cos(ang), jnp.sin(ang)


def apply_axial_rope(x, cos, sin):
    B, L, H, D = x.shape
    q = D // 4
    xf = x.astype(jnp.float32).reshape(B, L, H, 2, 2, q)
    x1 = xf[..., 0, :]
    x2 = xf[..., 1, :]
    c = cos.reshape(L, 1, 2, q)
    s = sin.reshape(L, 1, 2, q)
    out = jnp.stack([x1 * c - x2 * s, x2 * c + x1 * s], axis=-2)
    return out.reshape(B, L, H, D).astype(x.dtype)


def gqa_block_attention(q, k, v):
    B, L = q.shape[0], q.shape[1]
    G = N_Q_HEADS // N_KV_HEADS
    nblk = -(-L // Q_BLOCK)
    qp = jnp.pad(q, ((0, 0), (0, nblk * Q_BLOCK - L), (0, 0), (0, 0)))
    qb = qp.reshape(B, nblk, Q_BLOCK, N_KV_HEADS, G, HEAD_DIM).transpose(1, 0, 2, 3, 4, 5)
    scale = HEAD_DIM ** -0.5

    def block(qi):
        s = jnp.einsum('bqkgd,bskd->bkgqs', qi, k).astype(jnp.float32) * scale
        p = jax.nn.softmax(s, axis=-1).astype(v.dtype)
        return jnp.einsum('bkgqs,bskd->bqkgd', p, v)

    o = lax.map(block, qb)
    o = o.transpose(1, 0, 2, 3, 4, 5).reshape(B, nblk * Q_BLOCK, ATTN_WIDTH)
    return o[:, :L]


def rg_lru(x, w_r, b_r, w_i, b_i, lam, reverse):
    B, L, C = x.shape
    xb = x.reshape(B, L, LRU_BLOCKS, LRU_BLOCK)
    r = jax.nn.sigmoid((jnp.einsum('blnc,ncd->blnd', xb, w_r).reshape(B, L, C) + b_r).astype(jnp.float32))
    i = jax.nn.sigmoid((jnp.einsum('blnc,ncd->blnd', xb, w_i).reshape(B, L, C) + b_i).astype(jnp.float32))
    log_a = -LRU_C * r * jax.nn.softplus(-lam.astype(jnp.float32))
    a = jnp.exp(log_a)
    u = jnp.sqrt(-jnp.expm1(2.0 * log_a)) * (i * x.astype(jnp.float32))

    def combine(e1, e2):
        a1, b1 = e1
        a2, b2 = e2
        return a1 * a2, a2 * b1 + b2

    _, h = lax.associative_scan(combine, (a, u), reverse=reverse, axis=1)
    return h


def encoder_layer(h, norm_pre_mix, w_in, q_norm, k_norm, lru_conv_w, lru_conv_b, lru_w_r, lru_b_r,
                  lru_w_i, lru_b_i, lru_lambda, attn_out_norm, lru_out_norm, w_out, norm_post_mix,
                  norm_pre_ffn, w_up, ffn_conv_w, ffn_conv_b, w_down, norm_post_ffn):
    B, L, _ = h.shape
    z = rms_norm(h, norm_pre_mix) @ w_in
    o1 = ATTN_WIDTH
    o2 = o1 + KV_WIDTH
    o3 = o2 + KV_WIDTH
    o4 = o3 + LRU_WIDTH
    q = z[..., :o1].reshape(B, L, N_Q_HEADS, HEAD_DIM)
    k = z[..., o1:o2].reshape(B, L, N_KV_HEADS, HEAD_DIM)
    v = z[..., o2:o3].reshape(B, L, N_KV_HEADS, HEAD_DIM)
    xr = z[..., o3:o4]
    gy = z[..., o4:]
    cos, sin = axial_rope_tables(L - N_META)
    q = apply_axial_rope(rms_norm(q, q_norm), cos, sin)
    k = apply_axial_rope(rms_norm(k, k_norm), cos, sin)
    attn = gqa_block_attention(q, k, v)
    xc = dwconv(xr, lru_conv_w, lru_conv_b, LRU_CONV_W // 2)
    hr = (rg_lru(xc, lru_w_r[0], lru_b_r[0], lru_w_i[0], lru_b_i[0], lru_lambda[0], False)
          + rg_lru(xc, lru_w_r[1], lru_b_r[1], lru_w_i[1], lru_b_i[1], lru_lambda[1], True))
    lru = (hr * jax.nn.gelu(gy.astype(jnp.float32))).astype(h.dtype)
    mixed = jnp.concatenate([rms_norm(attn, attn_out_norm), rms_norm(lru, lru_out_norm)], axis=-1)
    h = h + rms_norm(mixed @ w_out, norm_post_mix)
    up = rms_norm(h, norm_pre_ffn) @ w_up
    g = dwconv(up[..., :FFN_DIM], ffn_conv_w, ffn_conv_b, FFN_CONV_W // 2)
    f = (jax.nn.silu(g) * up[..., FFN_DIM:]) @ w_down
    return h + rms_norm(f, norm_post_ffn)


def run_trunk(x, meta_tokens, weights):
    B = x.shape[0]
    meta = jnp.broadcast_to(meta_tokens.astype(x.dtype)[None], (B, N_META, D_MODEL))
    h = jnp.concatenate([meta, x], axis=1)
    for l in range(DEPTH):
        h = encoder_layer(h, *[w[l] for w in weights])
    return h[:, N_META:]


def setup_inputs(seed: int = 0) -> dict:
    key = jax.random.key(seed)
    ks = jax.random.split(key, 24)
    f32 = jnp.float32

    def nrm(k, shape, scale):
        return jax.random.normal(k, shape, f32) * scale

    def gain(k, n):
        return 1.0 + 0.05 * jax.random.normal(k, (DEPTH, n), f32)

    u = jax.random.uniform(ks[13], (DEPTH, 2, LRU_WIDTH), f32, 0.9, 0.999)
    a = u ** (1.0 / LRU_C)
    lam = jnp.log(a) - jnp.log1p(-a)
    return {
        'x_prompt': nrm(ks[0], (BATCH, SEQ, D_MODEL), 1.0),
        'x_sample': nrm(ks[1], (DEC_BATCH, DEC_SEQ, D_MODEL), 1.0),
        'meta_tokens': nrm(ks[2], (N_META, D_MODEL), 1.0),
        'norm_pre_mix': gain(ks[3], D_MODEL),
        'w_in': nrm(ks[4], (DEPTH, D_MODEL, IN_WIDTH), D_MODEL ** -0.5),
        'q_norm': gain(ks[5], HEAD_DIM),
        'k_norm': gain(ks[6], HEAD_DIM),
        'lru_conv_w': nrm(ks[7], (DEPTH, LRU_CONV_W, LRU_WIDTH), LRU_CONV_W ** -0.5),
        'lru_conv_b': nrm(ks[8], (DEPTH, LRU_WIDTH), 0.02),
        'lru_w_r': nrm(ks[9], (DEPTH, 2, LRU_BLOCKS, LRU_BLOCK, LRU_BLOCK), LRU_BLOCK ** -0.5),
        'lru_b_r': nrm(ks[10], (DEPTH, 2, LRU_WIDTH), 0.02),
        'lru_w_i': nrm(ks[11], (DEPTH, 2, LRU_BLOCKS, LRU_BLOCK, LRU_BLOCK), LRU_BLOCK ** -0.5),
        'lru_b_i': nrm(ks[12], (DEPTH, 2, LRU_WIDTH), 0.02),
        'lru_lambda': lam,
        'attn_out_norm': gain(ks[14], ATTN_WIDTH),
        'lru_out_norm': gain(ks[15], LRU_WIDTH),
        'w_out': nrm(ks[16], (DEPTH, MIX_WIDTH, D_MODEL), MIX_WIDTH ** -0.5),
        'norm_post_mix': gain(ks[17], D_MODEL),
        'norm_pre_ffn': gain(ks[18], D_MODEL),
        'w_up': nrm(ks[19], (DEPTH, D_MODEL, 2 * FFN_DIM), D_MODEL ** -0.5),
        'ffn_conv_w': nrm(ks[20], (DEPTH, FFN_CONV_W, FFN_DIM), FFN_CONV_W ** -0.5),
        'ffn_conv_b': nrm(ks[21], (DEPTH, FFN_DIM), 0.02),
        'w_down': nrm(ks[22], (DEPTH, FFN_DIM, D_MODEL), FFN_DIM ** -0.5),
        'norm_post_ffn': gain(ks[23], D_MODEL),
    }


def reference(x_prompt, x_sample, meta_tokens, norm_pre_mix, w_in, q_norm, k_norm, lru_conv_w, lru_conv_b,
              lru_w_r, lru_b_r, lru_w_i, lru_b_i, lru_lambda, attn_out_norm, lru_out_norm, w_out,
              norm_post_mix, norm_pre_ffn, w_up, ffn_conv_w, ffn_conv_b, w_down, norm_post_ffn):
    weights = (norm_pre_mix, w_in, q_norm, k_norm, lru_conv_w, lru_conv_b, lru_w_r, lru_b_r,
               lru_w_i, lru_b_i, lru_lambda, attn_out_norm, lru_out_norm, w_out, norm_post_mix,
               norm_pre_ffn, w_up, ffn_conv_w, ffn_conv_b, w_down, norm_post_ffn)
    y_prompt = run_trunk(x_prompt, meta_tokens, weights)
    y_sample = run_trunk(x_sample, meta_tokens, weights)
    return (y_prompt, y_sample)
```

```python
import functools

import jax
import jax.numpy as jnp
from jax import lax
from jax.experimental import pallas as pl
from jax.experimental.pallas import tpu as pltpu

D_MODEL = 2048
N_META = 16
GRID_W = 64
HEAD_DIM = 128
N_Q_HEADS = 8
N_KV_HEADS = 2
Q_PER_KV = N_Q_HEADS // N_KV_HEADS
ATTN_WIDTH = N_Q_HEADS * HEAD_DIM
KV_WIDTH = N_KV_HEADS * HEAD_DIM
LRU_WIDTH = D_MODEL - ATTN_WIDTH
LRU_BLOCKS = 8
LRU_BLOCK = LRU_WIDTH // LRU_BLOCKS
LRU_CONV_W = 4
LRU_C = 8.0
IN_WIDTH = ATTN_WIDTH + 2 * KV_WIDTH + 2 * LRU_WIDTH
FFN_DIM = 5632
FFN_CONV_W = 3
ROPE_THETA = 10000.0
EPS = 1e-6
ATTN_SCALE = HEAD_DIM ** -0.5

LANES = 128
SUBLANES_F32 = 8
SUBLANES_BF16 = 16
VMEM_LIMIT_BYTES = 56 * 1024 * 1024

F32 = jnp.float32
BF16 = jnp.bfloat16


def _rms(x, g):
    ms = jnp.mean(x * x, axis=-1, keepdims=True)
    return x * lax.rsqrt(ms + EPS) * g


def _dot(a, b):
    return jnp.dot(a, b, preferred_element_type=F32)


def _const_spec(shape):
    zeros = (0,) * len(shape)
    return pl.BlockSpec(shape, lambda *_: zeros)


def _in_proj_kernel(x_ref, g_ref, w_ref, qg_ref, kg_ref, cos_ref, sin_ref,
                    q_ref, k_ref, v_ref, xr_ref, gy_ref):
    xn = _rms(x_ref[...], g_ref[...]).astype(BF16)
    cos = cos_ref[...]
    sin = sin_ref[...]

    def rope_head(zh, gain):
        y = _rms(zh, gain)
        return y * cos + pltpu.roll(y, HEAD_DIM // 2, axis=1) * sin

    zq = _dot(xn, w_ref[:, 0:ATTN_WIDTH])
    for h in range(N_Q_HEADS):
        sl = slice(h * HEAD_DIM, (h + 1) * HEAD_DIM)
        q_ref[:, sl] = (rope_head(zq[:, sl], qg_ref[...]) * ATTN_SCALE).astype(BF16)
    zkv = _dot(xn, w_ref[:, ATTN_WIDTH:ATTN_WIDTH + 2 * KV_WIDTH])
    for h in range(N_KV_HEADS):
        sl = slice(h * HEAD_DIM, (h + 1) * HEAD_DIM)
        k_ref[:, sl] = rope_head(zkv[:, sl], kg_ref[...]).astype(BF16)
    v_ref[...] = zkv[:, KV_WIDTH:2 * KV_WIDTH].astype(BF16)
    o3 = ATTN_WIDTH + 2 * KV_WIDTH
    xr_ref[...] = _dot(xn, w_ref[:, o3:o3 + LRU_WIDTH])
    gy_ref[...] = _dot(xn, w_ref[:, o3 + LRU_WIDTH:o3 + 2 * LRU_WIDTH])


def _in_proj(x2d, gain, w_in, q_gain, k_gain, cos_t, sin_t, tm):
    rows = x2d.shape[0]
    n_tab = cos_t.shape[0] // tm
    row_spec = lambda w: pl.BlockSpec((tm, w), lambda i: (i, 0))
    tab_spec = pl.BlockSpec((tm, HEAD_DIM), lambda i: (i % n_tab, 0))
    return pl.pallas_call(
        _in_proj_kernel,
        grid=(rows // tm,),
        in_specs=[row_spec(D_MODEL), _const_spec((1, D_MODEL)),
                  pl.BlockSpec((D_MODEL, IN_WIDTH), lambda i: (0, 0), pipeline_mode=pl.Buffered(1)),
                  _const_spec((1, HEAD_DIM)), _const_spec((1, HEAD_DIM)), tab_spec, tab_spec],
        out_specs=[row_spec(ATTN_WIDTH), row_spec(KV_WIDTH), row_spec(KV_WIDTH),
                   row_spec(LRU_WIDTH), row_spec(LRU_WIDTH)],
        out_shape=[jax.ShapeDtypeStruct((rows, ATTN_WIDTH), BF16),
                   jax.ShapeDtypeStruct((rows, KV_WIDTH), BF16),
                   jax.ShapeDtypeStruct((rows, KV_WIDTH), BF16),
                   jax.ShapeDtypeStruct((rows, LRU_WIDTH), F32),
                   jax.ShapeDtypeStruct((rows, LRU_WIDTH), F32)],
        compiler_params=pltpu.CompilerParams(dimension_semantics=("parallel",),
                                             vmem_limit_bytes=VMEM_LIMIT_BYTES),
        name="in_proj",
    )(x2d, gain, w_in, q_gain, k_gain, cos_t, sin_t)


def _attn_kernel(q_ref, k_ref, v_ref, km_ref, vm_ref, o_ref, m_ref, l_ref, acc_ref, *, nkv):
    j = pl.program_id(3)
    tq = q_ref.shape[1]

    def step(kb, vb, mask):
        for g in range(Q_PER_KV):
            q = q_ref[0, :, g * HEAD_DIM:(g + 1) * HEAD_DIM]
            s = lax.dot_general(q, kb, (((1,), (1,)), ((), ())), preferred_element_type=F32)
            if mask is not None:
                s = jnp.where(mask, s, -jnp.inf)
            m_prev = m_ref[g]
            m_new = jnp.maximum(m_prev, jnp.max(s, axis=-1, keepdims=True))
            alpha = jnp.exp(m_prev - m_new)
            p = jnp.exp(s - m_new[:, :1])
            l_ref[g] = alpha * l_ref[g] + jnp.sum(p, axis=-1, keepdims=True)
            acc_ref[g] = alpha * acc_ref[g] + _dot(p.astype(BF16), vb)
            m_ref[g] = m_new

    @pl.when(j == 0)
    def _():
        m_ref[...] = jnp.full(m_ref.shape, -jnp.inf, F32)
        l_ref[...] = jnp.zeros(l_ref.shape, F32)
        acc_ref[...] = jnp.zeros(acc_ref.shape, F32)
        valid = lax.broadcasted_iota(jnp.int32, (tq, km_ref.shape[0]), 1) < N_META
        step(km_ref[...], vm_ref[...], valid)

    @pl.when(j > 0)
    def _():
        step(k_ref[0], v_ref[0], None)

    @pl.when(j == nkv)
    def _():
        for g in range(Q_PER_KV):
            o_ref[0, :, g * HEAD_DIM:(g + 1) * HEAD_DIM] = (acc_ref[g] / l_ref[g]).astype(BF16)


def _attention(q, k, v, km, vm, tq, tk):
    b, nq, _ = q.shape
    n = k.shape[1]
    nkv = n // tk
    gw = Q_PER_KV * HEAD_DIM
    return pl.pallas_call(
        functools.partial(_attn_kernel, nkv=nkv),
        grid=(b, N_KV_HEADS, nq // tq, nkv + 1),
        in_specs=[pl.BlockSpec((1, tq, gw), lambda bi, h, i, j: (bi, i, h)),
                  pl.BlockSpec((1, tk, HEAD_DIM), lambda bi, h, i, j: (bi, jnp.maximum(j - 1, 0), h)),
                  pl.BlockSpec((1, tk, HEAD_DIM), lambda bi, h, i, j: (bi, jnp.maximum(j - 1, 0), h)),
                  pl.BlockSpec((km.shape[0], HEAD_DIM), lambda bi, h, i, j: (0, h)),
                  pl.BlockSpec((vm.shape[0], HEAD_DIM), lambda bi, h, i, j: (0, h))],
        out_specs=pl.BlockSpec((1, tq, gw), lambda bi, h, i, j: (bi, i, h)),
        out_shape=jax.ShapeDtypeStruct((b, nq, ATTN_WIDTH), BF16),
        scratch_shapes=[pltpu.VMEM((Q_PER_KV, tq, LANES), F32),
                        pltpu.VMEM((Q_PER_KV, tq, LANES), F32),
                        pltpu.VMEM((Q_PER_KV, tq, HEAD_DIM), F32)],
        compiler_params=pltpu.CompilerParams(
            dimension_semantics=("parallel", "parallel", "parallel", "arbitrary"),
            vmem_limit_bytes=VMEM_LIMIT_BYTES),
        name="attention",
    )(q, k, v, km, vm)


def _softplus(x):
    return jnp.maximum(x, 0.0) + jnp.log1p(jnp.exp(-jnp.abs(x)))


def _one_minus_exp(y, exp_y):
    series = -y * (1.0 + y * (0.5 + y * (1.0 / 6.0 + y * (1.0 / 24.0 + y * (1.0 / 120.0)))))
    return jnp.where(y > -0.03, series, 1.0 - exp_y)


def _lru_prep(ext, rows, cw_ref, cb_ref, wr_ref, br_ref, wi_ref, bi_ref, lam_ref, a_ref, u_ref):
    n_ext = rows + 2 * SUBLANES_F32
    xc = jnp.broadcast_to(cb_ref[...], (rows, LRU_WIDTH))
    for kk in range(LRU_CONV_W):
        shift = (LRU_CONV_W // 2 - kk) % n_ext
        ek = ext if shift == 0 else pltpu.roll(ext, shift, axis=0)
        xc = xc + ek[SUBLANES_F32:SUBLANES_F32 + rows] * cw_ref[kk:kk + 1, :]
    xcb = xc.astype(BF16)
    r_parts, i_parts = [], []
    for blk in range(LRU_BLOCKS):
        xb = xcb[:, blk * LRU_BLOCK:(blk + 1) * LRU_BLOCK]
        r_parts.append(_dot(xb, wr_ref[blk]))
        i_parts.append(_dot(xb, wi_ref[blk]))
    r = jax.nn.sigmoid(jnp.concatenate(r_parts, axis=-1) + br_ref[...])
    i = jax.nn.sigmoid(jnp.concatenate(i_parts, axis=-1) + bi_ref[...])
    log_a = (-LRU_C) * r * _softplus(-lam_ref[...])
    a = jnp.exp(log_a)
    a_ref[0:rows, :] = a
    u_ref[0:rows, :] = jnp.sqrt(_one_minus_exp(2.0 * log_a, a * a)) * (i * xc)


def _lru_scan(rows, a_ref, u_ref, h_ref, out_ref, reverse):
    def body(s, h):
        t = rows - 1 - s if reverse else s
        h = a_ref[pl.ds(t, 1), :] * h + u_ref[pl.ds(t, 1), :]
        out_ref[pl.ds(t, 1), :] = h
        return h
    h_ref[...] = lax.fori_loop(0, rows, body, h_ref[...], unroll=8)


def _lru_fwd_kernel(xr_ref, prev_ref, next_ref, xm_ref, cw_ref, cb_ref, wr_ref, br_ref, wi_ref, bi_ref,
                    lam_ref, hf_ref, hfm_ref, a_ref, u_ref, h_ref, *, nc):
    j = pl.program_id(1)
    tc = xr_ref.shape[1]
    params = (cw_ref, cb_ref, wr_ref, br_ref, wi_ref, bi_ref, lam_ref, a_ref, u_ref)

    @pl.when(j == 0)
    def _():
        h_ref[...] = jnp.zeros(h_ref.shape, F32)
        ext = jnp.concatenate([jnp.zeros((SUBLANES_F32, LRU_WIDTH), F32), xm_ref[...], next_ref[0]], axis=0)
        _lru_prep(ext, N_META, *params)
        _lru_scan(N_META, a_ref, u_ref, h_ref, hfm_ref.at[0], False)

    @pl.when(j > 0)
    def _():
        prev = jnp.where(j == 1, xm_ref[N_META - SUBLANES_F32:N_META, :], prev_ref[0])
        nxt = jnp.where(j == nc, 0.0, next_ref[0])
        ext = jnp.concatenate([prev, xr_ref[0], nxt], axis=0)
        _lru_prep(ext, tc, *params)
        _lru_scan(tc, a_ref, u_ref, h_ref, hf_ref.at[0], False)


def _lru_bwd_kernel(xr_ref, prev_ref, next_ref, xm_ref, gy_ref, gym_ref, hf_ref, hfm_ref,
                    cw_ref, cb_ref, wr_ref, br_ref, wi_ref, bi_ref, lam_ref, og_ref,
                    out_ref, outm_ref, a_ref, u_ref, h_ref, hb_ref, *, nc):
    j = pl.program_id(1)
    tc = xr_ref.shape[1]
    params = (cw_ref, cb_ref, wr_ref, br_ref, wi_ref, bi_ref, lam_ref, a_ref, u_ref)

    def finish(rows, hf, gy, dst_ref):
        lru = (hf + hb_ref[0:rows, :]) * jax.nn.gelu(gy)
        dst_ref[0] = _rms(lru, og_ref[...]).astype(BF16)

    @pl.when(j == 0)
    def _():
        h_ref[...] = jnp.zeros(h_ref.shape, F32)

    @pl.when(j < nc)
    def _():
        prev = jnp.where(j == nc - 1, xm_ref[N_META - SUBLANES_F32:N_META, :], prev_ref[0])
        nxt = jnp.where(j == 0, 0.0, next_ref[0])
        ext = jnp.concatenate([prev, xr_ref[0], nxt], axis=0)
        _lru_prep(ext, tc, *params)
        _lru_scan(tc, a_ref, u_ref, h_ref, hb_ref, True)
        finish(tc, hf_ref[0], gy_ref[0], out_ref)

    @pl.when(j == nc)
    def _():
        ext = jnp.concatenate([jnp.zeros((SUBLANES_F32, LRU_WIDTH), F32), xm_ref[...], next_ref[0]], axis=0)
        _lru_prep(ext, N_META, *params)
        _lru_scan(N_META, a_ref, u_ref, h_ref, hb_ref, True)
        finish(N_META, hfm_ref[0], gym_ref[...], outm_ref)


def _lru(xr, gy, xm, gym, conv_w, conv_b, w_r, b_r, w_i, b_i, lam, out_gain, tc):
    b, n, c = xr.shape
    nc = n // tc
    sub = SUBLANES_F32
    nsub = n // sub
    tcs = tc // sub
    wspec = _const_spec((LRU_BLOCKS, LRU_BLOCK, LRU_BLOCK))
    vspec = _const_spec((1, c))
    pspecs = [_const_spec((LRU_CONV_W, c)), vspec, wspec, vspec, wspec, vspec, vspec]
    mspec = _const_spec((N_META, c))

    def chunk_f(j):
        return jnp.maximum(j - 1, 0)

    hf, hfm = pl.pallas_call(
        functools.partial(_lru_fwd_kernel, nc=nc),
        grid=(b, nc + 1),
        in_specs=[pl.BlockSpec((1, tc, c), lambda bi, j: (bi, chunk_f(j), 0)),
                  pl.BlockSpec((1, sub, c), lambda bi, j: (bi, jnp.maximum(chunk_f(j) * tcs - 1, 0), 0)),
                  pl.BlockSpec((1, sub, c), lambda bi, j: (bi, jnp.minimum(j * tcs, nsub - 1), 0)),
                  mspec] + pspecs,
        out_specs=[pl.BlockSpec((1, tc, c), lambda bi, j: (bi, chunk_f(j), 0)),
                   pl.BlockSpec((1, N_META, c), lambda bi, j: (bi, 0, 0))],
        out_shape=[jax.ShapeDtypeStruct((b, n, c), F32), jax.ShapeDtypeStruct((b, N_META, c), F32)],
        scratch_shapes=[pltpu.VMEM((tc, c), F32), pltpu.VMEM((tc, c), F32), pltpu.VMEM((1, c), F32)],
        compiler_params=pltpu.CompilerParams(dimension_semantics=("parallel", "arbitrary"),
                                             vmem_limit_bytes=VMEM_LIMIT_BYTES),
        name="lru_fwd",
    )(xr, xr, xr, xm, conv_w, conv_b, w_r[0], b_r[0:1], w_i[0], b_i[0:1], lam[0:1])

    def chunk_b(j):
        return jnp.maximum(nc - 1 - j, 0)

    def next_b(j):
        return jnp.where(j == nc, 0, jnp.minimum((chunk_b(j) + 1) * tcs, nsub - 1))

    out, outm = pl.pallas_call(
        functools.partial(_lru_bwd_kernel, nc=nc),
        grid=(b, nc + 1),
        in_specs=[pl.BlockSpec((1, tc, c), lambda bi, j: (bi, chunk_b(j), 0)),
                  pl.BlockSpec((1, sub, c), lambda bi, j: (bi, jnp.maximum(chunk_b(j) * tcs - 1, 0), 0)),
                  pl.BlockSpec((1, sub, c), lambda bi, j: (bi, next_b(j), 0)),
                  mspec,
                  pl.BlockSpec((1, tc, c), lambda bi, j: (bi, chunk_b(j), 0)),
                  mspec,
                  pl.BlockSpec((1, tc, c), lambda bi, j: (bi, chunk_b(j), 0)),
                  pl.BlockSpec((1, N_META, c), lambda bi, j: (bi, 0, 0))] + pspecs + [vspec],
        out_specs=[pl.BlockSpec((1, tc, c), lambda bi, j: (bi, chunk_b(j), 0)),
                   pl.BlockSpec((1, N_META, c), lambda bi, j: (bi, 0, 0))],
        out_shape=[jax.ShapeDtypeStruct((b, n, c), BF16), jax.ShapeDtypeStruct((b, N_META, c), BF16)],
        scratch_shapes=[pltpu.VMEM((tc, c), F32), pltpu.VMEM((tc, c), F32), pltpu.VMEM((1, c), F32),
                        pltpu.VMEM((tc, c), F32)],
        compiler_params=pltpu.CompilerParams(dimension_semantics=("parallel", "arbitrary"),
                                             vmem_limit_bytes=VMEM_LIMIT_BYTES),
        name="lru_bwd",
    )(xr, xr, xr, xm, gy, gym, hf, hfm, conv_w, conv_b, w_r[1], b_r[1:2], w_i[1], b_i[1:2], lam[1:2],
      out_gain)
    return out, outm


def _out_proj_kernel(attn_ref, lru_ref, res_ref, ag_ref, wa_ref, wl_ref, pg_ref, fg_ref, h1_ref, hn_ref):
    an = _rms(attn_ref[...].astype(F32), ag_ref[...]).astype(BF16)
    mixed = _dot(an, wa_ref[...]) + _dot(lru_ref[...], wl_ref[...])
    h1 = res_ref[...] + _rms(mixed, pg_ref[...])
    h1_ref[...] = h1
    hn_ref[...] = _rms(h1, fg_ref[...]).astype(BF16)


def _out_proj(attn2d, lru2d, res2d, attn_gain, w_a, w_l, post_gain, ffn_gain, tm):
    rows = attn2d.shape[0]
    row_spec = lambda w: pl.BlockSpec((tm, w), lambda i: (i, 0))
    wspec = pl.BlockSpec((ATTN_WIDTH, D_MODEL), lambda i: (0, 0), pipeline_mode=pl.Buffered(1))
    return pl.pallas_call(
        _out_proj_kernel,
        grid=(rows // tm,),
        in_specs=[row_spec(ATTN_WIDTH), row_spec(LRU_WIDTH), row_spec(D_MODEL), _const_spec((1, ATTN_WIDTH)),
                  wspec, wspec, _const_spec((1, D_MODEL)), _const_spec((1, D_MODEL))],
        out_specs=[row_spec(D_MODEL), row_spec(D_MODEL)],
        out_shape=[jax.ShapeDtypeStruct((rows, D_MODEL), F32), jax.ShapeDtypeStruct((rows, D_MODEL), BF16)],
        compiler_params=pltpu.CompilerParams(dimension_semantics=("parallel",),
                                             vmem_limit_bytes=VMEM_LIMIT_BYTES),
        name="out_proj",
    )(attn2d, lru2d, res2d, attn_gain, w_a, w_l, post_gain, ffn_gain)


def _ffn_kernel(hn_ref, prev_ref, next_ref, hm_ref, h1_ref, wg_ref, wv_ref, wd_ref, cw_ref, cb_ref, og_ref,
                out_ref, ext_ref, acc_ref, *, nt, nf):
    i = pl.program_id(1)
    f = pl.program_id(2)
    tm = hn_ref.shape[1]
    halo = SUBLANES_BF16

    @pl.when(f == 0)
    def _():
        ext_ref[0:halo, :] = jnp.where(i == 0, hm_ref[0], prev_ref[0])
        ext_ref[halo:halo + tm, :] = hn_ref[0]
        ext_ref[halo + tm:2 * halo + tm, :] = jnp.where(i == nt - 1, jnp.zeros_like(next_ref[0]), next_ref[0])
        acc_ref[...] = jnp.zeros(acc_ref.shape, F32)

    n_ext = tm + 2 * halo
    gfull = _dot(ext_ref[...], wg_ref[...])
    g = jnp.broadcast_to(cb_ref[...], (tm, gfull.shape[1]))
    for kk in range(FFN_CONV_W):
        shift = (FFN_CONV_W // 2 - kk) % n_ext
        gk = gfull if shift == 0 else pltpu.roll(gfull, shift, axis=0)
        g = g + gk[halo:halo + tm] * cw_ref[kk:kk + 1, :]
    val = _dot(ext_ref[halo:halo + tm, :], wv_ref[...])
    act = (jax.nn.silu(g) * val).astype(BF16)
    acc_ref[...] += _dot(act, wd_ref[...])

    @pl.when(f == nf - 1)
    def _():
        out_ref[0] = h1_ref[0] + _rms(acc_ref[...], og_ref[...])


def _ffn(hn, hnm, h1, w_up, w_down, conv_w, conv_b, out_gain, tm, fc):
    b, n, d = hn.shape
    nt = n // tm
    nf = FFN_DIM // fc
    halo = SUBLANES_BF16
    th = tm // halo
    nh = n // halo
    return pl.pallas_call(
        functools.partial(_ffn_kernel, nt=nt, nf=nf),
        grid=(b, nt, nf),
        in_specs=[pl.BlockSpec((1, tm, d), lambda bi, i, f: (bi, i, 0)),
                  pl.BlockSpec((1, halo, d), lambda bi, i, f: (bi, jnp.maximum(i * th - 1, 0), 0)),
                  pl.BlockSpec((1, halo, d), lambda bi, i, f: (bi, jnp.minimum((i + 1) * th, nh - 1), 0)),
                  pl.BlockSpec((1, N_META, d), lambda bi, i, f: (bi, 0, 0)),
                  pl.BlockSpec((1, tm, d), lambda bi, i, f: (bi, i, 0)),
                  pl.BlockSpec((d, fc), lambda bi, i, f: (0, f)),
                  pl.BlockSpec((d, fc), lambda bi, i, f: (0, f + nf)),
                  pl.BlockSpec((fc, d), lambda bi, i, f: (f, 0)),
                  pl.BlockSpec((FFN_CONV_W, fc), lambda bi, i, f: (0, f)),
                  pl.BlockSpec((1, fc), lambda bi, i, f: (0, f)),
                  _const_spec((1, d))],
        out_specs=pl.BlockSpec((1, tm, d), lambda bi, i, f: (bi, i, 0)),
        out_shape=jax.ShapeDtypeStruct((b, n, d), F32),
        scratch_shapes=[pltpu.VMEM((tm + 2 * halo, d), BF16), pltpu.VMEM((tm, d), F32)],
        compiler_params=pltpu.CompilerParams(
            dimension_semantics=("parallel", "parallel", "arbitrary"),
            vmem_limit_bytes=VMEM_LIMIT_BYTES),
        name="conv_ffn",
    )(hn, hn, hn, hnm, h1, w_up, w_up, w_down, conv_w, conv_b, out_gain)


def _rope_tables(n):
    rows = n // GRID_W
    t_row = jnp.repeat(jnp.arange(rows), GRID_W).astype(F32)
    t_col = jnp.tile(jnp.arange(GRID_W), rows).astype(F32)
    half = HEAD_DIM // 2
    inv = ROPE_THETA ** (-jnp.arange(0, half, 2, dtype=F32) / half)
    ang = jnp.concatenate([t_row[:, None] * inv, t_col[:, None] * inv], axis=-1)
    cos, sin = jnp.cos(ang), jnp.sin(ang)
    return jnp.concatenate([cos, cos], axis=-1), jnp.concatenate([-sin, sin], axis=-1)


_TILES = dict(tm=512, tq=512, tk=512, tc=512, tf=512, fc=512)


def _pick(n, pref):
    t = pref
    while n % t:
        t //= 2
    return t


def _trunk(x, meta, mp, p):
    b, n, d = x.shape
    rows = b * n
    cos_t, sin_t = _rope_tables(n)
    tm = _pick(n, _TILES["tm"])
    q, k, v, xr, gy = _in_proj(x.reshape(rows, d), p["norm_pre_mix"], p["w_in"], p["q_norm"], p["k_norm"],
                               cos_t, sin_t, tm)
    q = q.reshape(b, n, ATTN_WIDTH)
    k = k.reshape(b, n, KV_WIDTH)
    v = v.reshape(b, n, KV_WIDTH)
    tq = _pick(n, _TILES["tq"])
    tk = _pick(n, _TILES["tk"])
    attn = _attention(q, k, v, mp["km"], mp["vm"], tq, tk)
    qm = jnp.broadcast_to(mp["q"][None], (b, N_META, ATTN_WIDTH))
    attn_m = _attention(qm, k, v, mp["km"], mp["vm"], N_META, tk)
    lru, lru_m = _lru(xr.reshape(b, n, LRU_WIDTH), gy.reshape(b, n, LRU_WIDTH), mp["xr"], mp["gy"],
                      p["lru_conv_w"], p["lru_conv_b"], p["lru_w_r"], p["lru_b_r"], p["lru_w_i"], p["lru_b_i"],
                      p["lru_lambda"], p["lru_out_norm"], _pick(n, _TILES["tc"]))
    op = (p["attn_out_norm"], p["w_out_a"], p["w_out_l"], p["norm_post_mix"], p["norm_pre_ffn"])
    h1, hn = _out_proj(attn.reshape(rows, ATTN_WIDTH), lru.reshape(rows, LRU_WIDTH), x.reshape(rows, d), *op, tm)
    res_m = jnp.broadcast_to(meta[None], (b, N_META, d)).reshape(b * N_META, d)
    _, hn_m = _out_proj(attn_m.reshape(b * N_META, ATTN_WIDTH), lru_m.reshape(b * N_META, LRU_WIDTH), res_m,
                        *op, N_META)
    return _ffn(hn.reshape(b, n, d), hn_m.reshape(b, N_META, d), h1.reshape(b, n, d), p["w_up"], p["w_down"],
                p["ffn_conv_w"], p["ffn_conv_b"], p["norm_post_ffn"], _pick(n, _TILES["tf"]), _TILES["fc"])


def _head_perm():
    q4 = HEAD_DIM // 4
    idx = jnp.arange(HEAD_DIM).reshape(2, 2, q4)
    return idx.transpose(1, 0, 2).reshape(HEAD_DIM)


def kernel(x_prompt, x_sample, meta_tokens, norm_pre_mix, w_in, q_norm, k_norm, lru_conv_w, lru_conv_b,
           lru_w_r, lru_b_r, lru_w_i, lru_b_i, lru_lambda, attn_out_norm, lru_out_norm, w_out,
           norm_post_mix, norm_pre_ffn, w_up, ffn_conv_w, ffn_conv_b, w_down, norm_post_ffn):
    perm = _head_perm()
    n_rot = N_Q_HEADS + N_KV_HEADS
    cols = (jnp.arange(n_rot)[:, None] * HEAD_DIM + perm[None, :]).reshape(-1)
    cols = jnp.concatenate([cols, jnp.arange(n_rot * HEAD_DIM, IN_WIDTH)])
    w_out_b = w_out[0].astype(BF16)
    p = {
        "norm_pre_mix": norm_pre_mix[0][None], "w_in": w_in[0][:, cols].astype(BF16),
        "q_norm": q_norm[0][perm][None], "k_norm": k_norm[0][perm][None],
        "lru_conv_w": lru_conv_w[0], "lru_conv_b": lru_conv_b[0][None],
        "lru_w_r": lru_w_r[0].astype(BF16), "lru_b_r": lru_b_r[0],
        "lru_w_i": lru_w_i[0].astype(BF16), "lru_b_i": lru_b_i[0], "lru_lambda": lru_lambda[0],
        "attn_out_norm": attn_out_norm[0][None], "lru_out_norm": lru_out_norm[0][None],
        "w_out_a": w_out_b[:ATTN_WIDTH], "w_out_l": w_out_b[ATTN_WIDTH:],
        "norm_post_mix": norm_post_mix[0][None], "norm_pre_ffn": norm_pre_ffn[0][None],
        "w_up": w_up[0].astype(BF16), "ffn_conv_w": ffn_conv_w[0], "ffn_conv_b": ffn_conv_b[0][None],
        "w_down": w_down[0].astype(BF16), "norm_post_ffn": norm_post_ffn[0][None],
    }
    ones = jnp.ones((N_META, HEAD_DIM), F32)
    qm, km, vm, xrm, gym = _in_proj(meta_tokens, p["norm_pre_mix"], p["w_in"], p["q_norm"], p["k_norm"],
                                    ones, jnp.zeros_like(ones), N_META)
    pad = ((0, LANES - N_META), (0, 0))
    mp = {"q": qm, "km": jnp.pad(km, pad), "vm": jnp.pad(vm, pad), "xr": xrm, "gy": gym}
    return _trunk(x_prompt, meta_tokens, mp, p), _trunk(x_sample, meta_tokens, mp, p)
```

```python
import functools

import jax
import jax.numpy as jnp
from jax import lax
from jax.experimental import pallas as pl
from jax.experimental.pallas import tpu as pltpu

D_MODEL = 2048
N_META = 16
GRID_W = 64
HEAD_DIM = 128
N_Q_HEADS = 8
N_KV_HEADS = 2
Q_PER_KV = N_Q_HEADS // N_KV_HEADS
ATTN_WIDTH = N_Q_HEADS * HEAD_DIM
KV_WIDTH = N_KV_HEADS * HEAD_DIM
LRU_WIDTH = D_MODEL - ATTN_WIDTH
LRU_BLOCKS = 8
LRU_BLOCK = LRU_WIDTH // LRU_BLOCKS
LRU_CONV_W = 4
LRU_C = 8.0
IN_WIDTH = ATTN_WIDTH + 2 * KV_WIDTH + 2 * LRU_WIDTH
FFN_DIM = 5632
FFN_CONV_W = 3
ROPE_THETA = 10000.0
EPS = 1e-6
ATTN_SCALE = HEAD_DIM ** -0.5
LOG2E = 1.4426950408889634
Q_PRESCALE = ATTN_SCALE * LOG2E
MAX_FIXED_SHIFT_RANGE = 120.0
BF16_ROUNDING_SLACK = 1.0 + 2.0 ** -6

LANES = 128
SUBLANES_F32 = 8
SUBLANES_BF16 = 16
VMEM_LIMIT_BYTES = 56 * 1024 * 1024

F32 = jnp.float32
BF16 = jnp.bfloat16


def _rms(x, g):
    ms = jnp.mean(x * x, axis=-1, keepdims=True)
    return x * lax.rsqrt(ms + EPS) * g


def _dot(a, b):
    return jnp.dot(a, b, preferred_element_type=F32)


def _const_spec(shape):
    zeros = (0,) * len(shape)
    return pl.BlockSpec(shape, lambda *_: zeros)


def _in_proj_kernel(x_ref, g_ref, w_ref, qg_ref, kg_ref, cos_ref, sin_ref,
                    q_ref, k_ref, vt_ref, xr_ref, gy_ref):
    xn = _rms(x_ref[...], g_ref[...]).astype(BF16)
    cos = cos_ref[...]
    sin = sin_ref[...]

    def rope_head(zh, gain):
        y = _rms(zh, gain)
        return y * cos + pltpu.roll(y, HEAD_DIM // 2, axis=1) * sin

    zq = _dot(xn, w_ref[:, 0:ATTN_WIDTH])
    for h in range(N_Q_HEADS):
        sl = slice(h * HEAD_DIM, (h + 1) * HEAD_DIM)
        q_ref[h] = (rope_head(zq[:, sl], qg_ref[...]) * Q_PRESCALE).astype(BF16)
    zkv = _dot(xn, w_ref[:, ATTN_WIDTH:ATTN_WIDTH + 2 * KV_WIDTH])
    for h in range(N_KV_HEADS):
        sl = slice(h * HEAD_DIM, (h + 1) * HEAD_DIM)
        k_ref[h] = rope_head(zkv[:, sl], kg_ref[...]).astype(BF16)
    vt_ref[...] = zkv[:, KV_WIDTH:2 * KV_WIDTH].T.astype(BF16)
    o3 = ATTN_WIDTH + 2 * KV_WIDTH
    xr_ref[...] = _dot(xn, w_ref[:, o3:o3 + LRU_WIDTH])
    gy_ref[...] = _dot(xn, w_ref[:, o3 + LRU_WIDTH:o3 + 2 * LRU_WIDTH])


def _in_proj(x2d, gain, w_in, q_gain, k_gain, cos_t, sin_t, tm):
    rows = x2d.shape[0]
    n_tab = cos_t.shape[0] // tm
    row_spec = lambda w: pl.BlockSpec((tm, w), lambda i: (i, 0))
    tab_spec = pl.BlockSpec((tm, HEAD_DIM), lambda i: (i % n_tab, 0))
    return pl.pallas_call(
        _in_proj_kernel,
        grid=(rows // tm,),
        in_specs=[row_spec(D_MODEL), _const_spec((1, D_MODEL)),
                  pl.BlockSpec((D_MODEL, IN_WIDTH), lambda i: (0, 0), pipeline_mode=pl.Buffered(1)),
                  _const_spec((1, HEAD_DIM)), _const_spec((1, HEAD_DIM)), tab_spec, tab_spec],
        out_specs=[pl.BlockSpec((N_Q_HEADS, tm, HEAD_DIM), lambda i: (0, i, 0)),
                   pl.BlockSpec((N_KV_HEADS, tm, HEAD_DIM), lambda i: (0, i, 0)),
                   pl.BlockSpec((KV_WIDTH, tm), lambda i: (0, i)),
                   row_spec(LRU_WIDTH), row_spec(LRU_WIDTH)],
        out_shape=[jax.ShapeDtypeStruct((N_Q_HEADS, rows, HEAD_DIM), BF16),
                   jax.ShapeDtypeStruct((N_KV_HEADS, rows, HEAD_DIM), BF16),
                   jax.ShapeDtypeStruct((KV_WIDTH, rows), BF16),
                   jax.ShapeDtypeStruct((rows, LRU_WIDTH), F32),
                   jax.ShapeDtypeStruct((rows, LRU_WIDTH), F32)],
        compiler_params=pltpu.CompilerParams(dimension_semantics=("parallel",),
                                             vmem_limit_bytes=VMEM_LIMIT_BYTES),
        name="in_proj",
    )(x2d, gain, w_in, q_gain, k_gain, cos_t, sin_t)


def _attn_kernel(bound_ref, q_ref, k_ref, vt_ref, km_ref, vmt_ref, o_ref, l_ref, acc_ref, *m_scratch,
                 nkv, tk, running_max):
    tq = q_ref.shape[1]
    width = Q_PER_KV * tq
    qs = q_ref[...].reshape(width, HEAD_DIM)
    nt = (((1,), (1,)), ((), ()))

    def block(kb, vtb, first):
        st = lax.dot_general(kb, qs, nt, preferred_element_type=F32)
        if running_max:
            m_ref, = m_scratch
            m_new = jnp.max(st, axis=0, keepdims=True)
            if not first:
                m_prev = m_ref[...]
                m_new = jnp.maximum(m_prev, m_new)
                alpha = jnp.exp2(m_prev - m_new)
            m_ref[...] = m_new
            p = jnp.exp2(st - m_new)
        else:
            p = jnp.exp2(st - bound_ref[0])
        psum = jnp.sum(p.reshape(p.shape[0] // SUBLANES_F32, SUBLANES_F32, width), axis=0)
        pv = _dot(vtb, p.astype(BF16))
        if first:
            l_ref[...] = psum
            acc_ref[...] = pv
        elif running_max:
            l_ref[...] = alpha * l_ref[...] + psum
            acc_ref[...] = alpha * acc_ref[...] + pv
        else:
            l_ref[...] += psum
            acc_ref[...] += pv

    block(km_ref[0], vmt_ref[...], True)

    def body(j, carry):
        start = pl.multiple_of(j * tk, tk)
        block(k_ref[0, pl.ds(start, tk), :], vt_ref[:, pl.ds(start, tk)], False)
        return carry

    lax.fori_loop(0, nkv, body, 0)

    o = acc_ref[...] * (1.0 / jnp.sum(l_ref[...], axis=0, keepdims=True))
    for g in range(Q_PER_KV):
        o_ref[0, :, g * HEAD_DIM:(g + 1) * HEAD_DIM] = o[:, g * tq:(g + 1) * tq].T.astype(BF16)


def _attention(bound, q, k, vt, km, vmt, batch, tq, tk, shared_q, running_max):
    n = k.shape[1] // batch
    nq = q.shape[1] if shared_q else q.shape[1] // batch
    nkv = n // tk
    nqt = nq // tq
    width = Q_PER_KV * tq
    if shared_q:
        q_map = lambda bi, h, i: (h, i, 0)
    else:
        q_map = lambda bi, h, i: (h, bi * nqt + i, 0)
    scratch = [pltpu.VMEM((SUBLANES_F32, width), F32), pltpu.VMEM((HEAD_DIM, width), F32)]
    if running_max:
        scratch.append(pltpu.VMEM((1, width), F32))
    return pl.pallas_call(
        functools.partial(_attn_kernel, nkv=nkv, tk=tk, running_max=running_max),
        grid=(batch, N_KV_HEADS, nqt),
        in_specs=[pl.BlockSpec(memory_space=pltpu.SMEM),
                  pl.BlockSpec((Q_PER_KV, tq, HEAD_DIM), q_map),
                  pl.BlockSpec((1, n, HEAD_DIM), lambda bi, h, i: (h, bi, 0)),
                  pl.BlockSpec((HEAD_DIM, n), lambda bi, h, i: (h, bi)),
                  pl.BlockSpec((1, N_META, HEAD_DIM), lambda bi, h, i: (h, 0, 0)),
                  pl.BlockSpec((HEAD_DIM, N_META), lambda bi, h, i: (h, 0))],
        out_specs=pl.BlockSpec((1, tq, Q_PER_KV * HEAD_DIM), lambda bi, h, i: (bi, i, h)),
        out_shape=jax.ShapeDtypeStruct((batch, nq, ATTN_WIDTH), BF16),
        scratch_shapes=scratch,
        compiler_params=pltpu.CompilerParams(
            dimension_semantics=("parallel", "parallel", "arbitrary"),
            vmem_limit_bytes=VMEM_LIMIT_BYTES),
        name="attention",
    )(bound, q, k, vt, km, vmt)


def _softplus(x):
    return jnp.maximum(x, 0.0) + jnp.log1p(jnp.exp(-jnp.abs(x)))


def _one_minus_exp(y, exp_y):
    series = -y * (1.0 + y * (0.5 + y * (1.0 / 6.0 + y * (1.0 / 24.0 + y * (1.0 / 120.0)))))
    return jnp.where(y > -0.03, series, 1.0 - exp_y)


def _lru_prep(ext, rows, cw_ref, cb_ref, wr_ref, br_ref, wi_ref, bi_ref, lam_ref, a_ref, u_ref):
    n_ext = rows + 2 * SUBLANES_F32
    xc = jnp.broadcast_to(cb_ref[...], (rows, LRU_WIDTH))
    for kk in range(LRU_CONV_W):
        shift = (LRU_CONV_W // 2 - kk) % n_ext
        ek = ext if shift == 0 else pltpu.roll(ext, shift, axis=0)
        xc = xc + ek[SUBLANES_F32:SUBLANES_F32 + rows] * cw_ref[kk:kk + 1, :]
    xcb = xc.astype(BF16)
    r_parts, i_parts = [], []
    for blk in range(LRU_BLOCKS):
        xb = xcb[:, blk * LRU_BLOCK:(blk + 1) * LRU_BLOCK]
        r_parts.append(_dot(xb, wr_ref[blk]))
        i_parts.append(_dot(xb, wi_ref[blk]))
    r = jax.nn.sigmoid(jnp.concatenate(r_parts, axis=-1) + br_ref[...])
    i = jax.nn.sigmoid(jnp.concatenate(i_parts, axis=-1) + bi_ref[...])
    log_a = (-LRU_C) * r * _softplus(-lam_ref[...])
    a = jnp.exp(log_a)
    a_ref[0:rows, :] = a
    u_ref[0:rows, :] = jnp.sqrt(_one_minus_exp(2.0 * log_a, a * a)) * (i * xc)


def _lru_scan(rows, a_ref, u_ref, h_ref, out_ref, reverse):
    def body(s, h):
        t = rows - 1 - s if reverse else s
        h = a_ref[pl.ds(t, 1), :] * h + u_ref[pl.ds(t, 1), :]
        out_ref[pl.ds(t, 1), :] = h
        return h
    h_ref[...] = lax.fori_loop(0, rows, body, h_ref[...], unroll=8)


def _lru_fwd_kernel(xr_ref, prev_ref, next_ref, xm_ref, cw_ref, cb_ref, wr_ref, br_ref, wi_ref, bi_ref,
                    lam_ref, hf_ref, hfm_ref, a_ref, u_ref, h_ref, *, nc):
    j = pl.program_id(1)
    tc = xr_ref.shape[1]
    params = (cw_ref, cb_ref, wr_ref, br_ref, wi_ref, bi_ref, lam_ref, a_ref, u_ref)

    @pl.when(j == 0)
    def _():
        h_ref[...] = jnp.zeros(h_ref.shape, F32)
        ext = jnp.concatenate([jnp.zeros((SUBLANES_F32, LRU_WIDTH), F32), xm_ref[...], next_ref[0]], axis=0)
        _lru_prep(ext, N_META, *params)
        _lru_scan(N_META, a_ref, u_ref, h_ref, hfm_ref.at[0], False)

    @pl.when(j > 0)
    def _():
        prev = jnp.where(j == 1, xm_ref[N_META - SUBLANES_F32:N_META, :], prev_ref[0])
        nxt = jnp.where(j == nc, 0.0, next_ref[0])
        ext = jnp.concatenate([prev, xr_ref[0], nxt], axis=0)
        _lru_prep(ext, tc, *params)
        _lru_scan(tc, a_ref, u_ref, h_ref, hf_ref.at[0], False)


def _lru_bwd_kernel(xr_ref, prev_ref, next_ref, xm_ref, gy_ref, gym_ref, hf_ref, hfm_ref,
                    cw_ref, cb_ref, wr_ref, br_ref, wi_ref, bi_ref, lam_ref, og_ref,
                    out_ref, outm_ref, a_ref, u_ref, h_ref, hb_ref, *, nc):
    j = pl.program_id(1)
    tc = xr_ref.shape[1]
    params = (cw_ref, cb_ref, wr_ref, br_ref, wi_ref, bi_ref, lam_ref, a_ref, u_ref)

    def finish(rows, hf, gy, dst_ref):
        lru = (hf + hb_ref[0:rows, :]) * jax.nn.gelu(gy)
        dst_ref[0] = _rms(lru, og_ref[...]).astype(BF16)

    @pl.when(j == 0)
    def _():
        h_ref[...] = jnp.zeros(h_ref.shape, F32)

    @pl.when(j < nc)
    def _():
        prev = jnp.where(j == nc - 1, xm_ref[N_META - SUBLANES_F32:N_META, :], prev_ref[0])
        nxt = jnp.where(j == 0, 0.0, next_ref[0])
        ext = jnp.concatenate([prev, xr_ref[0], nxt], axis=0)
        _lru_prep(ext, tc, *params)
        _lru_scan(tc, a_ref, u_ref, h_ref, hb_ref, True)
        finish(tc, hf_ref[0], gy_ref[0], out_ref)

    @pl.when(j == nc)
    def _():
        ext = jnp.concatenate([jnp.zeros((SUBLANES_F32, LRU_WIDTH), F32), xm_ref[...], next_ref[0]], axis=0)
        _lru_prep(ext, N_META, *params)
        _lru_scan(N_META, a_ref, u_ref, h_ref, hb_ref, True)
        finish(N_META, hfm_ref[0], gym_ref[...], outm_ref)


def _lru(xr, gy, xm, gym, conv_w, conv_b, w_r, b_r, w_i, b_i, lam, out_gain, tc):
    b, n, c = xr.shape
    nc = n // tc
    sub = SUBLANES_F32
    nsub = n // sub
    tcs = tc // sub
    wspec = _const_spec((LRU_BLOCKS, LRU_BLOCK, LRU_BLOCK))
    vspec = _const_spec((1, c))
    pspecs = [_const_spec((LRU_CONV_W, c)), vspec, wspec, vspec, wspec, vspec, vspec]
    mspec = _const_spec((N_META, c))

    def chunk_f(j):
        return jnp.maximum(j - 1, 0)

    hf, hfm = pl.pallas_call(
        functools.partial(_lru_fwd_kernel, nc=nc),
        grid=(b, nc + 1),
        in_specs=[pl.BlockSpec((1, tc, c), lambda bi, j: (bi, chunk_f(j), 0)),
                  pl.BlockSpec((1, sub, c), lambda bi, j: (bi, jnp.maximum(chunk_f(j) * tcs - 1, 0), 0)),
                  pl.BlockSpec((1, sub, c), lambda bi, j: (bi, jnp.minimum(j * tcs, nsub - 1), 0)),
                  mspec] + pspecs,
        out_specs=[pl.BlockSpec((1, tc, c), lambda bi, j: (bi, chunk_f(j), 0)),
                   pl.BlockSpec((1, N_META, c), lambda bi, j: (bi, 0, 0))],
        out_shape=[jax.ShapeDtypeStruct((b, n, c), F32), jax.ShapeDtypeStruct((b, N_META, c), F32)],
        scratch_shapes=[pltpu.VMEM((tc, c), F32), pltpu.VMEM((tc, c), F32), pltpu.VMEM((1, c), F32)],
        compiler_params=pltpu.CompilerParams(dimension_semantics=("parallel", "arbitrary"),
                                             vmem_limit_bytes=VMEM_LIMIT_BYTES),
        name="lru_fwd",
    )(xr, xr, xr, xm, conv_w, conv_b, w_r[0], b_r[0:1], w_i[0], b_i[0:1], lam[0:1])

    def chunk_b(j):
        return jnp.maximum(nc - 1 - j, 0)

    def next_b(j):
        return jnp.where(j == nc, 0, jnp.minimum((chunk_b(j) + 1) * tcs, nsub - 1))

    out, outm = pl.pallas_call(
        functools.partial(_lru_bwd_kernel, nc=nc),
        grid=(b, nc + 1),
        in_specs=[pl.BlockSpec((1, tc, c), lambda bi, j: (bi, chunk_b(j), 0)),
                  pl.BlockSpec((1, sub, c), lambda bi, j: (bi, jnp.maximum(chunk_b(j) * tcs - 1, 0), 0)),
                  pl.BlockSpec((1, sub, c), lambda bi, j: (bi, next_b(j), 0)),
                  mspec,
                  pl.BlockSpec((1, tc, c), lambda bi, j: (bi, chunk_b(j), 0)),
                  mspec,
                  pl.BlockSpec((1, tc, c), lambda bi, j: (bi, chunk_b(j), 0)),
                  pl.BlockSpec((1, N_META, c), lambda bi, j: (bi, 0, 0))] + pspecs + [vspec],
        out_specs=[pl.BlockSpec((1, tc, c), lambda bi, j: (bi, chunk_b(j), 0)),
                   pl.BlockSpec((1, N_META, c), lambda bi, j: (bi, 0, 0))],
        out_shape=[jax.ShapeDtypeStruct((b, n, c), BF16), jax.ShapeDtypeStruct((b, N_META, c), BF16)],
        scratch_shapes=[pltpu.VMEM((tc, c), F32), pltpu.VMEM((tc, c), F32), pltpu.VMEM((1, c), F32),
                        pltpu.VMEM((tc, c), F32)],
        compiler_params=pltpu.CompilerParams(dimension_semantics=("parallel", "arbitrary"),
                                             vmem_limit_bytes=VMEM_LIMIT_BYTES),
        name="lru_bwd",
    )(xr, xr, xr, xm, gy, gym, hf, hfm, conv_w, conv_b, w_r[1], b_r[1:2], w_i[1], b_i[1:2], lam[1:2],
      out_gain)
    return out, outm


def _out_proj_kernel(attn_ref, lru_ref, res_ref, ag_ref, wa_ref, wl_ref, pg_ref, fg_ref, h1_ref, hn_ref):
    an = _rms(attn_ref[...].astype(F32), ag_ref[...]).astype(BF16)
    mixed = _dot(an, wa_ref[...]) + _dot(lru_ref[...], wl_ref[...])
    h1 = res_ref[...] + _rms(mixed, pg_ref[...])
    h1_ref[...] = h1
    hn_ref[...] = _rms(h1, fg_ref[...]).astype(BF16)


def _out_proj(attn2d, lru2d, res2d, attn_gain, w_a, w_l, post_gain, ffn_gain, tm):
    rows = attn2d.shape[0]
    row_spec = lambda w: pl.BlockSpec((tm, w), lambda i: (i, 0))
    wspec = pl.BlockSpec((ATTN_WIDTH, D_MODEL), lambda i: (0, 0), pipeline_mode=pl.Buffered(1))
    return pl.pallas_call(
        _out_proj_kernel,
        grid=(rows // tm,),
        in_specs=[row_spec(ATTN_WIDTH), row_spec(LRU_WIDTH), row_spec(D_MODEL), _const_spec((1, ATTN_WIDTH)),
                  wspec, wspec, _const_spec((1, D_MODEL)), _const_spec((1, D_MODEL))],
        out_specs=[row_spec(D_MODEL), row_spec(D_MODEL)],
        out_shape=[jax.ShapeDtypeStruct((rows, D_MODEL), F32), jax.ShapeDtypeStruct((rows, D_MODEL), BF16)],
        compiler_params=pltpu.CompilerParams(dimension_semantics=("parallel",),
                                             vmem_limit_bytes=VMEM_LIMIT_BYTES),
        name="out_proj",
    )(attn2d, lru2d, res2d, attn_gain, w_a, w_l, post_gain, ffn_gain)


def _ffn_kernel(hn_ref, prev_ref, next_ref, hm_ref, h1_ref, wg_ref, wv_ref, wd_ref, cw_ref, cb_ref, og_ref,
                out_ref, ext_ref, acc_ref, *, nt, nf):
    i = pl.program_id(1)
    f = pl.program_id(2)
    tm = hn_ref.shape[1]
    halo = SUBLANES_BF16

    @pl.when(f == 0)
    def _():
        ext_ref[0:halo, :] = jnp.where(i == 0, hm_ref[0], prev_ref[0])
        ext_ref[halo:halo + tm, :] = hn_ref[0]
        ext_ref[halo + tm:2 * halo + tm, :] = jnp.where(i == nt - 1, jnp.zeros_like(next_ref[0]), next_ref[0])
        acc_ref[...] = jnp.zeros(acc_ref.shape, F32)

    n_ext = tm + 2 * halo
    gfull = _dot(ext_ref[...], wg_ref[...])
    g = jnp.broadcast_to(cb_ref[...], (tm, gfull.shape[1]))
    for kk in range(FFN_CONV_W):
        shift = (FFN_CONV_W // 2 - kk) % n_ext
        gk = gfull if shift == 0 else pltpu.roll(gfull, shift, axis=0)
        g = g + gk[halo:halo + tm] * cw_ref[kk:kk + 1, :]
    val = _dot(ext_ref[halo:halo + tm, :], wv_ref[...])
    act = (jax.nn.silu(g) * val).astype(BF16)
    acc_ref[...] += _dot(act, wd_ref[...])

    @pl.when(f == nf - 1)
    def _():
        out_ref[0] = h1_ref[0] + _rms(acc_ref[...], og_ref[...])


def _ffn(hn, hnm, h1, w_up, w_down, conv_w, conv_b, out_gain, tm, fc):
    b, n, d = hn.shape
    nt = n // tm
    nf = FFN_DIM // fc
    halo = SUBLANES_BF16
    th = tm // halo
    nh = n // halo
    return pl.pallas_call(
        functools.partial(_ffn_kernel, nt=nt, nf=nf),
        grid=(b, nt, nf),
        in_specs=[pl.BlockSpec((1, tm, d), lambda bi, i, f: (bi, i, 0)),
                  pl.BlockSpec((1, halo, d), lambda bi, i, f: (bi, jnp.maximum(i * th - 1, 0), 0)),
                  pl.BlockSpec((1, halo, d), lambda bi, i, f: (bi, jnp.minimum((i + 1) * th, nh - 1), 0)),
                  pl.BlockSpec((1, N_META, d), lambda bi, i, f: (bi, 0, 0)),
                  pl.BlockSpec((1, tm, d), lambda bi, i, f: (bi, i, 0)),
                  pl.BlockSpec((d, fc), lambda bi, i, f: (0, f)),
                  pl.BlockSpec((d, fc), lambda bi, i, f: (0, f + nf)),
                  pl.BlockSpec((fc, d), lambda bi, i, f: (f, 0)),
                  pl.BlockSpec((FFN_CONV_W, fc), lambda bi, i, f: (0, f)),
                  pl.BlockSpec((1, fc), lambda bi, i, f: (0, f)),
                  _const_spec((1, d))],
        out_specs=pl.BlockSpec((1, tm, d), lambda bi, i, f: (bi, i, 0)),
        out_shape=jax.ShapeDtypeStruct((b, n, d), F32),
        scratch_shapes=[pltpu.VMEM((tm + 2 * halo, d), BF16), pltpu.VMEM((tm, d), F32)],
        compiler_params=pltpu.CompilerParams(
            dimension_semantics=("parallel", "parallel", "arbitrary"),
            vmem_limit_bytes=VMEM_LIMIT_BYTES),
        name="conv_ffn",
    )(hn, hn, hn, hnm, h1, w_up, w_up, w_down, conv_w, conv_b, out_gain)


def _rope_tables(n):
    rows = n // GRID_W
    t_row = jnp.repeat(jnp.arange(rows), GRID_W).astype(F32)
    t_col = jnp.tile(jnp.arange(GRID_W), rows).astype(F32)
    half = HEAD_DIM // 2
    inv = ROPE_THETA ** (-jnp.arange(0, half, 2, dtype=F32) / half)
    ang = jnp.concatenate([t_row[:, None] * inv, t_col[:, None] * inv], axis=-1)
    cos, sin = jnp.cos(ang), jnp.sin(ang)
    return jnp.concatenate([cos, cos], axis=-1), jnp.concatenate([-sin, sin], axis=-1)


_TILES = dict(tm=512, tq=512, tk=1024, tc=512, tf=512, fc=512)


def _pick(n, pref):
    t = pref
    while n % t:
        t //= 2
    return t


def _trunk(x, meta, mp, p):
    b, n, d = x.shape
    rows = b * n
    cos_t, sin_t = _rope_tables(n)
    tm = _pick(n, _TILES["tm"])
    q, k, vt, xr, gy = _in_proj(x.reshape(rows, d), p["norm_pre_mix"], p["w_in"], p["q_norm"], p["k_norm"],
                                cos_t, sin_t, tm)
    tq = _pick(n, _TILES["tq"])
    tk = _pick(n, _TILES["tk"])
    bound = mp["bound"]

    def attend(running_max):
        def run(bound, q, qm, k, vt, km, vmt):
            real = _attention(bound, q, k, vt, km, vmt, b, tq, tk, False, running_max)
            meta_rows = _attention(bound, qm, k, vt, km, vmt, b, LANES, tk, True, running_max)
            return real, meta_rows
        return run

    attn, attn_m = lax.cond(2.0 * bound[0] <= MAX_FIXED_SHIFT_RANGE, attend(False), attend(True),
                            bound, q, mp["q"], k, vt, mp["km"], mp["vmt"])
    attn_m = attn_m[:, :N_META]
    lru, lru_m = _lru(xr.reshape(b, n, LRU_WIDTH), gy.reshape(b, n, LRU_WIDTH), mp["xr"], mp["gy"],
                      p["lru_conv_w"], p["lru_conv_b"], p["lru_w_r"], p["lru_b_r"], p["lru_w_i"], p["lru_b_i"],
                      p["lru_lambda"], p["lru_out_norm"], _pick(n, _TILES["tc"]))
    op = (p["attn_out_norm"], p["w_out_a"], p["w_out_l"], p["norm_post_mix"], p["norm_pre_ffn"])
    h1, hn = _out_proj(attn.reshape(rows, ATTN_WIDTH), lru.reshape(rows, LRU_WIDTH), x.reshape(rows, d), *op, tm)
    res_m = jnp.broadcast_to(meta[None], (b, N_META, d)).reshape(b * N_META, d)
    _, hn_m = _out_proj(attn_m.reshape(b * N_META, ATTN_WIDTH), lru_m.reshape(b * N_META, LRU_WIDTH), res_m,
                        *op, N_META)
    return _ffn(hn.reshape(b, n, d), hn_m.reshape(b, N_META, d), h1.reshape(b, n, d), p["w_up"], p["w_down"],
                p["ffn_conv_w"], p["ffn_conv_b"], p["norm_post_ffn"], _pick(n, _TILES["tf"]), _TILES["fc"])


def _head_perm():
    q4 = HEAD_DIM // 4
    idx = jnp.arange(HEAD_DIM).reshape(2, 2, q4)
    return idx.transpose(1, 0, 2).reshape(HEAD_DIM)


def kernel(x_prompt, x_sample, meta_tokens, norm_pre_mix, w_in, q_norm, k_norm, lru_conv_w, lru_conv_b,
           lru_w_r, lru_b_r, lru_w_i, lru_b_i, lru_lambda, attn_out_norm, lru_out_norm, w_out,
           norm_post_mix, norm_pre_ffn, w_up, ffn_conv_w, ffn_conv_b, w_down, norm_post_ffn):
    perm = _head_perm()
    n_rot = N_Q_HEADS + N_KV_HEADS
    cols = (jnp.arange(n_rot)[:, None] * HEAD_DIM + perm[None, :]).reshape(-1)
    cols = jnp.concatenate([cols, jnp.arange(n_rot * HEAD_DIM, IN_WIDTH)])
    w_out_b = w_out[0].astype(BF16)
    p = {
        "norm_pre_mix": norm_pre_mix[0][None], "w_in": w_in[0][:, cols].astype(BF16),
        "q_norm": q_norm[0][perm][None], "k_norm": k_norm[0][perm][None],
        "lru_conv_w": lru_conv_w[0], "lru_conv_b": lru_conv_b[0][None],
        "lru_w_r": lru_w_r[0].astype(BF16), "lru_b_r": lru_b_r[0],
        "lru_w_i": lru_w_i[0].astype(BF16), "lru_b_i": lru_b_i[0], "lru_lambda": lru_lambda[0],
        "attn_out_norm": attn_out_norm[0][None], "lru_out_norm": lru_out_norm[0][None],
        "w_out_a": w_out_b[:ATTN_WIDTH], "w_out_l": w_out_b[ATTN_WIDTH:],
        "norm_post_mix": norm_post_mix[0][None], "norm_pre_ffn": norm_pre_ffn[0][None],
        "w_up": w_up[0].astype(BF16), "ffn_conv_w": ffn_conv_w[0], "ffn_conv_b": ffn_conv_b[0][None],
        "w_down": w_down[0].astype(BF16), "norm_post_ffn": norm_post_ffn[0][None],
    }
    ones = jnp.ones((LANES, HEAD_DIM), F32)
    meta_pad = jnp.pad(meta_tokens, ((0, LANES - N_META), (0, 0)))
    qm, km, vtm, xrm, gym = _in_proj(meta_pad, p["norm_pre_mix"], p["w_in"], p["q_norm"], p["k_norm"],
                                     ones, jnp.zeros_like(ones), LANES)
    bound = (HEAD_DIM * Q_PRESCALE * BF16_ROUNDING_SLACK) * jnp.max(jnp.abs(q_norm[0])) * jnp.max(jnp.abs(k_norm[0]))
    mp = {"q": qm, "km": km[:, :N_META], "vmt": vtm[:, :N_META], "xr": xrm[:N_META], "gy": gym[:N_META],
          "bound": bound.reshape(1).astype(F32)}
    return _trunk(x_prompt, meta_tokens, mp, p), _trunk(x_sample, meta_tokens, mp, p)
```

```python
import functools

import jax
import jax.numpy as jnp
from jax import lax
from jax.experimental import pallas as pl
from jax.experimental.pallas import tpu as pltpu

D_MODEL = 2048
N_META = 16
GRID_W = 64
HEAD_DIM = 128
N_Q_HEADS = 8
N_KV_HEADS = 2
Q_PER_KV = N_Q_HEADS // N_KV_HEADS
ATTN_WIDTH = N_Q_HEADS * HEAD_DIM
KV_WIDTH = N_KV_HEADS * HEAD_DIM
LRU_WIDTH = D_MODEL - ATTN_WIDTH
LRU_BLOCKS = 8
LRU_BLOCK = LRU_WIDTH // LRU_BLOCKS
LRU_CONV_W = 4
LRU_C = 8.0
LRU_SUB = 64
OUT_PROJ_PARTS = 2
IN_WIDTH = ATTN_WIDTH + 2 * KV_WIDTH + 2 * LRU_WIDTH
FFN_DIM = 5632
FFN_CONV_W = 3
ROPE_THETA = 10000.0
EPS = 1e-6
ATTN_SCALE = HEAD_DIM ** -0.5
LOG2E = 1.4426950408889634
LN2 = 0.6931471805599453
Q_PRESCALE = ATTN_SCALE * LOG2E
MAX_FIXED_SHIFT_RANGE = 120.0
BF16_ROUNDING_SLACK = 1.0 + 2.0 ** -6

LANES = 128
SUBLANES_F32 = 8
SUBLANES_BF16 = 16
VMEM_LIMIT_BYTES = 56 * 1024 * 1024

F32 = jnp.float32
BF16 = jnp.bfloat16


def _rms(x, g):
    ms = jnp.mean(x * x, axis=-1, keepdims=True)
    return x * lax.rsqrt(ms + EPS) * g


def _dot(a, b):
    return jnp.dot(a, b, preferred_element_type=F32)


def _const_spec(shape):
    zeros = (0,) * len(shape)
    return pl.BlockSpec(shape, lambda *_: zeros)


def _in_proj_kernel(x_ref, g_ref, w_ref, qg_ref, kg_ref, cos_ref, sin_ref,
                    q_ref, k_ref, vt_ref, xr_ref, gy_ref):
    xn = _rms(x_ref[...], g_ref[...]).astype(BF16)
    cos = cos_ref[...]
    sin = sin_ref[...]

    def rope_head(zh, gain):
        y = _rms(zh, gain)
        return y * cos + pltpu.roll(y, HEAD_DIM // 2, axis=1) * sin

    zq = _dot(xn, w_ref[:, 0:ATTN_WIDTH])
    for h in range(N_Q_HEADS):
        sl = slice(h * HEAD_DIM, (h + 1) * HEAD_DIM)
        q_ref[h] = (rope_head(zq[:, sl], qg_ref[...]) * Q_PRESCALE).astype(BF16)
    zkv = _dot(xn, w_ref[:, ATTN_WIDTH:ATTN_WIDTH + 2 * KV_WIDTH])
    for h in range(N_KV_HEADS):
        sl = slice(h * HEAD_DIM, (h + 1) * HEAD_DIM)
        k_ref[h] = rope_head(zkv[:, sl], kg_ref[...]).astype(BF16)
    vt_ref[...] = zkv[:, KV_WIDTH:2 * KV_WIDTH].T.astype(BF16)
    o3 = ATTN_WIDTH + 2 * KV_WIDTH
    xr_ref[...] = _dot(xn, w_ref[:, o3:o3 + LRU_WIDTH])
    gy_ref[...] = _dot(xn, w_ref[:, o3 + LRU_WIDTH:o3 + 2 * LRU_WIDTH])


def _in_proj(x2d, gain, w_in, q_gain, k_gain, cos_t, sin_t, tm):
    rows = x2d.shape[0]
    n_tab = cos_t.shape[0] // tm
    row_spec = lambda w: pl.BlockSpec((tm, w), lambda i: (i, 0))
    tab_spec = pl.BlockSpec((tm, HEAD_DIM), lambda i: (i % n_tab, 0))
    return pl.pallas_call(
        _in_proj_kernel,
        grid=(rows // tm,),
        in_specs=[row_spec(D_MODEL), _const_spec((1, D_MODEL)),
                  pl.BlockSpec((D_MODEL, IN_WIDTH), lambda i: (0, 0), pipeline_mode=pl.Buffered(1)),
                  _const_spec((1, HEAD_DIM)), _const_spec((1, HEAD_DIM)), tab_spec, tab_spec],
        out_specs=[pl.BlockSpec((N_Q_HEADS, tm, HEAD_DIM), lambda i: (0, i, 0)),
                   pl.BlockSpec((N_KV_HEADS, tm, HEAD_DIM), lambda i: (0, i, 0)),
                   pl.BlockSpec((KV_WIDTH, tm), lambda i: (0, i)),
                   row_spec(LRU_WIDTH), row_spec(LRU_WIDTH)],
        out_shape=[jax.ShapeDtypeStruct((N_Q_HEADS, rows, HEAD_DIM), BF16),
                   jax.ShapeDtypeStruct((N_KV_HEADS, rows, HEAD_DIM), BF16),
                   jax.ShapeDtypeStruct((KV_WIDTH, rows), BF16),
                   jax.ShapeDtypeStruct((rows, LRU_WIDTH), F32),
                   jax.ShapeDtypeStruct((rows, LRU_WIDTH), F32)],
        compiler_params=pltpu.CompilerParams(dimension_semantics=("parallel",),
                                             vmem_limit_bytes=VMEM_LIMIT_BYTES),
        name="in_proj",
    )(x2d, gain, w_in, q_gain, k_gain, cos_t, sin_t)


def _attn_kernel(bound_ref, q_ref, k_ref, vt_ref, km_ref, vmt_ref, o_ref, l_ref, acc_ref, *m_scratch,
                 nkv, tk, running_max):
    tq = q_ref.shape[1]
    width = Q_PER_KV * tq
    qs = q_ref[...].reshape(width, HEAD_DIM)
    nt = (((1,), (1,)), ((), ()))

    def block(kb, vtb, first):
        st = lax.dot_general(kb, qs, nt, preferred_element_type=F32)
        if running_max:
            m_ref, = m_scratch
            m_new = jnp.max(st, axis=0, keepdims=True)
            if not first:
                m_prev = m_ref[...]
                m_new = jnp.maximum(m_prev, m_new)
                alpha = jnp.exp2(m_prev - m_new)
            m_ref[...] = m_new
            p = jnp.exp2(st - m_new)
        else:
            p = jnp.exp2(st - bound_ref[0])
        psum = jnp.sum(p.reshape(p.shape[0] // SUBLANES_F32, SUBLANES_F32, width), axis=0)
        pv = _dot(vtb, p.astype(BF16))
        if first:
            l_ref[...] = psum
            acc_ref[...] = pv
        elif running_max:
            l_ref[...] = alpha * l_ref[...] + psum
            acc_ref[...] = alpha * acc_ref[...] + pv
        else:
            l_ref[...] += psum
            acc_ref[...] += pv

    block(km_ref[0], vmt_ref[...], True)

    def body(j, carry):
        start = pl.multiple_of(j * tk, tk)
        block(k_ref[0, pl.ds(start, tk), :], vt_ref[:, pl.ds(start, tk)], False)
        return carry

    lax.fori_loop(0, nkv, body, 0)

    o = acc_ref[...] * (1.0 / jnp.sum(l_ref[...], axis=0, keepdims=True))
    for g in range(Q_PER_KV):
        o_ref[0, :, g * HEAD_DIM:(g + 1) * HEAD_DIM] = o[:, g * tq:(g + 1) * tq].T.astype(BF16)


def _attention(bound, q, k, vt, km, vmt, batch, tq, tk, shared_q, running_max):
    n = k.shape[1] // batch
    nq = q.shape[1] if shared_q else q.shape[1] // batch
    nkv = n // tk
    nqt = nq // tq
    width = Q_PER_KV * tq
    if shared_q:
        q_map = lambda bi, h, i: (h, i, 0)
    else:
        q_map = lambda bi, h, i: (h, bi * nqt + i, 0)
    scratch = [pltpu.VMEM((SUBLANES_F32, width), F32), pltpu.VMEM((HEAD_DIM, width), F32)]
    if running_max:
        scratch.append(pltpu.VMEM((1, width), F32))
    return pl.pallas_call(
        functools.partial(_attn_kernel, nkv=nkv, tk=tk, running_max=running_max),
        grid=(batch, N_KV_HEADS, nqt),
        in_specs=[pl.BlockSpec(memory_space=pltpu.SMEM),
                  pl.BlockSpec((Q_PER_KV, tq, HEAD_DIM), q_map),
                  pl.BlockSpec((1, n, HEAD_DIM), lambda bi, h, i: (h, bi, 0)),
                  pl.BlockSpec((HEAD_DIM, n), lambda bi, h, i: (h, bi)),
                  pl.BlockSpec((1, N_META, HEAD_DIM), lambda bi, h, i: (h, 0, 0)),
                  pl.BlockSpec((HEAD_DIM, N_META), lambda bi, h, i: (h, 0))],
        out_specs=pl.BlockSpec((1, tq, Q_PER_KV * HEAD_DIM), lambda bi, h, i: (bi, i, h)),
        out_shape=jax.ShapeDtypeStruct((batch, nq, ATTN_WIDTH), BF16),
        scratch_shapes=scratch,
        compiler_params=pltpu.CompilerParams(
            dimension_semantics=("parallel", "parallel", "arbitrary"),
            vmem_limit_bytes=VMEM_LIMIT_BYTES),
        name="attention",
    )(bound, q, k, vt, km, vmt)


def _softplus(x):
    return jnp.maximum(x, 0.0) + jnp.log1p(jnp.exp(-jnp.abs(x)))


def _lru_prep(ext, rows, cw_ref, cb_ref, wr_ref, br_ref, wi_ref, bi_ref, lam_ref, a_ref, u_ref):
    n_ext = rows + 2 * SUBLANES_F32
    xc = jnp.broadcast_to(cb_ref[...], (rows, LRU_WIDTH))
    for kk in range(LRU_CONV_W):
        shift = (LRU_CONV_W // 2 - kk) % n_ext
        ek = ext if shift == 0 else pltpu.roll(ext, shift, axis=0)
        xc = xc + ek[SUBLANES_F32:SUBLANES_F32 + rows] * cw_ref[kk:kk + 1, :]
    xcb = xc.astype(BF16)
    r_parts, i_parts = [], []
    for blk in range(LRU_BLOCKS):
        xb = xcb[:, blk * LRU_BLOCK:(blk + 1) * LRU_BLOCK]
        r_parts.append(_dot(xb, wr_ref[blk]))
        i_parts.append(_dot(xb, wi_ref[blk]))
    tr = jnp.tanh(jnp.concatenate(r_parts, axis=-1) + br_ref[...])
    ti = jnp.tanh(jnp.concatenate(i_parts, axis=-1) + bi_ref[...])
    c1 = (-0.5 * LRU_C * LOG2E) * _softplus(-lam_ref[...])
    log2_a = c1 * tr + c1
    a = jnp.exp2(log2_a)
    a_ref[0:rows, :] = a
    gate2 = jnp.tanh((-LN2) * log2_a) * (1.0 + a * a)
    gate = jnp.where(gate2 > 0.0, gate2 * lax.rsqrt(gate2), 0.0)
    half_xc = 0.5 * xc
    u_ref[0:rows, :] = gate * (half_xc * ti + half_xc)


def _lru_scan(rows, a_ref, u_ref, h, out_ref, base, reverse):
    for r in (range(rows - 1, -1, -1) if reverse else range(rows)):
        h = a_ref[r:r + 1, :] * h + u_ref[r:r + 1, :]
        out_ref[pl.ds(base + r, 1), :] = h
    return h


def _lru_fill_ext(ext_ref, prev, main, nxt):
    rows = main.shape[0]
    ext_ref[0:SUBLANES_F32, :] = prev
    ext_ref[SUBLANES_F32:SUBLANES_F32 + rows, :] = main
    ext_ref[SUBLANES_F32 + rows:2 * SUBLANES_F32 + rows, :] = nxt


def _lru_fwd_kernel(xr_ref, prev_ref, next_ref, xm_ref, cw_ref, cb_ref, wr_ref, br_ref, wi_ref, bi_ref,
                    lam_ref, hf_ref, hfm_ref, a0_ref, u0_ref, a1_ref, u1_ref, h_ref, ext_ref, *, nc):
    j = pl.program_id(1)
    tc = xr_ref.shape[1]
    wts = (cw_ref, cb_ref, wr_ref, br_ref, wi_ref, bi_ref, lam_ref)
    nsub = tc // LRU_SUB
    zeros8 = jnp.zeros((SUBLANES_F32, LRU_WIDTH), F32)

    @pl.when(j == 0)
    def _():
        ext = jnp.concatenate([zeros8, xm_ref[...], next_ref[0]], axis=0)
        _lru_prep(ext, N_META, *wts, a0_ref, u0_ref)
        h_ref[...] = _lru_scan(N_META, a0_ref, u0_ref, jnp.zeros(h_ref.shape, F32), hfm_ref.at[0], 0, False)

    @pl.when(j > 0)
    def _():
        _lru_fill_ext(ext_ref, jnp.where(j == 1, xm_ref[N_META - SUBLANES_F32:N_META, :], prev_ref[0]),
                      xr_ref[0], jnp.where(j == nc, zeros8, next_ref[0]))

        def prep(s, a_ref, u_ref):
            start = pl.multiple_of(s * LRU_SUB, LRU_SUB)
            _lru_prep(ext_ref[pl.ds(start, LRU_SUB + 2 * SUBLANES_F32), :], LRU_SUB, *wts, a_ref, u_ref)

        prep(0, a0_ref, u0_ref)

        def pair(i, h):
            s = 2 * i
            prep(s + 1, a1_ref, u1_ref)
            h = _lru_scan(LRU_SUB, a0_ref, u0_ref, h, hf_ref.at[0], pl.multiple_of(s * LRU_SUB, LRU_SUB), False)
            prep(jnp.minimum(s + 2, nsub - 1), a0_ref, u0_ref)
            return _lru_scan(LRU_SUB, a1_ref, u1_ref, h, hf_ref.at[0],
                             pl.multiple_of((s + 1) * LRU_SUB, LRU_SUB), False)

        h_ref[...] = lax.fori_loop(0, nsub // 2, pair, h_ref[...])


def _lru_bwd_kernel(xr_ref, prev_ref, next_ref, xm_ref, gy_ref, gym_ref, hf_ref, hfm_ref,
                    cw_ref, cb_ref, wr_ref, br_ref, wi_ref, bi_ref, lam_ref, og_ref,
                    out_ref, outm_ref, a0_ref, u0_ref, a1_ref, u1_ref, h_ref, ext_ref, hb0_ref, hb1_ref, *, nc):
    j = pl.program_id(1)
    tc = xr_ref.shape[1]
    wts = (cw_ref, cb_ref, wr_ref, br_ref, wi_ref, bi_ref, lam_ref)
    nsub = tc // LRU_SUB
    zeros8 = jnp.zeros((SUBLANES_F32, LRU_WIDTH), F32)

    def finish(hf, hb, gy):
        return _rms((hf + hb) * jax.nn.gelu(gy), og_ref[...]).astype(BF16)

    @pl.when(j == 0)
    def _():
        h_ref[...] = jnp.zeros(h_ref.shape, F32)

    @pl.when(j < nc)
    def _():
        _lru_fill_ext(ext_ref, jnp.where(j == nc - 1, xm_ref[N_META - SUBLANES_F32:N_META, :], prev_ref[0]),
                      xr_ref[0], jnp.where(j == 0, zeros8, next_ref[0]))

        def prep(s, a_ref, u_ref):
            start = pl.multiple_of(s * LRU_SUB, LRU_SUB)
            _lru_prep(ext_ref[pl.ds(start, LRU_SUB + 2 * SUBLANES_F32), :], LRU_SUB, *wts, a_ref, u_ref)

        def piece(s, a_ref, u_ref, hb_ref, h):
            h = _lru_scan(LRU_SUB, a_ref, u_ref, h, hb_ref, 0, True)
            rows = pl.ds(pl.multiple_of(s * LRU_SUB, LRU_SUB), LRU_SUB)
            out_ref[0, rows, :] = finish(hf_ref[0, rows, :], hb_ref[...], gy_ref[0, rows, :])
            return h

        prep(nsub - 1, a0_ref, u0_ref)

        def pair(i, h):
            s = nsub - 1 - 2 * i
            prep(s - 1, a1_ref, u1_ref)
            h = piece(s, a0_ref, u0_ref, hb0_ref, h)
            prep(jnp.maximum(s - 2, 0), a0_ref, u0_ref)
            return piece(s - 1, a1_ref, u1_ref, hb1_ref, h)

        h_ref[...] = lax.fori_loop(0, nsub // 2, pair, h_ref[...])

    @pl.when(j == nc)
    def _():
        ext = jnp.concatenate([zeros8, xm_ref[...], next_ref[0]], axis=0)
        _lru_prep(ext, N_META, *wts, a0_ref, u0_ref)
        h_ref[...] = _lru_scan(N_META, a0_ref, u0_ref, h_ref[...], hb0_ref, 0, True)
        outm_ref[0] = finish(hfm_ref[0], hb0_ref[0:N_META, :], gym_ref[...])


def _lru(xr, gy, xm, gym, conv_w, conv_b, w_r, b_r, w_i, b_i, lam, out_gain, tc):
    b, n, c = xr.shape
    nc = n // tc
    sub = SUBLANES_F32
    nsub = n // sub
    tcs = tc // sub
    wspec = _const_spec((LRU_BLOCKS, LRU_BLOCK, LRU_BLOCK))
    vspec = _const_spec((1, c))
    pspecs = [_const_spec((LRU_CONV_W, c)), vspec, wspec, vspec, wspec, vspec, vspec]
    mspec = _const_spec((N_META, c))
    slots = [pltpu.VMEM((LRU_SUB, c), F32)] * 4
    assert tc % (2 * LRU_SUB) == 0 and LRU_SUB >= N_META

    def chunk_f(j):
        return jnp.maximum(j - 1, 0)

    hf, hfm = pl.pallas_call(
        functools.partial(_lru_fwd_kernel, nc=nc),
        grid=(b, nc + 1),
        in_specs=[pl.BlockSpec((1, tc, c), lambda bi, j: (bi, chunk_f(j), 0)),
                  pl.BlockSpec((1, sub, c), lambda bi, j: (bi, jnp.maximum(chunk_f(j) * tcs - 1, 0), 0)),
                  pl.BlockSpec((1, sub, c), lambda bi, j: (bi, jnp.minimum(j * tcs, nsub - 1), 0)),
                  mspec] + pspecs,
        out_specs=[pl.BlockSpec((1, tc, c), lambda bi, j: (bi, chunk_f(j), 0)),
                   pl.BlockSpec((1, N_META, c), lambda bi, j: (bi, 0, 0))],
        out_shape=[jax.ShapeDtypeStruct((b, n, c), F32), jax.ShapeDtypeStruct((b, N_META, c), F32)],
        scratch_shapes=slots + [pltpu.VMEM((1, c), F32), pltpu.VMEM((tc + 2 * sub, c), F32)],
        compiler_params=pltpu.CompilerParams(dimension_semantics=("parallel", "arbitrary"),
                                             vmem_limit_bytes=VMEM_LIMIT_BYTES),
        name="lru_fwd",
    )(xr, xr, xr, xm, conv_w, conv_b, w_r[0], b_r[0:1], w_i[0], b_i[0:1], lam[0:1])

    def chunk_b(j):
        return jnp.maximum(nc - 1 - j, 0)

    def next_b(j):
        return jnp.where(j == nc, 0, jnp.minimum((chunk_b(j) + 1) * tcs, nsub - 1))

    out, outm = pl.pallas_call(
        functools.partial(_lru_bwd_kernel, nc=nc),
        grid=(b, nc + 1),
        in_specs=[pl.BlockSpec((1, tc, c), lambda bi, j: (bi, chunk_b(j), 0)),
                  pl.BlockSpec((1, sub, c), lambda bi, j: (bi, jnp.maximum(chunk_b(j) * tcs - 1, 0), 0)),
                  pl.BlockSpec((1, sub, c), lambda bi, j: (bi, next_b(j), 0)),
                  mspec,
                  pl.BlockSpec((1, tc, c), lambda bi, j: (bi, chunk_b(j), 0)),
                  mspec,
                  pl.BlockSpec((1, tc, c), lambda bi, j: (bi, chunk_b(j), 0)),
                  pl.BlockSpec((1, N_META, c), lambda bi, j: (bi, 0, 0))] + pspecs + [vspec],
        out_specs=[pl.BlockSpec((1, tc, c), lambda bi, j: (bi, chunk_b(j), 0)),
                   pl.BlockSpec((1, N_META, c), lambda bi, j: (bi, 0, 0))],
        out_shape=[jax.ShapeDtypeStruct((b, n, c), BF16), jax.ShapeDtypeStruct((b, N_META, c), BF16)],
        scratch_shapes=slots + [pltpu.VMEM((1, c), F32), pltpu.VMEM((tc + 2 * sub, c), F32),
                                pltpu.VMEM((LRU_SUB, c), F32), pltpu.VMEM((LRU_SUB, c), F32)],
        compiler_params=pltpu.CompilerParams(dimension_semantics=("parallel", "arbitrary"),
                                             vmem_limit_bytes=VMEM_LIMIT_BYTES),
        name="lru_bwd",
    )(xr, xr, xr, xm, gy, gym, hf, hfm, conv_w, conv_b, w_r[1], b_r[1:2], w_i[1], b_i[1:2], lam[1:2],
      out_gain)
    return out, outm


def _out_proj_kernel(attn_ref, lru_ref, res_ref, ag_ref, wa_ref, wl_ref, pg_ref, fg_ref, h1_ref, hn_ref):
    tm = attn_ref.shape[0]
    parts = OUT_PROJ_PARTS if tm % (OUT_PROJ_PARTS * LANES) == 0 else 1
    part = tm // parts
    for s in range(parts):
        rows = slice(s * part, (s + 1) * part)
        an = _rms(attn_ref[rows, :].astype(F32), ag_ref[...]).astype(BF16)
        mixed = _dot(an, wa_ref[...]) + _dot(lru_ref[rows, :], wl_ref[...])
        h1 = res_ref[rows, :] + _rms(mixed, pg_ref[...])
        h1_ref[rows, :] = h1
        hn_ref[rows, :] = _rms(h1, fg_ref[...]).astype(BF16)


def _out_proj(attn2d, lru2d, res2d, attn_gain, w_a, w_l, post_gain, ffn_gain, tm):
    rows = attn2d.shape[0]
    row_spec = lambda w: pl.BlockSpec((tm, w), lambda i: (i, 0))
    wspec = pl.BlockSpec((ATTN_WIDTH, D_MODEL), lambda i: (0, 0), pipeline_mode=pl.Buffered(1))
    return pl.pallas_call(
        _out_proj_kernel,
        grid=(rows // tm,),
        in_specs=[row_spec(ATTN_WIDTH), row_spec(LRU_WIDTH), row_spec(D_MODEL), _const_spec((1, ATTN_WIDTH)),
                  wspec, wspec, _const_spec((1, D_MODEL)), _const_spec((1, D_MODEL))],
        out_specs=[row_spec(D_MODEL), row_spec(D_MODEL)],
        out_shape=[jax.ShapeDtypeStruct((rows, D_MODEL), F32), jax.ShapeDtypeStruct((rows, D_MODEL), BF16)],
        compiler_params=pltpu.CompilerParams(dimension_semantics=("parallel",),
                                             vmem_limit_bytes=VMEM_LIMIT_BYTES),
        name="out_proj",
    )(attn2d, lru2d, res2d, attn_gain, w_a, w_l, post_gain, ffn_gain)


def _ffn_kernel(hn_ref, prev_ref, next_ref, hm_ref, h1_ref, wg_ref, wv_ref, wd_ref, cw_ref, cb_ref, og_ref,
                out_ref, ext_ref, acc_ref, *, nt, nf):
    i = pl.program_id(1)
    f = pl.program_id(2)
    tm = hn_ref.shape[1]
    halo = SUBLANES_BF16

    @pl.when(f == 0)
    def _():
        ext_ref[0:halo, :] = jnp.where(i == 0, hm_ref[0], prev_ref[0])
        ext_ref[halo:halo + tm, :] = hn_ref[0]
        ext_ref[halo + tm:2 * halo + tm, :] = jnp.where(i == nt - 1, jnp.zeros_like(next_ref[0]), next_ref[0])
        acc_ref[...] = jnp.zeros(acc_ref.shape, F32)

    n_ext = tm + 2 * halo
    gfull = _dot(ext_ref[...], wg_ref[...])
    g = jnp.broadcast_to(cb_ref[...], (tm, gfull.shape[1]))
    for kk in range(FFN_CONV_W):
        shift = (FFN_CONV_W // 2 - kk) % n_ext
        gk = gfull if shift == 0 else pltpu.roll(gfull, shift, axis=0)
        g = g + gk[halo:halo + tm] * cw_ref[kk:kk + 1, :]
    val = _dot(ext_ref[halo:halo + tm, :], wv_ref[...])
    act = (jax.nn.silu(g) * val).astype(BF16)
    acc_ref[...] += _dot(act, wd_ref[...])

    @pl.when(f == nf - 1)
    def _():
        out_ref[0] = h1_ref[0] + _rms(acc_ref[...], og_ref[...])


def _ffn(hn, hnm, h1, w_up, w_down, conv_w, conv_b, out_gain, tm, fc):
    b, n, d = hn.shape
    nt = n // tm
    nf = FFN_DIM // fc
    halo = SUBLANES_BF16
    th = tm // halo
    nh = n // halo
    return pl.pallas_call(
        functools.partial(_ffn_kernel, nt=nt, nf=nf),
        grid=(b, nt, nf),
        in_specs=[pl.BlockSpec((1, tm, d), lambda bi, i, f: (bi, i, 0)),
                  pl.BlockSpec((1, halo, d), lambda bi, i, f: (bi, jnp.maximum(i * th - 1, 0), 0)),
                  pl.BlockSpec((1, halo, d), lambda bi, i, f: (bi, jnp.minimum((i + 1) * th, nh - 1), 0)),
                  pl.BlockSpec((1, N_META, d), lambda bi, i, f: (bi, 0, 0)),
                  pl.BlockSpec((1, tm, d), lambda bi, i, f: (bi, i, 0)),
                  pl.BlockSpec((d, fc), lambda bi, i, f: (0, f)),
                  pl.BlockSpec((d, fc), lambda bi, i, f: (0, f + nf)),
                  pl.BlockSpec((fc, d), lambda bi, i, f: (f, 0)),
                  pl.BlockSpec((FFN_CONV_W, fc), lambda bi, i, f: (0, f)),
                  pl.BlockSpec((1, fc), lambda bi, i, f: (0, f)),
                  _const_spec((1, d))],
        out_specs=pl.BlockSpec((1, tm, d), lambda bi, i, f: (bi, i, 0)),
        out_shape=jax.ShapeDtypeStruct((b, n, d), F32),
        scratch_shapes=[pltpu.VMEM((tm + 2 * halo, d), BF16), pltpu.VMEM((tm, d), F32)],
        compiler_params=pltpu.CompilerParams(
            dimension_semantics=("parallel", "parallel", "arbitrary"),
            vmem_limit_bytes=VMEM_LIMIT_BYTES),
        name="conv_ffn",
    )(hn, hn, hn, hnm, h1, w_up, w_up, w_down, conv_w, conv_b, out_gain)


def _rope_tables(n):
    rows = n // GRID_W
    t_row = jnp.repeat(jnp.arange(rows), GRID_W).astype(F32)
    t_col = jnp.tile(jnp.arange(GRID_W), rows).astype(F32)
    half = HEAD_DIM // 2
    inv = ROPE_THETA ** (-jnp.arange(0, half, 2, dtype=F32) / half)
    ang = jnp.concatenate([t_row[:, None] * inv, t_col[:, None] * inv], axis=-1)
    cos, sin = jnp.cos(ang), jnp.sin(ang)
    return jnp.concatenate([cos, cos], axis=-1), jnp.concatenate([-sin, sin], axis=-1)


_TILES = dict(tm=512, tq=512, tk=1024, tc=512, tf=512, fc=512)


def _pick(n, pref):
    t = pref
    while n % t:
        t //= 2
    return t


def _trunk(x, meta, mp, p):
    b, n, d = x.shape
    rows = b * n
    cos_t, sin_t = _rope_tables(n)
    tm = _pick(n, _TILES["tm"])
    q, k, vt, xr, gy = _in_proj(x.reshape(rows, d), p["norm_pre_mix"], p["w_in"], p["q_norm"], p["k_norm"],
                                cos_t, sin_t, tm)
    tq = _pick(n, _TILES["tq"])
    tk = _pick(n, _TILES["tk"])
    bound = mp["bound"]

    def attend(running_max):
        def run(bound, q, qm, k, vt, km, vmt):
            real = _attention(bound, q, k, vt, km, vmt, b, tq, tk, False, running_max)
            meta_rows = _attention(bound, qm, k, vt, km, vmt, b, LANES, tk, True, running_max)
            return real, meta_rows
        return run

    attn, attn_m = lax.cond(2.0 * bound[0] <= MAX_FIXED_SHIFT_RANGE, attend(False), attend(True),
                            bound, q, mp["q"], k, vt, mp["km"], mp["vmt"])
    attn_m = attn_m[:, :N_META]
    lru, lru_m = _lru(xr.reshape(b, n, LRU_WIDTH), gy.reshape(b, n, LRU_WIDTH), mp["xr"], mp["gy"],
                      p["lru_conv_w"], p["lru_conv_b"], p["lru_w_r"], p["lru_b_r"], p["lru_w_i"], p["lru_b_i"],
                      p["lru_lambda"], p["lru_out_norm"], _pick(n, _TILES["tc"]))
    op = (p["attn_out_norm"], p["w_out_a"], p["w_out_l"], p["norm_post_mix"], p["norm_pre_ffn"])
    h1, hn = _out_proj(attn.reshape(rows, ATTN_WIDTH), lru.reshape(rows, LRU_WIDTH), x.reshape(rows, d), *op, tm)
    res_m = jnp.broadcast_to(meta[None], (b, N_META, d)).reshape(b * N_META, d)
    _, hn_m = _out_proj(attn_m.reshape(b * N_META, ATTN_WIDTH), lru_m.reshape(b * N_META, LRU_WIDTH), res_m,
                        *op, N_META)
    return _ffn(hn.reshape(b, n, d), hn_m.reshape(b, N_META, d), h1.reshape(b, n, d), p["w_up"], p["w_down"],
                p["ffn_conv_w"], p["ffn_conv_b"], p["norm_post_ffn"], _pick(n, _TILES["tf"]), _TILES["fc"])


def _head_perm():
    q4 = HEAD_DIM // 4
    idx = jnp.arange(HEAD_DIM).reshape(2, 2, q4)
    return idx.transpose(1, 0, 2).reshape(HEAD_DIM)


def kernel(x_prompt, x_sample, meta_tokens, norm_pre_mix, w_in, q_norm, k_norm, lru_conv_w, lru_conv_b,
           lru_w_r, lru_b_r, lru_w_i, lru_b_i, lru_lambda, attn_out_norm, lru_out_norm, w_out,
           norm_post_mix, norm_pre_ffn, w_up, ffn_conv_w, ffn_conv_b, w_down, norm_post_ffn):
    perm = _head_perm()
    n_rot = N_Q_HEADS + N_KV_HEADS
    cols = (jnp.arange(n_rot)[:, None] * HEAD_DIM + perm[None, :]).reshape(-1)
    cols = jnp.concatenate([cols, jnp.arange(n_rot * HEAD_DIM, IN_WIDTH)])
    w_out_b = w_out[0].astype(BF16)
    p = {
        "norm_pre_mix": norm_pre_mix[0][None], "w_in": w_in[0][:, cols].astype(BF16),
        "q_norm": q_norm[0][perm][None], "k_norm": k_norm[0][perm][None],
        "lru_conv_w": lru_conv_w[0], "lru_conv_b": lru_conv_b[0][None],
        "lru_w_r": (0.5 * lru_w_r[0]).astype(BF16), "lru_b_r": 0.5 * lru_b_r[0],
        "lru_w_i": (0.5 * lru_w_i[0]).astype(BF16), "lru_b_i": 0.5 * lru_b_i[0], "lru_lambda": lru_lambda[0],
        "attn_out_norm": attn_out_norm[0][None], "lru_out_norm": lru_out_norm[0][None],
        "w_out_a": w_out_b[:ATTN_WIDTH], "w_out_l": w_out_b[ATTN_WIDTH:],
        "norm_post_mix": norm_post_mix[0][None], "norm_pre_ffn": norm_pre_ffn[0][None],
        "w_up": w_up[0].astype(BF16), "ffn_conv_w": ffn_conv_w[0], "ffn_conv_b": ffn_conv_b[0][None],
        "w_down": w_down[0].astype(BF16), "norm_post_ffn": norm_post_ffn[0][None],
    }
    ones = jnp.ones((LANES, HEAD_DIM), F32)
    meta_pad = jnp.pad(meta_tokens, ((0, LANES - N_META), (0, 0)))
    qm, km, vtm, xrm, gym = _in_proj(meta_pad, p["norm_pre_mix"], p["w_in"], p["q_norm"], p["k_norm"],
                                     ones, jnp.zeros_like(ones), LANES)
    bound = (HEAD_DIM * Q_PRESCALE * BF16_ROUNDING_SLACK) * jnp.max(jnp.abs(q_norm[0])) * jnp.max(jnp.abs(k_norm[0]))
    mp = {"q": qm, "km": km[:, :N_META], "vmt": vtm[:, :N_META], "xr": xrm[:N_META], "gy": gym[:N_META],
          "bound": bound.reshape(1).astype(F32)}
    return _trunk(x_prompt, meta_tokens, mp, p), _trunk(x_sample, meta_tokens, mp, p)
```

```python
import functools

import jax
import jax.numpy as jnp
from jax import lax
from jax.experimental import pallas as pl
from jax.experimental.pallas import tpu as pltpu

D_MODEL = 2048
N_META = 16
GRID_W = 64
HEAD_DIM = 128
N_Q_HEADS = 8
N_KV_HEADS = 2
Q_PER_KV = N_Q_HEADS // N_KV_HEADS
ATTN_WIDTH = N_Q_HEADS * HEAD_DIM
KV_WIDTH = N_KV_HEADS * HEAD_DIM
LRU_WIDTH = D_MODEL - ATTN_WIDTH
LRU_BLOCKS = 8
LRU_BLOCK = LRU_WIDTH // LRU_BLOCKS
LRU_CONV_W = 4
LRU_C = 8.0
LRU_SUB = 64
OUT_PROJ_PARTS = 2
IN_WIDTH = ATTN_WIDTH + 2 * KV_WIDTH + 2 * LRU_WIDTH
FFN_DIM = 5632
FFN_CONV_W = 3
ROPE_THETA = 10000.0
EPS = 1e-6
ATTN_SCALE = HEAD_DIM ** -0.5
LOG2E = 1.4426950408889634
LN2 = 0.6931471805599453
Q_PRESCALE = ATTN_SCALE * LOG2E
MAX_FIXED_SHIFT_RANGE = 120.0
BF16_ROUNDING_SLACK = 1.0 + 2.0 ** -6

LANES = 128
SUBLANES_F32 = 8
SUBLANES_BF16 = 16
VMEM_LIMIT_BYTES = 56 * 1024 * 1024

F32 = jnp.float32
BF16 = jnp.bfloat16


def _rms(x, g):
    ms = jnp.mean(x * x, axis=-1, keepdims=True)
    return x * lax.rsqrt(ms + EPS) * g


def _dot(a, b):
    return jnp.dot(a, b, preferred_element_type=F32)


def _const_spec(shape):
    zeros = (0,) * len(shape)
    return pl.BlockSpec(shape, lambda *_: zeros)


def _in_proj_kernel(x_ref, xprev_ref, xnext_ref, xhead_ref, g_ref, w_ref, qg_ref, kg_ref, cos_ref, sin_ref,
                    cw_ref, cb_ref, q_ref, k_ref, vt_ref, xc_ref, gy_ref, xch_ref, *, nt):
    tm = x_ref.shape[0]
    halo = SUBLANES_BF16
    pos = pl.program_id(0) % nt
    xprev = jnp.where(pos == 0, xhead_ref[...], xprev_ref[...])
    xnext = jnp.where(pos == nt - 1, jnp.zeros_like(xnext_ref[...]), xnext_ref[...])
    xn = _rms(x_ref[...], g_ref[...]).astype(BF16)
    xn_ext = jnp.concatenate([_rms(xprev, g_ref[...]).astype(BF16), xn, _rms(xnext, g_ref[...]).astype(BF16)], axis=0)
    cos = cos_ref[...]
    sin = sin_ref[...]

    def rope_head(zh, gain):
        y = _rms(zh, gain)
        return y * cos + pltpu.roll(y, HEAD_DIM // 2, axis=1) * sin

    o3 = ATTN_WIDTH + 2 * KV_WIDTH
    zq = _dot(xn, w_ref[:, 0:ATTN_WIDTH])
    for h in range(N_Q_HEADS):
        sl = slice(h * HEAD_DIM, (h + 1) * HEAD_DIM)
        q_ref[h] = (rope_head(zq[:, sl], qg_ref[...]) * Q_PRESCALE).astype(BF16)
    xr_ext = _dot(xn_ext, w_ref[:, o3:o3 + LRU_WIDTH])
    n_ext = tm + 2 * halo
    xc_ext = jnp.broadcast_to(cb_ref[...], (n_ext, LRU_WIDTH))
    for kk in range(LRU_CONV_W):
        shift = (LRU_CONV_W // 2 - kk) % n_ext
        xk = xr_ext if shift == 0 else pltpu.roll(xr_ext, shift, axis=0)
        xc_ext = xc_ext + xk * cw_ref[kk:kk + 1, :]
    xc_ref[...] = xc_ext[halo:halo + tm]
    xch_ref[...] = xc_ext[0:halo]
    zkv = _dot(xn, w_ref[:, ATTN_WIDTH:ATTN_WIDTH + 2 * KV_WIDTH])
    for h in range(N_KV_HEADS):
        sl = slice(h * HEAD_DIM, (h + 1) * HEAD_DIM)
        k_ref[h] = rope_head(zkv[:, sl], kg_ref[...]).astype(BF16)
    vt_ref[...] = zkv[:, KV_WIDTH:2 * KV_WIDTH].T.astype(BF16)
    gy_ref[...] = _dot(xn, w_ref[:, o3 + LRU_WIDTH:o3 + 2 * LRU_WIDTH])


def _in_proj(x2d, xhead, nt, gain, w_in, q_gain, k_gain, cos_t, sin_t, conv_w, conv_b, tm):
    rows = x2d.shape[0]
    n_tab = cos_t.shape[0] // tm
    halo = SUBLANES_BF16
    th = tm // halo
    nh = rows // halo
    row_spec = lambda w: pl.BlockSpec((tm, w), lambda i: (i, 0))
    tab_spec = pl.BlockSpec((tm, HEAD_DIM), lambda i: (i % n_tab, 0))
    return pl.pallas_call(
        functools.partial(_in_proj_kernel, nt=nt),
        grid=(rows // tm,),
        in_specs=[row_spec(D_MODEL),
                  pl.BlockSpec((halo, D_MODEL), lambda i: (jnp.maximum(i * th - 1, 0), 0)),
                  pl.BlockSpec((halo, D_MODEL), lambda i: (jnp.minimum((i + 1) * th, nh - 1), 0)),
                  _const_spec((halo, D_MODEL)), _const_spec((1, D_MODEL)),
                  pl.BlockSpec((D_MODEL, IN_WIDTH), lambda i: (0, 0), pipeline_mode=pl.Buffered(1)),
                  _const_spec((1, HEAD_DIM)), _const_spec((1, HEAD_DIM)), tab_spec, tab_spec,
                  _const_spec((LRU_CONV_W, LRU_WIDTH)), _const_spec((1, LRU_WIDTH))],
        out_specs=[pl.BlockSpec((N_Q_HEADS, tm, HEAD_DIM), lambda i: (0, i, 0)),
                   pl.BlockSpec((N_KV_HEADS, tm, HEAD_DIM), lambda i: (0, i, 0)),
                   pl.BlockSpec((KV_WIDTH, tm), lambda i: (0, i)),
                   row_spec(LRU_WIDTH), row_spec(LRU_WIDTH),
                   pl.BlockSpec((halo, LRU_WIDTH), lambda i: (i, 0))],
        out_shape=[jax.ShapeDtypeStruct((N_Q_HEADS, rows, HEAD_DIM), BF16),
                   jax.ShapeDtypeStruct((N_KV_HEADS, rows, HEAD_DIM), BF16),
                   jax.ShapeDtypeStruct((KV_WIDTH, rows), BF16),
                   jax.ShapeDtypeStruct((rows, LRU_WIDTH), F32),
                   jax.ShapeDtypeStruct((rows, LRU_WIDTH), F32),
                   jax.ShapeDtypeStruct((rows // tm * halo, LRU_WIDTH), F32)],
        compiler_params=pltpu.CompilerParams(dimension_semantics=("parallel",),
                                             vmem_limit_bytes=VMEM_LIMIT_BYTES),
        name="in_proj",
    )(x2d, x2d, x2d, xhead, gain, w_in, q_gain, k_gain, cos_t, sin_t, conv_w, conv_b)


def _attn_kernel(bound_ref, q_ref, k_ref, vt_ref, km_ref, vmt_ref, o_ref, l_ref, acc_ref, *m_scratch,
                 nkv, tk, running_max):
    tq = q_ref.shape[1]
    width = Q_PER_KV * tq
    qs = q_ref[...].reshape(width, HEAD_DIM)
    nt = (((1,), (1,)), ((), ()))

    def block(kb, vtb, first):
        st = lax.dot_general(kb, qs, nt, preferred_element_type=F32)
        if running_max:
            m_ref, = m_scratch
            m_new = jnp.max(st, axis=0, keepdims=True)
            if not first:
                m_prev = m_ref[...]
                m_new = jnp.maximum(m_prev, m_new)
                alpha = jnp.exp2(m_prev - m_new)
            m_ref[...] = m_new
            p = jnp.exp2(st - m_new)
        else:
            p = jnp.exp2(st - bound_ref[0])
        psum = jnp.sum(p.reshape(p.shape[0] // SUBLANES_F32, SUBLANES_F32, width), axis=0)
        pv = _dot(vtb, p.astype(BF16))
        if first:
            l_ref[...] = psum
            acc_ref[...] = pv
        elif running_max:
            l_ref[...] = alpha * l_ref[...] + psum
            acc_ref[...] = alpha * acc_ref[...] + pv
        else:
            l_ref[...] += psum
            acc_ref[...] += pv

    block(km_ref[0], vmt_ref[...], True)

    def body(j, carry):
        start = pl.multiple_of(j * tk, tk)
        block(k_ref[0, pl.ds(start, tk), :], vt_ref[:, pl.ds(start, tk)], False)
        return carry

    lax.fori_loop(0, nkv, body, 0)

    o = acc_ref[...] * (1.0 / jnp.sum(l_ref[...], axis=0, keepdims=True))
    for g in range(Q_PER_KV):
        o_ref[0, :, g * HEAD_DIM:(g + 1) * HEAD_DIM] = o[:, g * tq:(g + 1) * tq].T.astype(BF16)


def _attention(bound, q, k, vt, km, vmt, batch, tq, tk, shared_q, running_max):
    n = k.shape[1] // batch
    nq = q.shape[1] if shared_q else q.shape[1] // batch
    nkv = n // tk
    nqt = nq // tq
    width = Q_PER_KV * tq
    if shared_q:
        q_map = lambda bi, h, i: (h, i, 0)
    else:
        q_map = lambda bi, h, i: (h, bi * nqt + i, 0)
    scratch = [pltpu.VMEM((SUBLANES_F32, width), F32), pltpu.VMEM((HEAD_DIM, width), F32)]
    if running_max:
        scratch.append(pltpu.VMEM((1, width), F32))
    return pl.pallas_call(
        functools.partial(_attn_kernel, nkv=nkv, tk=tk, running_max=running_max),
        grid=(batch, N_KV_HEADS, nqt),
        in_specs=[pl.BlockSpec(memory_space=pltpu.SMEM),
                  pl.BlockSpec((Q_PER_KV, tq, HEAD_DIM), q_map),
                  pl.BlockSpec((1, n, HEAD_DIM), lambda bi, h, i: (h, bi, 0)),
                  pl.BlockSpec((HEAD_DIM, n), lambda bi, h, i: (h, bi)),
                  pl.BlockSpec((1, N_META, HEAD_DIM), lambda bi, h, i: (h, 0, 0)),
                  pl.BlockSpec((HEAD_DIM, N_META), lambda bi, h, i: (h, 0))],
        out_specs=pl.BlockSpec((1, tq, Q_PER_KV * HEAD_DIM), lambda bi, h, i: (bi, i, h)),
        out_shape=jax.ShapeDtypeStruct((batch, nq, ATTN_WIDTH), BF16),
        scratch_shapes=scratch,
        compiler_params=pltpu.CompilerParams(
            dimension_semantics=("parallel", "parallel", "arbitrary"),
            vmem_limit_bytes=VMEM_LIMIT_BYTES),
        name="attention",
    )(bound, q, k, vt, km, vmt)


def _softplus(x):
    return jnp.maximum(x, 0.0) + jnp.log1p(jnp.exp(-jnp.abs(x)))


def _lru_prep(xc, rows, wr_ref, br_ref, wi_ref, bi_ref, lam_ref, a_ref, u_ref):
    xcb = xc.astype(BF16)
    r_parts, i_parts = [], []
    for blk in range(LRU_BLOCKS):
        xb = xcb[:, blk * LRU_BLOCK:(blk + 1) * LRU_BLOCK]
        r_parts.append(_dot(xb, wr_ref[blk]))
        i_parts.append(_dot(xb, wi_ref[blk]))
    tr = jnp.tanh(jnp.concatenate(r_parts, axis=-1) + br_ref[...])
    ti = jnp.tanh(jnp.concatenate(i_parts, axis=-1) + bi_ref[...])
    c1 = (-0.5 * LRU_C * LOG2E) * _softplus(-lam_ref[...])
    log2_a = c1 * tr + c1
    a = jnp.exp2(log2_a)
    a_ref[0:rows, :] = a
    gate2 = jnp.tanh((-LN2) * log2_a) * (1.0 + a * a)
    gate = jnp.where(gate2 > 0.0, gate2 * lax.rsqrt(gate2), 0.0)
    half_xc = 0.5 * xc
    u_ref[0:rows, :] = gate * (half_xc * ti + half_xc)


def _lru_scan(rows, a_ref, u_ref, h, out_ref, base, reverse):
    for r in (range(rows - 1, -1, -1) if reverse else range(rows)):
        h = a_ref[r:r + 1, :] * h + u_ref[r:r + 1, :]
        out_ref[pl.ds(base + r, 1), :] = h
    return h


def _lru_fwd_kernel(xc_ref, xcm_ref, wr_ref, br_ref, wi_ref, bi_ref,
                    lam_ref, hf_ref, hfm_ref, a0_ref, u0_ref, a1_ref, u1_ref, h_ref):
    j = pl.program_id(1)
    tc = xc_ref.shape[1]
    wts = (wr_ref, br_ref, wi_ref, bi_ref, lam_ref)
    nsub = tc // LRU_SUB

    @pl.when(j == 0)
    def _():
        _lru_prep(xcm_ref[0], N_META, *wts, a0_ref, u0_ref)
        h_ref[...] = _lru_scan(N_META, a0_ref, u0_ref, jnp.zeros(h_ref.shape, F32), hfm_ref.at[0], 0, False)

    @pl.when(j > 0)
    def _():
        def prep(s, a_ref, u_ref):
            start = pl.multiple_of(s * LRU_SUB, LRU_SUB)
            _lru_prep(xc_ref[0, pl.ds(start, LRU_SUB), :], LRU_SUB, *wts, a_ref, u_ref)

        prep(0, a0_ref, u0_ref)

        def pair(i, h):
            s = 2 * i
            prep(s + 1, a1_ref, u1_ref)
            h = _lru_scan(LRU_SUB, a0_ref, u0_ref, h, hf_ref.at[0], pl.multiple_of(s * LRU_SUB, LRU_SUB), False)
            prep(jnp.minimum(s + 2, nsub - 1), a0_ref, u0_ref)
            return _lru_scan(LRU_SUB, a1_ref, u1_ref, h, hf_ref.at[0],
                             pl.multiple_of((s + 1) * LRU_SUB, LRU_SUB), False)

        h_ref[...] = lax.fori_loop(0, nsub // 2, pair, h_ref[...])


def _lru_bwd_kernel(xc_ref, xcm_ref, gy_ref, gym_ref, hf_ref, hfm_ref,
                    wr_ref, br_ref, wi_ref, bi_ref, lam_ref, og_ref,
                    out_ref, outm_ref, a0_ref, u0_ref, a1_ref, u1_ref, h_ref, hb0_ref, hb1_ref, *, nc):
    j = pl.program_id(1)
    tc = xc_ref.shape[1]
    wts = (wr_ref, br_ref, wi_ref, bi_ref, lam_ref)
    nsub = tc // LRU_SUB

    def finish(hf, hb, gy):
        return _rms((hf + hb) * jax.nn.gelu(gy), og_ref[...]).astype(BF16)

    @pl.when(j == 0)
    def _():
        h_ref[...] = jnp.zeros(h_ref.shape, F32)

    @pl.when(j < nc)
    def _():
        def prep(s, a_ref, u_ref):
            start = pl.multiple_of(s * LRU_SUB, LRU_SUB)
            _lru_prep(xc_ref[0, pl.ds(start, LRU_SUB), :], LRU_SUB, *wts, a_ref, u_ref)

        def piece(s, a_ref, u_ref, hb_ref, h):
            h = _lru_scan(LRU_SUB, a_ref, u_ref, h, hb_ref, 0, True)
            rows = pl.ds(pl.multiple_of(s * LRU_SUB, LRU_SUB), LRU_SUB)
            out_ref[0, rows, :] = finish(hf_ref[0, rows, :], hb_ref[...], gy_ref[0, rows, :])
            return h

        prep(nsub - 1, a0_ref, u0_ref)

        def pair(i, h):
            s = nsub - 1 - 2 * i
            prep(s - 1, a1_ref, u1_ref)
            h = piece(s, a0_ref, u0_ref, hb0_ref, h)
            prep(jnp.maximum(s - 2, 0), a0_ref, u0_ref)
            return piece(s - 1, a1_ref, u1_ref, hb1_ref, h)

        h_ref[...] = lax.fori_loop(0, nsub // 2, pair, h_ref[...])

    @pl.when(j == nc)
    def _():
        _lru_prep(xcm_ref[0], N_META, *wts, a0_ref, u0_ref)
        h_ref[...] = _lru_scan(N_META, a0_ref, u0_ref, h_ref[...], hb0_ref, 0, True)
        outm_ref[0] = finish(hfm_ref[0], hb0_ref[0:N_META, :], gym_ref[...])


def _lru(xc, gy, xcm, gym, w_r, b_r, w_i, b_i, lam, out_gain, tc):
    b, n, c = xc.shape
    nc = n // tc
    wspec = _const_spec((LRU_BLOCKS, LRU_BLOCK, LRU_BLOCK))
    vspec = _const_spec((1, c))
    pspecs = [wspec, vspec, wspec, vspec, vspec]
    bmspec = pl.BlockSpec((1, N_META, c), lambda bi, j: (bi, 0, 0))
    slots = [pltpu.VMEM((LRU_SUB, c), F32)] * 4
    assert tc % (2 * LRU_SUB) == 0 and LRU_SUB >= N_META

    def chunk_f(j):
        return jnp.maximum(j - 1, 0)

    hf, hfm = pl.pallas_call(
        _lru_fwd_kernel,
        grid=(b, nc + 1),
        in_specs=[pl.BlockSpec((1, tc, c), lambda bi, j: (bi, chunk_f(j), 0)), bmspec] + pspecs,
        out_specs=[pl.BlockSpec((1, tc, c), lambda bi, j: (bi, chunk_f(j), 0)), bmspec],
        out_shape=[jax.ShapeDtypeStruct((b, n, c), F32), jax.ShapeDtypeStruct((b, N_META, c), F32)],
        scratch_shapes=slots + [pltpu.VMEM((1, c), F32)],
        compiler_params=pltpu.CompilerParams(dimension_semantics=("parallel", "arbitrary"),
                                             vmem_limit_bytes=VMEM_LIMIT_BYTES),
        name="lru_fwd",
    )(xc, xcm, w_r[0], b_r[0:1], w_i[0], b_i[0:1], lam[0:1])

    def chunk_b(j):
        return jnp.maximum(nc - 1 - j, 0)

    chunk_spec = pl.BlockSpec((1, tc, c), lambda bi, j: (bi, chunk_b(j), 0))
    out, outm = pl.pallas_call(
        functools.partial(_lru_bwd_kernel, nc=nc),
        grid=(b, nc + 1),
        in_specs=[chunk_spec, bmspec, chunk_spec, _const_spec((N_META, c)), chunk_spec, bmspec] + pspecs + [vspec],
        out_specs=[chunk_spec, bmspec],
        out_shape=[jax.ShapeDtypeStruct((b, n, c), BF16), jax.ShapeDtypeStruct((b, N_META, c), BF16)],
        scratch_shapes=slots + [pltpu.VMEM((1, c), F32),
                                pltpu.VMEM((LRU_SUB, c), F32), pltpu.VMEM((LRU_SUB, c), F32)],
        compiler_params=pltpu.CompilerParams(dimension_semantics=("parallel", "arbitrary"),
                                             vmem_limit_bytes=VMEM_LIMIT_BYTES),
        name="lru_bwd",
    )(xc, xcm, gy, gym, hf, hfm, w_r[1], b_r[1:2], w_i[1], b_i[1:2], lam[1:2], out_gain)
    return out, outm


def _out_proj_kernel(attn_ref, lru_ref, res_ref, ag_ref, wa_ref, wl_ref, pg_ref, fg_ref, h1_ref, hn_ref):
    tm = attn_ref.shape[0]
    parts = OUT_PROJ_PARTS if tm % (OUT_PROJ_PARTS * LANES) == 0 else 1
    part = tm // parts
    for s in range(parts):
        rows = slice(s * part, (s + 1) * part)
        an = _rms(attn_ref[rows, :].astype(F32), ag_ref[...]).astype(BF16)
        mixed = _dot(an, wa_ref[...]) + _dot(lru_ref[rows, :], wl_ref[...])
        h1 = res_ref[rows, :] + _rms(mixed, pg_ref[...])
        h1_ref[rows, :] = h1
        hn_ref[rows, :] = _rms(h1, fg_ref[...]).astype(BF16)


def _out_proj(attn2d, lru2d, res2d, attn_gain, w_a, w_l, post_gain, ffn_gain, tm):
    rows = attn2d.shape[0]
    row_spec = lambda w: pl.BlockSpec((tm, w), lambda i: (i, 0))
    wspec = pl.BlockSpec((ATTN_WIDTH, D_MODEL), lambda i: (0, 0), pipeline_mode=pl.Buffered(1))
    return pl.pallas_call(
        _out_proj_kernel,
        grid=(rows // tm,),
        in_specs=[row_spec(ATTN_WIDTH), row_spec(LRU_WIDTH), row_spec(D_MODEL), _const_spec((1, ATTN_WIDTH)),
                  wspec, wspec, _const_spec((1, D_MODEL)), _const_spec((1, D_MODEL))],
        out_specs=[row_spec(D_MODEL), row_spec(D_MODEL)],
        out_shape=[jax.ShapeDtypeStruct((rows, D_MODEL), F32), jax.ShapeDtypeStruct((rows, D_MODEL), BF16)],
        compiler_params=pltpu.CompilerParams(dimension_semantics=("parallel",),
                                             vmem_limit_bytes=VMEM_LIMIT_BYTES),
        name="out_proj",
    )(attn2d, lru2d, res2d, attn_gain, w_a, w_l, post_gain, ffn_gain)


def _ffn_kernel(hn_ref, prev_ref, next_ref, hm_ref, h1_hbm, wg_ref, wv_ref, wd_ref, cw_ref, cb_ref, og_ref,
                out_ref, ext_ref, h1_ref, h1_sem, *, nt, nf):
    bi = pl.program_id(0)
    i = pl.program_id(1)
    f = pl.program_id(2)
    tm = hn_ref.shape[1]
    halo = SUBLANES_BF16
    acc_ref = out_ref.at[0]

    def h1_copy():
        return pltpu.make_async_copy(h1_hbm.at[bi, pl.ds(i * tm, tm), :], h1_ref, h1_sem)

    @pl.when(f == 0)
    def _():
        h1_copy().start()
        ext_ref[0:halo, :] = jnp.where(i == 0, hm_ref[0], prev_ref[0])
        ext_ref[halo:halo + tm, :] = hn_ref[0]
        ext_ref[halo + tm:2 * halo + tm, :] = jnp.where(i == nt - 1, jnp.zeros_like(next_ref[0]), next_ref[0])
        acc_ref[...] = jnp.zeros(acc_ref.shape, F32)

    n_ext = tm + 2 * halo
    gfull = _dot(ext_ref[...], wg_ref[...])
    g = jnp.broadcast_to(cb_ref[...], (tm, gfull.shape[1]))
    for kk in range(FFN_CONV_W):
        shift = (FFN_CONV_W // 2 - kk) % n_ext
        gk = gfull if shift == 0 else pltpu.roll(gfull, shift, axis=0)
        g = g + gk[halo:halo + tm] * cw_ref[kk:kk + 1, :]
    val = _dot(ext_ref[halo:halo + tm, :], wv_ref[...])
    act = (jax.nn.silu(g) * val).astype(BF16)
    acc_ref[...] += _dot(act, wd_ref[...])

    @pl.when(f == nf - 1)
    def _():
        h1_copy().wait()
        acc_ref[...] = h1_ref[...] + _rms(acc_ref[...], og_ref[...])


def _ffn(hn, hnm, h1, w_up, w_down, conv_w, conv_b, out_gain, tm, fc):
    b, n, d = hn.shape
    nt = n // tm
    nf = FFN_DIM // fc
    halo = SUBLANES_BF16
    th = tm // halo
    nh = n // halo
    return pl.pallas_call(
        functools.partial(_ffn_kernel, nt=nt, nf=nf),
        grid=(b, nt, nf),
        in_specs=[pl.BlockSpec((1, tm, d), lambda bi, i, f: (bi, i, 0), pipeline_mode=pl.Buffered(1)),
                  pl.BlockSpec((1, halo, d), lambda bi, i, f: (bi, jnp.maximum(i * th - 1, 0), 0)),
                  pl.BlockSpec((1, halo, d), lambda bi, i, f: (bi, jnp.minimum((i + 1) * th, nh - 1), 0)),
                  pl.BlockSpec((1, N_META, d), lambda bi, i, f: (bi, 0, 0)),
                  pl.BlockSpec(memory_space=pl.ANY),
                  pl.BlockSpec((d, fc), lambda bi, i, f: (0, f)),
                  pl.BlockSpec((d, fc), lambda bi, i, f: (0, f + nf)),
                  pl.BlockSpec((fc, d), lambda bi, i, f: (f, 0)),
                  pl.BlockSpec((FFN_CONV_W, fc), lambda bi, i, f: (0, f)),
                  pl.BlockSpec((1, fc), lambda bi, i, f: (0, f)),
                  _const_spec((1, d))],
        out_specs=pl.BlockSpec((1, tm, d), lambda bi, i, f: (bi, i, 0)),
        out_shape=jax.ShapeDtypeStruct((b, n, d), F32),
        scratch_shapes=[pltpu.VMEM((tm + 2 * halo, d), BF16), pltpu.VMEM((tm, d), F32),
                        pltpu.SemaphoreType.DMA(())],
        compiler_params=pltpu.CompilerParams(
            dimension_semantics=("parallel", "parallel", "arbitrary"),
            vmem_limit_bytes=VMEM_LIMIT_BYTES),
        name="conv_ffn",
    )(hn, hn, hn, hnm, h1, w_up, w_up, w_down, conv_w, conv_b, out_gain)


def _rope_tables(n):
    rows = n // GRID_W
    t_row = jnp.repeat(jnp.arange(rows), GRID_W).astype(F32)
    t_col = jnp.tile(jnp.arange(GRID_W), rows).astype(F32)
    half = HEAD_DIM // 2
    inv = ROPE_THETA ** (-jnp.arange(0, half, 2, dtype=F32) / half)
    ang = jnp.concatenate([t_row[:, None] * inv, t_col[:, None] * inv], axis=-1)
    cos, sin = jnp.cos(ang), jnp.sin(ang)
    return jnp.concatenate([cos, cos], axis=-1), jnp.concatenate([-sin, sin], axis=-1)


_TILES = dict(tm=512, tq=512, tk=1024, tc=512, tf=1024, fc=512)


def _pick(n, pref):
    t = pref
    while n % t:
        t //= 2
    return t


def _trunk(x, meta, mp, p):
    b, n, d = x.shape
    rows = b * n
    cos_t, sin_t = _rope_tables(n)
    tm = _pick(n, _TILES["tm"])
    nt = n // tm
    q, k, vt, xc, gy, xc_head = _in_proj(x.reshape(rows, d), meta, nt, p["norm_pre_mix"], p["w_in"], p["q_norm"],
                                         p["k_norm"], cos_t, sin_t, p["lru_conv_w"], p["lru_conv_b"], tm)
    head = xc_head.reshape(b, nt, SUBLANES_BF16, LRU_WIDTH)[:, 0, SUBLANES_BF16 - 2:, :]
    xc_m = jnp.concatenate([jnp.broadcast_to(mp["xc"][None, :N_META - 2], (b, N_META - 2, LRU_WIDTH)), head], axis=1)
    tq = _pick(n, _TILES["tq"])
    tk = _pick(n, _TILES["tk"])
    bound = mp["bound"]

    def attend(running_max):
        def run(bound, q, qm, k, vt, km, vmt):
            real = _attention(bound, q, k, vt, km, vmt, b, tq, tk, False, running_max)
            meta_rows = _attention(bound, qm, k, vt, km, vmt, b, LANES, tk, True, running_max)
            return real, meta_rows
        return run

    attn, attn_m = lax.cond(2.0 * bound[0] <= MAX_FIXED_SHIFT_RANGE, attend(False), attend(True),
                            bound, q, mp["q"], k, vt, mp["km"], mp["vmt"])
    attn_m = attn_m[:, :N_META]
    lru, lru_m = _lru(xc.reshape(b, n, LRU_WIDTH), gy.reshape(b, n, LRU_WIDTH), xc_m, mp["gy"],
                      p["lru_w_r"], p["lru_b_r"], p["lru_w_i"], p["lru_b_i"],
                      p["lru_lambda"], p["lru_out_norm"], _pick(n, _TILES["tc"]))
    op = (p["attn_out_norm"], p["w_out_a"], p["w_out_l"], p["norm_post_mix"], p["norm_pre_ffn"])
    h1, hn = _out_proj(attn.reshape(rows, ATTN_WIDTH), lru.reshape(rows, LRU_WIDTH), x.reshape(rows, d), *op, tm)
    res_m = jnp.broadcast_to(meta[None], (b, N_META, d)).reshape(b * N_META, d)
    _, hn_m = _out_proj(attn_m.reshape(b * N_META, ATTN_WIDTH), lru_m.reshape(b * N_META, LRU_WIDTH), res_m,
                        *op, N_META)
    return _ffn(hn.reshape(b, n, d), hn_m.reshape(b, N_META, d), h1.reshape(b, n, d), p["w_up"], p["w_down"],
                p["ffn_conv_w"], p["ffn_conv_b"], p["norm_post_ffn"], _pick(n, _TILES["tf"]), _TILES["fc"])


def _head_perm():
    q4 = HEAD_DIM // 4
    idx = jnp.arange(HEAD_DIM).reshape(2, 2, q4)
    return idx.transpose(1, 0, 2).reshape(HEAD_DIM)


def kernel(x_prompt, x_sample, meta_tokens, norm_pre_mix, w_in, q_norm, k_norm, lru_conv_w, lru_conv_b,
           lru_w_r, lru_b_r, lru_w_i, lru_b_i, lru_lambda, attn_out_norm, lru_out_norm, w_out,
           norm_post_mix, norm_pre_ffn, w_up, ffn_conv_w, ffn_conv_b, w_down, norm_post_ffn):
    perm = _head_perm()
    n_rot = N_Q_HEADS + N_KV_HEADS
    cols = (jnp.arange(n_rot)[:, None] * HEAD_DIM + perm[None, :]).reshape(-1)
    cols = jnp.concatenate([cols, jnp.arange(n_rot * HEAD_DIM, IN_WIDTH)])
    w_out_b = w_out[0].astype(BF16)
    p = {
        "norm_pre_mix": norm_pre_mix[0][None], "w_in": w_in[0][:, cols].astype(BF16),
        "q_norm": q_norm[0][perm][None], "k_norm": k_norm[0][perm][None],
        "lru_conv_w": lru_conv_w[0], "lru_conv_b": lru_conv_b[0][None],
        "lru_w_r": (0.5 * lru_w_r[0]).astype(BF16), "lru_b_r": 0.5 * lru_b_r[0],
        "lru_w_i": (0.5 * lru_w_i[0]).astype(BF16), "lru_b_i": 0.5 * lru_b_i[0], "lru_lambda": lru_lambda[0],
        "attn_out_norm": attn_out_norm[0][None], "lru_out_norm": lru_out_norm[0][None],
        "w_out_a": w_out_b[:ATTN_WIDTH], "w_out_l": w_out_b[ATTN_WIDTH:],
        "norm_post_mix": norm_post_mix[0][None], "norm_pre_ffn": norm_pre_ffn[0][None],
        "w_up": w_up[0].astype(BF16), "ffn_conv_w": ffn_conv_w[0], "ffn_conv_b": ffn_conv_b[0][None],
        "w_down": w_down[0].astype(BF16), "norm_post_ffn": norm_post_ffn[0][None],
    }
    ones = jnp.ones((LANES, HEAD_DIM), F32)
    meta_pad = jnp.pad(meta_tokens, ((0, LANES - N_META), (0, 0)))
    qm, km, vtm, xcm, gym, _ = _in_proj(meta_pad, jnp.zeros((SUBLANES_BF16, D_MODEL), F32), 1, p["norm_pre_mix"],
                                        p["w_in"], p["q_norm"], p["k_norm"], ones, jnp.zeros_like(ones),
                                        p["lru_conv_w"], p["lru_conv_b"], LANES)
    bound = (HEAD_DIM * Q_PRESCALE * BF16_ROUNDING_SLACK) * jnp.max(jnp.abs(q_norm[0])) * jnp.max(jnp.abs(k_norm[0]))
    mp = {"q": qm, "km": km[:, :N_META], "vmt": vtm[:, :N_META], "xc": xcm[:N_META], "gy": gym[:N_META],
          "bound": bound.reshape(1).astype(F32)}
    return _trunk(x_prompt, meta_tokens, mp, p), _trunk(x_sample, meta_tokens, mp, p)
```

```python
import functools

import jax
import jax.numpy as jnp
from jax import lax
from jax.experimental import pallas as pl
from jax.experimental.pallas import tpu as pltpu

D_MODEL = 2048
N_META = 16
GRID_W = 64
HEAD_DIM = 128
N_Q_HEADS = 8
N_KV_HEADS = 2
Q_PER_KV = N_Q_HEADS // N_KV_HEADS
ATTN_WIDTH = N_Q_HEADS * HEAD_DIM
KV_WIDTH = N_KV_HEADS * HEAD_DIM
LRU_WIDTH = D_MODEL - ATTN_WIDTH
LRU_BLOCKS = 8
LRU_BLOCK = LRU_WIDTH // LRU_BLOCKS
LRU_CONV_W = 4
LRU_C = 8.0
LRU_SUB = 64
OUT_PROJ_PARTS = 2
IN_WIDTH = ATTN_WIDTH + 2 * KV_WIDTH + 2 * LRU_WIDTH
FFN_DIM = 5632
FFN_CONV_W = 3
ROPE_THETA = 10000.0
EPS = 1e-6
ATTN_SCALE = HEAD_DIM ** -0.5
LOG2E = 1.4426950408889634
LN2 = 0.6931471805599453
Q_PRESCALE = ATTN_SCALE * LOG2E
MAX_FIXED_SHIFT_RANGE = 120.0
BF16_ROUNDING_SLACK = 1.0 + 2.0 ** -6

LANES = 128
SUBLANES_F32 = 8
SUBLANES_BF16 = 16
VMEM_LIMIT_BYTES = 56 * 1024 * 1024

F32 = jnp.float32
BF16 = jnp.bfloat16


def _rms(x, g):
    ms = jnp.mean(x * x, axis=-1, keepdims=True)
    return x * lax.rsqrt(ms + EPS) * g


def _dot(a, b):
    return jnp.dot(a, b, preferred_element_type=F32)


def _const_spec(shape):
    zeros = (0,) * len(shape)
    return pl.BlockSpec(shape, lambda *_: zeros)


def _in_proj_kernel(x_ref, xprev_ref, xnext_ref, xhead_ref, g_ref, w_ref, qg_ref, kg_ref, cos_ref, sin_ref,
                    cw_ref, cb_ref, q_ref, k_ref, vt_ref, xc_ref, gy_ref, xch_ref, *, nt):
    tm = x_ref.shape[0]
    halo = SUBLANES_BF16
    pos = pl.program_id(0) % nt
    xprev = jnp.where(pos == 0, xhead_ref[...], xprev_ref[...])
    xnext = jnp.where(pos == nt - 1, jnp.zeros_like(xnext_ref[...]), xnext_ref[...])
    xn = _rms(x_ref[...], g_ref[...]).astype(BF16)
    xn_ext = jnp.concatenate([_rms(xprev, g_ref[...]).astype(BF16), xn, _rms(xnext, g_ref[...]).astype(BF16)], axis=0)
    cos = cos_ref[...]
    sin = sin_ref[...]

    def rope_head(zh, gain):
        y = _rms(zh, gain)
        return y * cos + pltpu.roll(y, HEAD_DIM // 2, axis=1) * sin

    o3 = ATTN_WIDTH + 2 * KV_WIDTH
    zq = _dot(xn, w_ref[:, 0:ATTN_WIDTH])
    for h in range(N_Q_HEADS):
        sl = slice(h * HEAD_DIM, (h + 1) * HEAD_DIM)
        q_ref[h] = (rope_head(zq[:, sl], qg_ref[...]) * Q_PRESCALE).astype(BF16)
    xr_ext = _dot(xn_ext, w_ref[:, o3:o3 + LRU_WIDTH])
    n_ext = tm + 2 * halo
    xc_ext = jnp.broadcast_to(cb_ref[...], (n_ext, LRU_WIDTH))
    for kk in range(LRU_CONV_W):
        shift = (LRU_CONV_W // 2 - kk) % n_ext
        xk = xr_ext if shift == 0 else pltpu.roll(xr_ext, shift, axis=0)
        xc_ext = xc_ext + xk * cw_ref[kk:kk + 1, :]
    xc_ref[...] = xc_ext[halo:halo + tm]
    xch_ref[...] = xc_ext[0:halo]
    zkv = _dot(xn, w_ref[:, ATTN_WIDTH:ATTN_WIDTH + 2 * KV_WIDTH])
    for h in range(N_KV_HEADS):
        sl = slice(h * HEAD_DIM, (h + 1) * HEAD_DIM)
        k_ref[h] = rope_head(zkv[:, sl], kg_ref[...]).astype(BF16)
    vt_ref[...] = zkv[:, KV_WIDTH:2 * KV_WIDTH].T.astype(BF16)
    gy_ref[...] = _dot(xn, w_ref[:, o3 + LRU_WIDTH:o3 + 2 * LRU_WIDTH])


def _in_proj(x2d, xhead, nt, gain, w_in, q_gain, k_gain, cos_t, sin_t, conv_w, conv_b, tm):
    rows = x2d.shape[0]
    n_tab = cos_t.shape[0] // tm
    halo = SUBLANES_BF16
    th = tm // halo
    nh = rows // halo
    row_spec = lambda w: pl.BlockSpec((tm, w), lambda i: (i, 0))
    tab_spec = pl.BlockSpec((tm, HEAD_DIM), lambda i: (i % n_tab, 0))
    return pl.pallas_call(
        functools.partial(_in_proj_kernel, nt=nt),
        grid=(rows // tm,),
        in_specs=[row_spec(D_MODEL),
                  pl.BlockSpec((halo, D_MODEL), lambda i: (jnp.maximum(i * th - 1, 0), 0)),
                  pl.BlockSpec((halo, D_MODEL), lambda i: (jnp.minimum((i + 1) * th, nh - 1), 0)),
                  _const_spec((halo, D_MODEL)), _const_spec((1, D_MODEL)),
                  pl.BlockSpec((D_MODEL, IN_WIDTH), lambda i: (0, 0), pipeline_mode=pl.Buffered(1)),
                  _const_spec((1, HEAD_DIM)), _const_spec((1, HEAD_DIM)), tab_spec, tab_spec,
                  _const_spec((LRU_CONV_W, LRU_WIDTH)), _const_spec((1, LRU_WIDTH))],
        out_specs=[pl.BlockSpec((N_Q_HEADS, tm, HEAD_DIM), lambda i: (0, i, 0)),
                   pl.BlockSpec((N_KV_HEADS, tm, HEAD_DIM), lambda i: (0, i, 0)),
                   pl.BlockSpec((KV_WIDTH, tm), lambda i: (0, i)),
                   row_spec(LRU_WIDTH), row_spec(LRU_WIDTH),
                   pl.BlockSpec((halo, LRU_WIDTH), lambda i: (i, 0))],
        out_shape=[jax.ShapeDtypeStruct((N_Q_HEADS, rows, HEAD_DIM), BF16),
                   jax.ShapeDtypeStruct((N_KV_HEADS, rows, HEAD_DIM), BF16),
                   jax.ShapeDtypeStruct((KV_WIDTH, rows), BF16),
                   jax.ShapeDtypeStruct((rows, LRU_WIDTH), F32),
                   jax.ShapeDtypeStruct((rows, LRU_WIDTH), F32),
                   jax.ShapeDtypeStruct((rows // tm * halo, LRU_WIDTH), F32)],
        compiler_params=pltpu.CompilerParams(dimension_semantics=("parallel",),
                                             vmem_limit_bytes=VMEM_LIMIT_BYTES),
        name="in_proj",
    )(x2d, x2d, x2d, xhead, gain, w_in, q_gain, k_gain, cos_t, sin_t, conv_w, conv_b)


def _attn_kernel(bound_ref, q_ref, k_ref, vt_ref, km_ref, vmt_ref, o_ref, l_ref, acc_ref, *m_scratch,
                 nkv, tk, running_max):
    tq = q_ref.shape[1]
    width = Q_PER_KV * tq
    qs = q_ref[...].reshape(width, HEAD_DIM)
    nt = (((1,), (1,)), ((), ()))

    def block(kb, vtb, first):
        st = lax.dot_general(kb, qs, nt, preferred_element_type=F32)
        if running_max:
            m_ref, = m_scratch
            m_new = jnp.max(st, axis=0, keepdims=True)
            if not first:
                m_prev = m_ref[...]
                m_new = jnp.maximum(m_prev, m_new)
                alpha = jnp.exp2(m_prev - m_new)
            m_ref[...] = m_new
            p = jnp.exp2(st - m_new)
        else:
            p = jnp.exp2(st - bound_ref[0])
        psum = jnp.sum(p.reshape(p.shape[0] // SUBLANES_F32, SUBLANES_F32, width), axis=0)
        pv = _dot(vtb, p.astype(BF16))
        if first:
            l_ref[...] = psum
            acc_ref[...] = pv
        elif running_max:
            l_ref[...] = alpha * l_ref[...] + psum
            acc_ref[...] = alpha * acc_ref[...] + pv
        else:
            l_ref[...] += psum
            acc_ref[...] += pv

    block(km_ref[0], vmt_ref[...], True)

    def body(j, carry):
        start = pl.multiple_of(j * tk, tk)
        block(k_ref[0, pl.ds(start, tk), :], vt_ref[:, pl.ds(start, tk)], False)
        return carry

    lax.fori_loop(0, nkv, body, 0)

    o = acc_ref[...] * (1.0 / jnp.sum(l_ref[...], axis=0, keepdims=True))
    for g in range(Q_PER_KV):
        o_ref[0, :, g * HEAD_DIM:(g + 1) * HEAD_DIM] = o[:, g * tq:(g + 1) * tq].T.astype(BF16)


def _attention(bound, q, k, vt, km, vmt, batch, tq, tk, shared_q, running_max):
    n = k.shape[1] // batch
    nq = q.shape[1] if shared_q else q.shape[1] // batch
    nkv = n // tk
    nqt = nq // tq
    width = Q_PER_KV * tq
    if shared_q:
        q_map = lambda bi, h, i: (h, i, 0)
    else:
        q_map = lambda bi, h, i: (h, bi * nqt + i, 0)
    scratch = [pltpu.VMEM((SUBLANES_F32, width), F32), pltpu.VMEM((HEAD_DIM, width), F32)]
    if running_max:
        scratch.append(pltpu.VMEM((1, width), F32))
    return pl.pallas_call(
        functools.partial(_attn_kernel, nkv=nkv, tk=tk, running_max=running_max),
        grid=(batch, N_KV_HEADS, nqt),
        in_specs=[pl.BlockSpec(memory_space=pltpu.SMEM),
                  pl.BlockSpec((Q_PER_KV, tq, HEAD_DIM), q_map),
                  pl.BlockSpec((1, n, HEAD_DIM), lambda bi, h, i: (h, bi, 0)),
                  pl.BlockSpec((HEAD_DIM, n), lambda bi, h, i: (h, bi)),
                  pl.BlockSpec((1, N_META, HEAD_DIM), lambda bi, h, i: (h, 0, 0)),
                  pl.BlockSpec((HEAD_DIM, N_META), lambda bi, h, i: (h, 0))],
        out_specs=pl.BlockSpec((1, tq, Q_PER_KV * HEAD_DIM), lambda bi, h, i: (bi, i, h)),
        out_shape=jax.ShapeDtypeStruct((batch, nq, ATTN_WIDTH), BF16),
        scratch_shapes=scratch,
        compiler_params=pltpu.CompilerParams(
            dimension_semantics=("parallel", "parallel", "arbitrary"),
            vmem_limit_bytes=VMEM_LIMIT_BYTES),
        name="attention",
    )(bound, q, k, vt, km, vmt)


def _softplus(x):
    return jnp.maximum(x, 0.0) + jnp.log1p(jnp.exp(-jnp.abs(x)))


def _lru_prep(xc, rows, wr_ref, br_ref, wi_ref, bi_ref, lam_ref, a_ref, u_ref):
    xcb = xc.astype(BF16)
    r_parts, i_parts = [], []
    for blk in range(LRU_BLOCKS):
        xb = xcb[:, blk * LRU_BLOCK:(blk + 1) * LRU_BLOCK]
        r_parts.append(_dot(xb, wr_ref[blk]))
        i_parts.append(_dot(xb, wi_ref[blk]))
    tr = jnp.tanh(jnp.concatenate(r_parts, axis=-1) + br_ref[...])
    ti = jnp.tanh(jnp.concatenate(i_parts, axis=-1) + bi_ref[...])
    c1 = (-0.5 * LRU_C * LOG2E) * _softplus(-lam_ref[...])
    log2_a = c1 * tr + c1
    a = jnp.exp2(log2_a)
    a_ref[0:rows, :] = a
    gate2 = jnp.tanh((-LN2) * log2_a) * (1.0 + a * a)
    gate = jnp.where(gate2 > 0.0, gate2 * lax.rsqrt(gate2), 0.0)
    half_xc = 0.5 * xc
    u_ref[0:rows, :] = gate * (half_xc * ti + half_xc)


def _lru_scan(rows, a_ref, u_ref, h, out_ref, base, reverse):
    for r in (range(rows - 1, -1, -1) if reverse else range(rows)):
        h = a_ref[r:r + 1, :] * h + u_ref[r:r + 1, :]
        out_ref[pl.ds(base + r, 1), :] = h
    return h


def _lru_fwd_kernel(xc_ref, xcm_ref, wr_ref, br_ref, wi_ref, bi_ref,
                    lam_ref, hf_ref, hfm_ref, a0_ref, u0_ref, a1_ref, u1_ref, h_ref):
    j = pl.program_id(1)
    tc = xc_ref.shape[1]
    wts = (wr_ref, br_ref, wi_ref, bi_ref, lam_ref)
    nsub = tc // LRU_SUB

    @pl.when(j == 0)
    def _():
        _lru_prep(xcm_ref[0], N_META, *wts, a0_ref, u0_ref)
        h_ref[...] = _lru_scan(N_META, a0_ref, u0_ref, jnp.zeros(h_ref.shape, F32), hfm_ref.at[0], 0, False)

    @pl.when(j > 0)
    def _():
        def prep(s, a_ref, u_ref):
            start = pl.multiple_of(s * LRU_SUB, LRU_SUB)
            _lru_prep(xc_ref[0, pl.ds(start, LRU_SUB), :], LRU_SUB, *wts, a_ref, u_ref)

        prep(0, a0_ref, u0_ref)

        def pair(i, h):
            s = 2 * i
            prep(s + 1, a1_ref, u1_ref)
            h = _lru_scan(LRU_SUB, a0_ref, u0_ref, h, hf_ref.at[0], pl.multiple_of(s * LRU_SUB, LRU_SUB), False)
            prep(jnp.minimum(s + 2, nsub - 1), a0_ref, u0_ref)
            return _lru_scan(LRU_SUB, a1_ref, u1_ref, h, hf_ref.at[0],
                             pl.multiple_of((s + 1) * LRU_SUB, LRU_SUB), False)

        h_ref[...] = lax.fori_loop(0, nsub // 2, pair, h_ref[...])


def _lru_bwd_kernel(xc_ref, xcm_ref, gy_ref, gym_ref, hf_ref, hfm_ref,
                    wr_ref, br_ref, wi_ref, bi_ref, lam_ref, og_ref,
                    out_ref, outm_ref, a0_ref, u0_ref, a1_ref, u1_ref, h_ref, hb0_ref, hb1_ref, *, nc):
    j = pl.program_id(1)
    tc = xc_ref.shape[1]
    wts = (wr_ref, br_ref, wi_ref, bi_ref, lam_ref)
    nsub = tc // LRU_SUB

    def finish(hf, hb, gy):
        return _rms((hf + hb) * jax.nn.gelu(gy), og_ref[...]).astype(BF16)

    @pl.when(j == 0)
    def _():
        h_ref[...] = jnp.zeros(h_ref.shape, F32)

    @pl.when(j < nc)
    def _():
        def prep(s, a_ref, u_ref):
            start = pl.multiple_of(s * LRU_SUB, LRU_SUB)
            _lru_prep(xc_ref[0, pl.ds(start, LRU_SUB), :], LRU_SUB, *wts, a_ref, u_ref)

        def piece(s, a_ref, u_ref, hb_ref, h):
            h = _lru_scan(LRU_SUB, a_ref, u_ref, h, hb_ref, 0, True)
            rows = pl.ds(pl.multiple_of(s * LRU_SUB, LRU_SUB), LRU_SUB)
            out_ref[0, rows, :] = finish(hf_ref[0, rows, :], hb_ref[...], gy_ref[0, rows, :])
            return h

        prep(nsub - 1, a0_ref, u0_ref)

        def pair(i, h):
            s = nsub - 1 - 2 * i
            prep(s - 1, a1_ref, u1_ref)
            h = piece(s, a0_ref, u0_ref, hb0_ref, h)
            prep(jnp.maximum(s - 2, 0), a0_ref, u0_ref)
            return piece(s - 1, a1_ref, u1_ref, hb1_ref, h)

        h_ref[...] = lax.fori_loop(0, nsub // 2, pair, h_ref[...])

    @pl.when(j == nc)
    def _():
        _lru_prep(xcm_ref[0], N_META, *wts, a0_ref, u0_ref)
        h_ref[...] = _lru_scan(N_META, a0_ref, u0_ref, h_ref[...], hb0_ref, 0, True)
        outm_ref[0] = finish(hfm_ref[0], hb0_ref[0:N_META, :], gym_ref[...])


def _lru(xc, gy, xcm, gym, w_r, b_r, w_i, b_i, lam, out_gain, tc):
    b, n, c = xc.shape
    nc = n // tc
    wspec = _const_spec((LRU_BLOCKS, LRU_BLOCK, LRU_BLOCK))
    vspec = _const_spec((1, c))
    pspecs = [wspec, vspec, wspec, vspec, vspec]
    bmspec = pl.BlockSpec((1, N_META, c), lambda bi, j: (bi, 0, 0))
    slots = [pltpu.VMEM((LRU_SUB, c), F32)] * 4
    assert tc % (2 * LRU_SUB) == 0 and LRU_SUB >= N_META

    def chunk_f(j):
        return jnp.maximum(j - 1, 0)

    hf, hfm = pl.pallas_call(
        _lru_fwd_kernel,
        grid=(b, nc + 1),
        in_specs=[pl.BlockSpec((1, tc, c), lambda bi, j: (bi, chunk_f(j), 0)), bmspec] + pspecs,
        out_specs=[pl.BlockSpec((1, tc, c), lambda bi, j: (bi, chunk_f(j), 0)), bmspec],
        out_shape=[jax.ShapeDtypeStruct((b, n, c), F32), jax.ShapeDtypeStruct((b, N_META, c), F32)],
        scratch_shapes=slots + [pltpu.VMEM((1, c), F32)],
        compiler_params=pltpu.CompilerParams(dimension_semantics=("parallel", "arbitrary"),
                                             vmem_limit_bytes=VMEM_LIMIT_BYTES),
        name="lru_fwd",
    )(xc, xcm, w_r[0], b_r[0:1], w_i[0], b_i[0:1], lam[0:1])

    def chunk_b(j):
        return jnp.maximum(nc - 1 - j, 0)

    chunk_spec = pl.BlockSpec((1, tc, c), lambda bi, j: (bi, chunk_b(j), 0))
    out, outm = pl.pallas_call(
        functools.partial(_lru_bwd_kernel, nc=nc),
        grid=(b, nc + 1),
        in_specs=[chunk_spec, bmspec, chunk_spec, _const_spec((N_META, c)), chunk_spec, bmspec] + pspecs + [vspec],
        out_specs=[chunk_spec, bmspec],
        out_shape=[jax.ShapeDtypeStruct((b, n, c), BF16), jax.ShapeDtypeStruct((b, N_META, c), BF16)],
        scratch_shapes=slots + [pltpu.VMEM((1, c), F32),
                                pltpu.VMEM((LRU_SUB, c), F32), pltpu.VMEM((LRU_SUB, c), F32)],
        compiler_params=pltpu.CompilerParams(dimension_semantics=("parallel", "arbitrary"),
                                             vmem_limit_bytes=VMEM_LIMIT_BYTES),
        name="lru_bwd",
    )(xc, xcm, gy, gym, hf, hfm, w_r[1], b_r[1:2], w_i[1], b_i[1:2], lam[1:2], out_gain)
    return out, outm


def _out_proj_kernel(attn_ref, lru_ref, res_ref, ag_ref, wa_ref, wl_ref, pg_ref, fg_ref, h1_ref, hn_ref):
    tm = attn_ref.shape[0]
    parts = OUT_PROJ_PARTS if tm % (OUT_PROJ_PARTS * LANES) == 0 else 1
    part = tm // parts
    for s in range(parts):
        rows = slice(s * part, (s + 1) * part)
        an = _rms(attn_ref[rows, :].astype(F32), ag_ref[...]).astype(BF16)
        mixed = _dot(an, wa_ref[...]) + _dot(lru_ref[rows, :], wl_ref[...])
        h1 = res_ref[rows, :] + _rms(mixed, pg_ref[...])
        h1_ref[rows, :] = h1
        hn_ref[rows, :] = _rms(h1, fg_ref[...]).astype(BF16)


def _out_proj(attn2d, lru2d, res2d, attn_gain, w_a, w_l, post_gain, ffn_gain, tm):
    rows = attn2d.shape[0]
    row_spec = lambda w: pl.BlockSpec((tm, w), lambda i: (i, 0))
    wspec = pl.BlockSpec((ATTN_WIDTH, D_MODEL), lambda i: (0, 0), pipeline_mode=pl.Buffered(1))
    return pl.pallas_call(
        _out_proj_kernel,
        grid=(rows // tm,),
        in_specs=[row_spec(ATTN_WIDTH), row_spec(LRU_WIDTH), row_spec(D_MODEL), _const_spec((1, ATTN_WIDTH)),
                  wspec, wspec, _const_spec((1, D_MODEL)), _const_spec((1, D_MODEL))],
        out_specs=[row_spec(D_MODEL), row_spec(D_MODEL)],
        out_shape=[jax.ShapeDtypeStruct((rows, D_MODEL), F32), jax.ShapeDtypeStruct((rows, D_MODEL), BF16)],
        compiler_params=pltpu.CompilerParams(dimension_semantics=("parallel",),
                                             vmem_limit_bytes=VMEM_LIMIT_BYTES),
        name="out_proj",
    )(attn2d, lru2d, res2d, attn_gain, w_a, w_l, post_gain, ffn_gain)


def _ffn_kernel(hn_ref, prev_ref, next_ref, hm_ref, h1_ref, wg_ref, wv_ref, wd_ref, cw_ref, cb_ref, og_ref,
                out_ref, ext_ref, *, nt, nf):
    i = pl.program_id(1)
    f = pl.program_id(2)
    tm = hn_ref.shape[1]
    halo = SUBLANES_BF16
    acc_ref = out_ref.at[0]

    @pl.when(f == 0)
    def _():
        ext_ref[0:halo, :] = jnp.where(i == 0, hm_ref[0], prev_ref[0])
        ext_ref[halo:halo + tm, :] = hn_ref[0]
        ext_ref[halo + tm:2 * halo + tm, :] = jnp.where(i == nt - 1, jnp.zeros_like(next_ref[0]), next_ref[0])
        acc_ref[...] = jnp.zeros(acc_ref.shape, F32)

    n_ext = tm + 2 * halo
    gfull = _dot(ext_ref[...], wg_ref[...])
    g = jnp.broadcast_to(cb_ref[...], (tm, gfull.shape[1]))
    for kk in range(FFN_CONV_W):
        shift = (FFN_CONV_W // 2 - kk) % n_ext
        gk = gfull if shift == 0 else pltpu.roll(gfull, shift, axis=0)
        g = g + gk[halo:halo + tm] * cw_ref[kk:kk + 1, :]
    val = _dot(ext_ref[halo:halo + tm, :], wv_ref[...])
    act = (jax.nn.silu(g) * val).astype(BF16)
    acc_ref[...] += _dot(act, wd_ref[...])

    @pl.when(f == nf - 1)
    def _():
        acc_ref[...] = h1_ref[0] + _rms(acc_ref[...], og_ref[...])


def _ffn(hn, hnm, h1, w_up, w_down, conv_w, conv_b, out_gain, tm, fc):
    b, n, d = hn.shape
    nt = n // tm
    nf = FFN_DIM // fc
    halo = SUBLANES_BF16
    th = tm // halo
    nh = n // halo
    return pl.pallas_call(
        functools.partial(_ffn_kernel, nt=nt, nf=nf),
        grid=(b, nt, nf),
        in_specs=[pl.BlockSpec((1, tm, d), lambda bi, i, f: (bi, i, 0)),
                  pl.BlockSpec((1, halo, d), lambda bi, i, f: (bi, jnp.maximum(i * th - 1, 0), 0)),
                  pl.BlockSpec((1, halo, d), lambda bi, i, f: (bi, jnp.minimum((i + 1) * th, nh - 1), 0)),
                  pl.BlockSpec((1, N_META, d), lambda bi, i, f: (bi, 0, 0)),
                  pl.BlockSpec((1, tm, d), lambda bi, i, f: (bi, i, 0)),
                  pl.BlockSpec((d, fc), lambda bi, i, f: (0, f)),
                  pl.BlockSpec((d, fc), lambda bi, i, f: (0, f + nf)),
                  pl.BlockSpec((fc, d), lambda bi, i, f: (f, 0)),
                  pl.BlockSpec((FFN_CONV_W, fc), lambda bi, i, f: (0, f)),
                  pl.BlockSpec((1, fc), lambda bi, i, f: (0, f)),
                  _const_spec((1, d))],
        out_specs=pl.BlockSpec((1, tm, d), lambda bi, i, f: (bi, i, 0)),
        out_shape=jax.ShapeDtypeStruct((b, n, d), F32),
        scratch_shapes=[pltpu.VMEM((tm + 2 * halo, d), BF16)],
        compiler_params=pltpu.CompilerParams(
            dimension_semantics=("parallel", "parallel", "arbitrary"),
            vmem_limit_bytes=VMEM_LIMIT_BYTES),
        name="conv_ffn",
    )(hn, hn, hn, hnm, h1, w_up, w_up, w_down, conv_w, conv_b, out_gain)


def _rope_tables(n):
    rows = n // GRID_W
    t_row = jnp.repeat(jnp.arange(rows), GRID_W).astype(F32)
    t_col = jnp.tile(jnp.arange(GRID_W), rows).astype(F32)
    half = HEAD_DIM // 2
    inv = ROPE_THETA ** (-jnp.arange(0, half, 2, dtype=F32) / half)
    ang = jnp.concatenate([t_row[:, None] * inv, t_col[:, None] * inv], axis=-1)
    cos, sin = jnp.cos(ang), jnp.sin(ang)
    return jnp.concatenate([cos, cos], axis=-1), jnp.concatenate([-sin, sin], axis=-1)


_TILES = dict(tm=512, tq=512, tk=1024, tc=1024, tf=512, fc=512)


def _pick(n, pref):
    t = pref
    while n % t:
        t //= 2
    return t


def _trunk(x, meta, mp, p):
    b, n, d = x.shape
    rows = b * n
    cos_t, sin_t = _rope_tables(n)
    tm = _pick(n, _TILES["tm"])
    nt = n // tm
    q, k, vt, xc, gy, xc_head = _in_proj(x.reshape(rows, d), meta, nt, p["norm_pre_mix"], p["w_in"], p["q_norm"],
                                         p["k_norm"], cos_t, sin_t, p["lru_conv_w"], p["lru_conv_b"], tm)
    head = xc_head.reshape(b, nt, SUBLANES_BF16, LRU_WIDTH)[:, 0, SUBLANES_BF16 - 2:, :]
    xc_m = jnp.concatenate([jnp.broadcast_to(mp["xc"][None, :N_META - 2], (b, N_META - 2, LRU_WIDTH)), head], axis=1)
    tq = _pick(n, _TILES["tq"])
    tk = _pick(n, _TILES["tk"])
    bound = mp["bound"]

    def attend(running_max):
        def run(bound, q, qm, k, vt, km, vmt):
            real = _attention(bound, q, k, vt, km, vmt, b, tq, tk, False, running_max)
            meta_rows = _attention(bound, qm, k, vt, km, vmt, b, LANES, tk, True, running_max)
            return real, meta_rows
        return run

    attn, attn_m = lax.cond(2.0 * bound[0] <= MAX_FIXED_SHIFT_RANGE, attend(False), attend(True),
                            bound, q, mp["q"], k, vt, mp["km"], mp["vmt"])
    attn_m = attn_m[:, :N_META]
    lru, lru_m = _lru(xc.reshape(b, n, LRU_WIDTH), gy.reshape(b, n, LRU_WIDTH), xc_m, mp["gy"],
                      p["lru_w_r"], p["lru_b_r"], p["lru_w_i"], p["lru_b_i"],
                      p["lru_lambda"], p["lru_out_norm"], _pick(n, _TILES["tc"]))
    op = (p["attn_out_norm"], p["w_out_a"], p["w_out_l"], p["norm_post_mix"], p["norm_pre_ffn"])
    h1, hn = _out_proj(attn.reshape(rows, ATTN_WIDTH), lru.reshape(rows, LRU_WIDTH), x.reshape(rows, d), *op, tm)
    res_m = jnp.broadcast_to(meta[None], (b, N_META, d)).reshape(b * N_META, d)
    _, hn_m = _out_proj(attn_m.reshape(b * N_META, ATTN_WIDTH), lru_m.reshape(b * N_META, LRU_WIDTH), res_m,
                        *op, N_META)
    return _ffn(hn.reshape(b, n, d), hn_m.reshape(b, N_META, d), h1.reshape(b, n, d), p["w_up"], p["w_down"],
                p["ffn_conv_w"], p["ffn_conv_b"], p["norm_post_ffn"], _pick(n, _TILES["tf"]), _TILES["fc"])


def _head_perm():
    q4 = HEAD_DIM // 4
    idx = jnp.arange(HEAD_DIM).reshape(2, 2, q4)
    return idx.transpose(1, 0, 2).reshape(HEAD_DIM)


def kernel(x_prompt, x_sample, meta_tokens, norm_pre_mix, w_in, q_norm, k_norm, lru_conv_w, lru_conv_b,
           lru_w_r, lru_b_r, lru_w_i, lru_b_i, lru_lambda, attn_out_norm, lru_out_norm, w_out,
           norm_post_mix, norm_pre_ffn, w_up, ffn_conv_w, ffn_conv_b, w_down, norm_post_ffn):
    perm = _head_perm()
    n_rot = N_Q_HEADS + N_KV_HEADS
    cols = (jnp.arange(n_rot)[:, None] * HEAD_DIM + perm[None, :]).reshape(-1)
    cols = jnp.concatenate([cols, jnp.arange(n_rot * HEAD_DIM, IN_WIDTH)])
    w_out_b = w_out[0].astype(BF16)
    p = {
        "norm_pre_mix": norm_pre_mix[0][None], "w_in": w_in[0][:, cols].astype(BF16),
        "q_norm": q_norm[0][perm][None], "k_norm": k_norm[0][perm][None],
        "lru_conv_w": lru_conv_w[0], "lru_conv_b": lru_conv_b[0][None],
        "lru_w_r": (0.5 * lru_w_r[0]).astype(BF16), "lru_b_r": 0.5 * lru_b_r[0],
        "lru_w_i": (0.5 * lru_w_i[0]).astype(BF16), "lru_b_i": 0.5 * lru_b_i[0], "lru_lambda": lru_lambda[0],
        "attn_out_norm": attn_out_norm[0][None], "lru_out_norm": lru_out_norm[0][None],
        "w_out_a": w_out_b[:ATTN_WIDTH], "w_out_l": w_out_b[ATTN_WIDTH:],
        "norm_post_mix": norm_post_mix[0][None], "norm_pre_ffn": norm_pre_ffn[0][None],
        "w_up": w_up[0].astype(BF16), "ffn_conv_w": ffn_conv_w[0], "ffn_conv_b": ffn_conv_b[0][None],
        "w_down": w_down[0].astype(BF16), "norm_post_ffn": norm_post_ffn[0][None],
    }
    ones = jnp.ones((LANES, HEAD_DIM), F32)
    meta_pad = jnp.pad(meta_tokens, ((0, LANES - N_META), (0, 0)))
    qm, km, vtm, xcm, gym, _ = _in_proj(meta_pad, jnp.zeros((SUBLANES_BF16, D_MODEL), F32), 1, p["norm_pre_mix"],
                                        p["w_in"], p["q_norm"], p["k_norm"], ones, jnp.zeros_like(ones),
                                        p["lru_conv_w"], p["lru_conv_b"], LANES)
    bound = (HEAD_DIM * Q_PRESCALE * BF16_ROUNDING_SLACK) * jnp.max(jnp.abs(q_norm[0])) * jnp.max(jnp.abs(k_norm[0]))
    mp = {"q": qm, "km": km[:, :N_META], "vmt": vtm[:, :N_META], "xc": xcm[:N_META], "gy": gym[:N_META],
          "bound": bound.reshape(1).astype(F32)}
    return _trunk(x_prompt, meta_tokens, mp, p), _trunk(x_sample, meta_tokens, mp, p)
```

```python
import functools

import jax
import jax.numpy as jnp
from jax import lax
from jax.experimental import pallas as pl
from jax.experimental.pallas import tpu as pltpu

D_MODEL = 2048
N_META = 16
GRID_W = 64
HEAD_DIM = 128
N_Q_HEADS = 8
N_KV_HEADS = 2
Q_PER_KV = N_Q_HEADS // N_KV_HEADS
ATTN_WIDTH = N_Q_HEADS * HEAD_DIM
KV_WIDTH = N_KV_HEADS * HEAD_DIM
LRU_WIDTH = D_MODEL - ATTN_WIDTH
LRU_BLOCKS = 8
LRU_BLOCK = LRU_WIDTH // LRU_BLOCKS
LRU_CONV_W = 4
LRU_C = 8.0
LRU_SUB = 64
OUT_PROJ_PARTS = 2
IN_WIDTH = ATTN_WIDTH + 2 * KV_WIDTH + 2 * LRU_WIDTH
FFN_DIM = 5632
FFN_CONV_W = 3
ROPE_THETA = 10000.0
EPS = 1e-6
ATTN_SCALE = HEAD_DIM ** -0.5
LOG2E = 1.4426950408889634
LN2 = 0.6931471805599453
Q_PRESCALE = ATTN_SCALE * LOG2E
MAX_FIXED_SHIFT_RANGE = 120.0
BF16_ROUNDING_SLACK = 1.0 + 2.0 ** -6

LANES = 128
SUBLANES_F32 = 8
SUBLANES_BF16 = 16
VMEM_LIMIT_BYTES = 56 * 1024 * 1024

F32 = jnp.float32
BF16 = jnp.bfloat16


def _rms(x, g):
    ms = jnp.mean(x * x, axis=-1, keepdims=True)
    return x * lax.rsqrt(ms + EPS) * g


def _dot(a, b):
    return jnp.dot(a, b, preferred_element_type=F32)


def _const_spec(shape):
    zeros = (0,) * len(shape)
    return pl.BlockSpec(shape, lambda *_: zeros)


def _in_proj_kernel(x_ref, xprev_ref, xnext_ref, xhead_ref, g_ref, w_ref, qg_ref, kg_ref, cos_ref, sin_ref,
                    cw_ref, cb_ref, q_ref, k_ref, vt_ref, xc_ref, gy_ref, xch_ref, *, nt):
    tm = x_ref.shape[0]
    halo = SUBLANES_BF16
    pos = pl.program_id(0) % nt
    xprev = jnp.where(pos == 0, xhead_ref[...], xprev_ref[...])
    xnext = jnp.where(pos == nt - 1, jnp.zeros_like(xnext_ref[...]), xnext_ref[...])
    xn = _rms(x_ref[...], g_ref[...]).astype(BF16)
    xn_ext = jnp.concatenate([_rms(xprev, g_ref[...]).astype(BF16), xn, _rms(xnext, g_ref[...]).astype(BF16)], axis=0)
    cos = cos_ref[...]
    sin = sin_ref[...]

    def rope_head(zh, gain):
        y = _rms(zh, gain)
        return y * cos + pltpu.roll(y, HEAD_DIM // 2, axis=1) * sin

    o3 = ATTN_WIDTH + 2 * KV_WIDTH
    zq = _dot(xn, w_ref[:, 0:ATTN_WIDTH])
    for h in range(N_Q_HEADS):
        sl = slice(h * HEAD_DIM, (h + 1) * HEAD_DIM)
        q_ref[h] = (rope_head(zq[:, sl], qg_ref[...]) * Q_PRESCALE).astype(BF16)
    xr_ext = _dot(xn_ext, w_ref[:, o3:o3 + LRU_WIDTH])
    n_ext = tm + 2 * halo
    xc_ext = jnp.broadcast_to(cb_ref[...], (n_ext, LRU_WIDTH))
    for kk in range(LRU_CONV_W):
        shift = (LRU_CONV_W // 2 - kk) % n_ext
        xk = xr_ext if shift == 0 else pltpu.roll(xr_ext, shift, axis=0)
        xc_ext = xc_ext + xk * cw_ref[kk:kk + 1, :]
    xc_ref[...] = xc_ext[halo:halo + tm]
    xch_ref[...] = xc_ext[0:halo]
    zkv = _dot(xn, w_ref[:, ATTN_WIDTH:ATTN_WIDTH + 2 * KV_WIDTH])
    for h in range(N_KV_HEADS):
        sl = slice(h * HEAD_DIM, (h + 1) * HEAD_DIM)
        k_ref[h] = rope_head(zkv[:, sl], kg_ref[...]).astype(BF16)
    vt_ref[...] = zkv[:, KV_WIDTH:2 * KV_WIDTH].T.astype(BF16)
    gy_ref[...] = _dot(xn, w_ref[:, o3 + LRU_WIDTH:o3 + 2 * LRU_WIDTH])


def _in_proj(x2d, xhead, nt, gain, w_in, q_gain, k_gain, cos_t, sin_t, conv_w, conv_b, tm):
    rows = x2d.shape[0]
    n_tab = cos_t.shape[0] // tm
    halo = SUBLANES_BF16
    th = tm // halo
    nh = rows // halo
    row_spec = lambda w: pl.BlockSpec((tm, w), lambda i: (i, 0))
    tab_spec = pl.BlockSpec((tm, HEAD_DIM), lambda i: (i % n_tab, 0))
    return pl.pallas_call(
        functools.partial(_in_proj_kernel, nt=nt),
        grid=(rows // tm,),
        in_specs=[row_spec(D_MODEL),
                  pl.BlockSpec((halo, D_MODEL), lambda i: (jnp.maximum(i * th - 1, 0), 0)),
                  pl.BlockSpec((halo, D_MODEL), lambda i: (jnp.minimum((i + 1) * th, nh - 1), 0)),
                  _const_spec((halo, D_MODEL)), _const_spec((1, D_MODEL)),
                  pl.BlockSpec((D_MODEL, IN_WIDTH), lambda i: (0, 0), pipeline_mode=pl.Buffered(1)),
                  _const_spec((1, HEAD_DIM)), _const_spec((1, HEAD_DIM)), tab_spec, tab_spec,
                  _const_spec((LRU_CONV_W, LRU_WIDTH)), _const_spec((1, LRU_WIDTH))],
        out_specs=[pl.BlockSpec((N_Q_HEADS, tm, HEAD_DIM), lambda i: (0, i, 0)),
                   pl.BlockSpec((N_KV_HEADS, tm, HEAD_DIM), lambda i: (0, i, 0)),
                   pl.BlockSpec((KV_WIDTH, tm), lambda i: (0, i)),
                   row_spec(LRU_WIDTH), row_spec(LRU_WIDTH),
                   pl.BlockSpec((halo, LRU_WIDTH), lambda i: (i, 0))],
        out_shape=[jax.ShapeDtypeStruct((N_Q_HEADS, rows, HEAD_DIM), BF16),
                   jax.ShapeDtypeStruct((N_KV_HEADS, rows, HEAD_DIM), BF16),
                   jax.ShapeDtypeStruct((KV_WIDTH, rows), BF16),
                   jax.ShapeDtypeStruct((rows, LRU_WIDTH), F32),
                   jax.ShapeDtypeStruct((rows, LRU_WIDTH), F32),
                   jax.ShapeDtypeStruct((rows // tm * halo, LRU_WIDTH), F32)],
        compiler_params=pltpu.CompilerParams(dimension_semantics=("parallel",),
                                             vmem_limit_bytes=VMEM_LIMIT_BYTES),
        name="in_proj",
    )(x2d, x2d, x2d, xhead, gain, w_in, q_gain, k_gain, cos_t, sin_t, conv_w, conv_b)


def _attn_kernel(bound_ref, q_ref, k_ref, vt_ref, km_ref, vmt_ref, o_ref, l_ref, acc_ref, *m_scratch,
                 nkv, tk, running_max):
    tq = q_ref.shape[1]
    width = Q_PER_KV * tq
    qs = q_ref[...].reshape(width, HEAD_DIM)
    nt = (((1,), (1,)), ((), ()))

    def block(kb, vtb, first):
        st = lax.dot_general(kb, qs, nt, preferred_element_type=F32)
        if running_max:
            m_ref, = m_scratch
            m_new = jnp.max(st, axis=0, keepdims=True)
            if not first:
                m_prev = m_ref[...]
                m_new = jnp.maximum(m_prev, m_new)
                alpha = jnp.exp2(m_prev - m_new)
            m_ref[...] = m_new
            p = jnp.exp2(st - m_new)
        else:
            p = jnp.exp2(st - bound_ref[0])
        psum = jnp.sum(p.reshape(p.shape[0] // SUBLANES_F32, SUBLANES_F32, width), axis=0)
        pv = _dot(vtb, p.astype(BF16))
        if first:
            l_ref[...] = psum
            acc_ref[...] = pv
        elif running_max:
            l_ref[...] = alpha * l_ref[...] + psum
            acc_ref[...] = alpha * acc_ref[...] + pv
        else:
            l_ref[...] += psum
            acc_ref[...] += pv

    block(km_ref[0], vmt_ref[...], True)

    def body(j, carry):
        start = pl.multiple_of(j * tk, tk)
        block(k_ref[0, pl.ds(start, tk), :], vt_ref[:, pl.ds(start, tk)], False)
        return carry

    lax.fori_loop(0, nkv, body, 0)

    o = acc_ref[...] * (1.0 / jnp.sum(l_ref[...], axis=0, keepdims=True))
    for g in range(Q_PER_KV):
        o_ref[0, :, g * HEAD_DIM:(g + 1) * HEAD_DIM] = o[:, g * tq:(g + 1) * tq].T.astype(BF16)


def _attention(bound, q, k, vt, km, vmt, batch, tq, tk, shared_q, running_max):
    n = k.shape[1] // batch
    nq = q.shape[1] if shared_q else q.shape[1] // batch
    nkv = n // tk
    nqt = nq // tq
    width = Q_PER_KV * tq
    if shared_q:
        q_map = lambda bi, h, i: (h, i, 0)
    else:
        q_map = lambda bi, h, i: (h, bi * nqt + i, 0)
    scratch = [pltpu.VMEM((SUBLANES_F32, width), F32), pltpu.VMEM((HEAD_DIM, width), F32)]
    if running_max:
        scratch.append(pltpu.VMEM((1, width), F32))
    return pl.pallas_call(
        functools.partial(_attn_kernel, nkv=nkv, tk=tk, running_max=running_max),
        grid=(batch, N_KV_HEADS, nqt),
        in_specs=[pl.BlockSpec(memory_space=pltpu.SMEM),
                  pl.BlockSpec((Q_PER_KV, tq, HEAD_DIM), q_map),
                  pl.BlockSpec((1, n, HEAD_DIM), lambda bi, h, i: (h, bi, 0)),
                  pl.BlockSpec((HEAD_DIM, n), lambda bi, h, i: (h, bi)),
                  pl.BlockSpec((1, N_META, HEAD_DIM), lambda bi, h, i: (h, 0, 0)),
                  pl.BlockSpec((HEAD_DIM, N_META), lambda bi, h, i: (h, 0))],
        out_specs=pl.BlockSpec((1, tq, Q_PER_KV * HEAD_DIM), lambda bi, h, i: (bi, i, h)),
        out_shape=jax.ShapeDtypeStruct((batch, nq, ATTN_WIDTH), BF16),
        scratch_shapes=scratch,
        compiler_params=pltpu.CompilerParams(
            dimension_semantics=("parallel", "parallel", "arbitrary"),
            vmem_limit_bytes=VMEM_LIMIT_BYTES),
        name="attention",
    )(bound, q, k, vt, km, vmt)


def _softplus(x):
    return jnp.maximum(x, 0.0) + jnp.log1p(jnp.exp(-jnp.abs(x)))


def _lru_prep(xc, rows, wr_ref, br_ref, wi_ref, bi_ref, lam_ref, a_ref, u_ref):
    xcb = xc.astype(BF16)
    r_parts, i_parts = [], []
    for blk in range(LRU_BLOCKS):
        xb = xcb[:, blk * LRU_BLOCK:(blk + 1) * LRU_BLOCK]
        r_parts.append(_dot(xb, wr_ref[blk]))
        i_parts.append(_dot(xb, wi_ref[blk]))
    tr = jnp.tanh(jnp.concatenate(r_parts, axis=-1) + br_ref[...])
    ti = jnp.tanh(jnp.concatenate(i_parts, axis=-1) + bi_ref[...])
    c1 = (-0.5 * LRU_C * LOG2E) * _softplus(-lam_ref[...])
    log2_a = c1 * tr + c1
    a = jnp.exp2(log2_a)
    a_ref[0:rows, :] = a
    gate2 = jnp.tanh((-LN2) * log2_a) * (1.0 + a * a)
    gate = jnp.where(gate2 > 0.0, gate2 * lax.rsqrt(gate2), 0.0)
    half_xc = 0.5 * xc
    u_ref[0:rows, :] = gate * (half_xc * ti + half_xc)


def _lru_scan(rows, a_ref, u_ref, h, out_ref, base, reverse):
    for r in (range(rows - 1, -1, -1) if reverse else range(rows)):
        h = a_ref[r:r + 1, :] * h + u_ref[r:r + 1, :]
        out_ref[pl.ds(base + r, 1), :] = h
    return h


def _lru_fwd_kernel(xc_ref, xcm_ref, wr_ref, br_ref, wi_ref, bi_ref,
                    lam_ref, hf_ref, hfm_ref, a0_ref, u0_ref, a1_ref, u1_ref, h_ref):
    j = pl.program_id(1)
    tc = xc_ref.shape[1]
    wts = (wr_ref, br_ref, wi_ref, bi_ref, lam_ref)
    nsub = tc // LRU_SUB

    @pl.when(j == 0)
    def _():
        _lru_prep(xcm_ref[0], N_META, *wts, a0_ref, u0_ref)
        h_ref[...] = _lru_scan(N_META, a0_ref, u0_ref, jnp.zeros(h_ref.shape, F32), hfm_ref.at[0], 0, False)

    @pl.when(j > 0)
    def _():
        def prep(s, a_ref, u_ref):
            start = pl.multiple_of(s * LRU_SUB, LRU_SUB)
            _lru_prep(xc_ref[0, pl.ds(start, LRU_SUB), :], LRU_SUB, *wts, a_ref, u_ref)

        prep(0, a0_ref, u0_ref)

        def pair(i, h):
            s = 2 * i
            prep(s + 1, a1_ref, u1_ref)
            h = _lru_scan(LRU_SUB, a0_ref, u0_ref, h, hf_ref.at[0], pl.multiple_of(s * LRU_SUB, LRU_SUB), False)
            prep(jnp.minimum(s + 2, nsub - 1), a0_ref, u0_ref)
            return _lru_scan(LRU_SUB, a1_ref, u1_ref, h, hf_ref.at[0],
                             pl.multiple_of((s + 1) * LRU_SUB, LRU_SUB), False)

        h_ref[...] = lax.fori_loop(0, nsub // 2, pair, h_ref[...])


def _lru_bwd_kernel(xc_ref, xcm_ref, gy_ref, gym_ref, hf_ref, hfm_ref,
                    wr_ref, br_ref, wi_ref, bi_ref, lam_ref, og_ref,
                    out_ref, outm_ref, a0_ref, u0_ref, a1_ref, u1_ref, h_ref, hb0_ref, hb1_ref, *, nc):
    j = pl.program_id(1)
    tc = xc_ref.shape[1]
    wts = (wr_ref, br_ref, wi_ref, bi_ref, lam_ref)
    nsub = tc // LRU_SUB

    def finish(hf, hb, gy):
        return _rms((hf + hb) * jax.nn.gelu(gy), og_ref[...]).astype(BF16)

    @pl.when(j == 0)
    def _():
        h_ref[...] = jnp.zeros(h_ref.shape, F32)

    @pl.when(j < nc)
    def _():
        def prep(s, a_ref, u_ref):
            start = pl.multiple_of(s * LRU_SUB, LRU_SUB)
            _lru_prep(xc_ref[0, pl.ds(start, LRU_SUB), :], LRU_SUB, *wts, a_ref, u_ref)

        def piece(s, a_ref, u_ref, hb_ref, h):
            h = _lru_scan(LRU_SUB, a_ref, u_ref, h, hb_ref, 0, True)
            rows = pl.ds(pl.multiple_of(s * LRU_SUB, LRU_SUB), LRU_SUB)
            out_ref[0, rows, :] = finish(hf_ref[0, rows, :], hb_ref[...], gy_ref[0, rows, :])
            return h

        prep(nsub - 1, a0_ref, u0_ref)

        def pair(i, h):
            s = nsub - 1 - 2 * i
            prep(s - 1, a1_ref, u1_ref)
            h = piece(s, a0_ref, u0_ref, hb0_ref, h)
            prep(jnp.maximum(s - 2, 0), a0_ref, u0_ref)
            return piece(s - 1, a1_ref, u1_ref, hb1_ref, h)

        h_ref[...] = lax.fori_loop(0, nsub // 2, pair, h_ref[...])

    @pl.when(j == nc)
    def _():
        _lru_prep(xcm_ref[0], N_META, *wts, a0_ref, u0_ref)
        h_ref[...] = _lru_scan(N_META, a0_ref, u0_ref, h_ref[...], hb0_ref, 0, True)
        outm_ref[0] = finish(hfm_ref[0], hb0_ref[0:N_META, :], gym_ref[...])


def _lru(xc, gy, xcm, gym, w_r, b_r, w_i, b_i, lam, out_gain, tc):
    b, n, c = xc.shape
    nc = n // tc
    wspec = _const_spec((LRU_BLOCKS, LRU_BLOCK, LRU_BLOCK))
    vspec = _const_spec((1, c))
    pspecs = [wspec, vspec, wspec, vspec, vspec]
    bmspec = pl.BlockSpec((1, N_META, c), lambda bi, j: (bi, 0, 0))
    slots = [pltpu.VMEM((LRU_SUB, c), F32)] * 4
    assert tc % (2 * LRU_SUB) == 0 and LRU_SUB >= N_META

    def chunk_f(j):
        return jnp.maximum(j - 1, 0)

    hf, hfm = pl.pallas_call(
        _lru_fwd_kernel,
        grid=(b, nc + 1),
        in_specs=[pl.BlockSpec((1, tc, c), lambda bi, j: (bi, chunk_f(j), 0)), bmspec] + pspecs,
        out_specs=[pl.BlockSpec((1, tc, c), lambda bi, j: (bi, chunk_f(j), 0)), bmspec],
        out_shape=[jax.ShapeDtypeStruct((b, n, c), F32), jax.ShapeDtypeStruct((b, N_META, c), F32)],
        scratch_shapes=slots + [pltpu.VMEM((1, c), F32)],
        compiler_params=pltpu.CompilerParams(dimension_semantics=("parallel", "arbitrary"),
                                             vmem_limit_bytes=VMEM_LIMIT_BYTES),
        name="lru_fwd",
    )(xc, xcm, w_r[0], b_r[0:1], w_i[0], b_i[0:1], lam[0:1])

    def chunk_b(j):
        return jnp.maximum(nc - 1 - j, 0)

    chunk_spec = pl.BlockSpec((1, tc, c), lambda bi, j: (bi, chunk_b(j), 0))
    out, outm = pl.pallas_call(
        functools.partial(_lru_bwd_kernel, nc=nc),
        grid=(b, nc + 1),
        in_specs=[chunk_spec, bmspec, chunk_spec, _const_spec((N_META, c)), chunk_spec, bmspec] + pspecs + [vspec],
        out_specs=[chunk_spec, bmspec],
        out_shape=[jax.ShapeDtypeStruct((b, n, c), BF16), jax.ShapeDtypeStruct((b, N_META, c), BF16)],
        scratch_shapes=slots + [pltpu.VMEM((1, c), F32),
                                pltpu.VMEM((LRU_SUB, c), F32), pltpu.VMEM((LRU_SUB, c), F32)],
        compiler_params=pltpu.CompilerParams(dimension_semantics=("parallel", "arbitrary"),
                                             vmem_limit_bytes=VMEM_LIMIT_BYTES),
        name="lru_bwd",
    )(xc, xcm, gy, gym, hf, hfm, w_r[1], b_r[1:2], w_i[1], b_i[1:2], lam[1:2], out_gain)
    return out, outm


def _out_proj_kernel(attn_ref, lru_ref, res_ref, ag_ref, wa_ref, wl_ref, pg_ref, fg_ref, h1_ref, hn_ref):
    tm = attn_ref.shape[0]
    parts = OUT_PROJ_PARTS if tm % (OUT_PROJ_PARTS * LANES) == 0 else 1
    part = tm // parts
    for s in range(parts):
        rows = slice(s * part, (s + 1) * part)
        an = _rms(attn_ref[rows, :].astype(F32), ag_ref[...]).astype(BF16)
        mixed = _dot(an, wa_ref[...]) + _dot(lru_ref[rows, :], wl_ref[...])
        h1 = res_ref[rows, :] + _rms(mixed, pg_ref[...])
        h1_ref[rows, :] = h1
        hn_ref[rows, :] = _rms(h1, fg_ref[...]).astype(BF16)


def _out_proj(attn2d, lru2d, res2d, attn_gain, w_a, w_l, post_gain, ffn_gain, tm):
    rows = attn2d.shape[0]
    row_spec = lambda w: pl.BlockSpec((tm, w), lambda i: (i, 0))
    wspec = pl.BlockSpec((ATTN_WIDTH, D_MODEL), lambda i: (0, 0), pipeline_mode=pl.Buffered(1))
    return pl.pallas_call(
        _out_proj_kernel,
        grid=(rows // tm,),
        in_specs=[row_spec(ATTN_WIDTH), row_spec(LRU_WIDTH), row_spec(D_MODEL), _const_spec((1, ATTN_WIDTH)),
                  wspec, wspec, _const_spec((1, D_MODEL)), _const_spec((1, D_MODEL))],
        out_specs=[row_spec(D_MODEL), row_spec(D_MODEL)],
        out_shape=[jax.ShapeDtypeStruct((rows, D_MODEL), F32), jax.ShapeDtypeStruct((rows, D_MODEL), BF16)],
        compiler_params=pltpu.CompilerParams(dimension_semantics=("parallel",),
                                             vmem_limit_bytes=VMEM_LIMIT_BYTES),
        name="out_proj",
    )(attn2d, lru2d, res2d, attn_gain, w_a, w_l, post_gain, ffn_gain)


def _ffn_kernel(hn_ref, prev_ref, next_ref, hm_ref, h1_ref, wu_ref, wd_ref, cw_ref, cb_ref, og_ref,
                out_ref, ext_ref, *, nt, nf):
    i = pl.program_id(1)
    f = pl.program_id(2)
    tm = hn_ref.shape[1]
    fc = wd_ref.shape[0]
    halo = SUBLANES_BF16
    acc_ref = out_ref.at[0]
    cols = pl.ds(pl.multiple_of(f * fc, fc), fc)

    @pl.when(f == 0)
    def _():
        ext_ref[0:halo, :] = jnp.where(i == 0, hm_ref[0], prev_ref[0])
        ext_ref[halo:halo + tm, :] = hn_ref[0]
        ext_ref[halo + tm:2 * halo + tm, :] = jnp.where(i == nt - 1, jnp.zeros_like(next_ref[0]), next_ref[0])
        acc_ref[...] = jnp.zeros(acc_ref.shape, F32)

    n_ext = tm + 2 * halo
    gfull = _dot(ext_ref[...], wu_ref[:, 0:fc])
    g = jnp.broadcast_to(cb_ref[:, cols], (tm, fc))
    for kk in range(FFN_CONV_W):
        shift = (FFN_CONV_W // 2 - kk) % n_ext
        gk = gfull if shift == 0 else pltpu.roll(gfull, shift, axis=0)
        g = g + gk[halo:halo + tm] * cw_ref[kk:kk + 1, cols]
    val = _dot(ext_ref[halo:halo + tm, :], wu_ref[:, fc:2 * fc])
    act = (jax.nn.silu(g) * val).astype(BF16)
    acc_ref[...] += _dot(act, wd_ref[...])

    @pl.when(f == nf - 1)
    def _():
        acc_ref[...] = h1_ref[0] + _rms(acc_ref[...], og_ref[...])


def _ffn(hn, hnm, h1, w_up, w_down, conv_w, conv_b, out_gain, tm, fc):
    b, n, d = hn.shape
    nt = n // tm
    nf = FFN_DIM // fc
    halo = SUBLANES_BF16
    th = tm // halo
    nh = n // halo
    return pl.pallas_call(
        functools.partial(_ffn_kernel, nt=nt, nf=nf),
        grid=(b, nt, nf),
        in_specs=[pl.BlockSpec((1, tm, d), lambda bi, i, f: (bi, i, 0)),
                  pl.BlockSpec((1, halo, d), lambda bi, i, f: (bi, jnp.maximum(i * th - 1, 0), 0)),
                  pl.BlockSpec((1, halo, d), lambda bi, i, f: (bi, jnp.minimum((i + 1) * th, nh - 1), 0)),
                  pl.BlockSpec((1, N_META, d), lambda bi, i, f: (bi, 0, 0)),
                  pl.BlockSpec((1, tm, d), lambda bi, i, f: (bi, i, 0)),
                  pl.BlockSpec((d, 2 * fc), lambda bi, i, f: (0, f)),
                  pl.BlockSpec((fc, d), lambda bi, i, f: (f, 0)),
                  _const_spec((FFN_CONV_W, FFN_DIM)), _const_spec((1, FFN_DIM)), _const_spec((1, d))],
        out_specs=pl.BlockSpec((1, tm, d), lambda bi, i, f: (bi, i, 0)),
        out_shape=jax.ShapeDtypeStruct((b, n, d), F32),
        scratch_shapes=[pltpu.VMEM((tm + 2 * halo, d), BF16)],
        compiler_params=pltpu.CompilerParams(
            dimension_semantics=("parallel", "parallel", "arbitrary"),
            vmem_limit_bytes=VMEM_LIMIT_BYTES),
        name="conv_ffn",
    )(hn, hn, hn, hnm, h1, w_up, w_down, conv_w, conv_b, out_gain)


def _rope_tables(n):
    rows = n // GRID_W
    t_row = jnp.repeat(jnp.arange(rows), GRID_W).astype(F32)
    t_col = jnp.tile(jnp.arange(GRID_W), rows).astype(F32)
    half = HEAD_DIM // 2
    inv = ROPE_THETA ** (-jnp.arange(0, half, 2, dtype=F32) / half)
    ang = jnp.concatenate([t_row[:, None] * inv, t_col[:, None] * inv], axis=-1)
    cos, sin = jnp.cos(ang), jnp.sin(ang)
    return jnp.concatenate([cos, cos], axis=-1), jnp.concatenate([-sin, sin], axis=-1)


_TILES = dict(tm=512, tq=512, tk=1024, tc=1024, tf=512, fc=512)


def _pick(n, pref):
    t = pref
    while n % t:
        t //= 2
    return t


def _trunk(x, meta, mp, p):
    b, n, d = x.shape
    rows = b * n
    cos_t, sin_t = _rope_tables(n)
    tm = _pick(n, _TILES["tm"])
    nt = n // tm
    q, k, vt, xc, gy, xc_head = _in_proj(x.reshape(rows, d), meta, nt, p["norm_pre_mix"], p["w_in"], p["q_norm"],
                                         p["k_norm"], cos_t, sin_t, p["lru_conv_w"], p["lru_conv_b"], tm)
    head = xc_head.reshape(b, nt, SUBLANES_BF16, LRU_WIDTH)[:, 0, SUBLANES_BF16 - 2:, :]
    xc_m = jnp.concatenate([jnp.broadcast_to(mp["xc"][None, :N_META - 2], (b, N_META - 2, LRU_WIDTH)), head], axis=1)
    tq = _pick(n, _TILES["tq"])
    tk = _pick(n, _TILES["tk"])
    bound = mp["bound"]

    def attend(running_max):
        def run(bound, q, qm, k, vt, km, vmt):
            real = _attention(bound, q, k, vt, km, vmt, b, tq, tk, False, running_max)
            meta_rows = _attention(bound, qm, k, vt, km, vmt, b, LANES, tk, True, running_max)
            return real, meta_rows
        return run

    attn, attn_m = lax.cond(2.0 * bound[0] <= MAX_FIXED_SHIFT_RANGE, attend(False), attend(True),
                            bound, q, mp["q"], k, vt, mp["km"], mp["vmt"])
    attn_m = attn_m[:, :N_META]
    lru, lru_m = _lru(xc.reshape(b, n, LRU_WIDTH), gy.reshape(b, n, LRU_WIDTH), xc_m, mp["gy"],
                      p["lru_w_r"], p["lru_b_r"], p["lru_w_i"], p["lru_b_i"],
                      p["lru_lambda"], p["lru_out_norm"], _pick(n, _TILES["tc"]))
    op = (p["attn_out_norm"], p["w_out_a"], p["w_out_l"], p["norm_post_mix"], p["norm_pre_ffn"])
    h1, hn = _out_proj(attn.reshape(rows, ATTN_WIDTH), lru.reshape(rows, LRU_WIDTH), x.reshape(rows, d), *op, tm)
    res_m = jnp.broadcast_to(meta[None], (b, N_META, d)).reshape(b * N_META, d)
    _, hn_m = _out_proj(attn_m.reshape(b * N_META, ATTN_WIDTH), lru_m.reshape(b * N_META, LRU_WIDTH), res_m,
                        *op, N_META)
    return _ffn(hn.reshape(b, n, d), hn_m.reshape(b, N_META, d), h1.reshape(b, n, d), p["w_up"], p["w_down"],
                p["ffn_conv_w"], p["ffn_conv_b"], p["norm_post_ffn"], _pick(n, _TILES["tf"]), _TILES["fc"])


def _interleave_up(w_up, fc):
    d = w_up.shape[0]
    nf = FFN_DIM // fc
    halves = w_up.reshape(d, 2, nf, fc)
    return halves.transpose(0, 2, 1, 3).reshape(d, 2 * FFN_DIM)


def _permute_heads(w, n_heads):
    lead = w.shape[:-1]
    w = w.reshape(*lead, n_heads, 2, 2, HEAD_DIM // 4)
    return jnp.swapaxes(w, -3, -2).reshape(*lead, n_heads * HEAD_DIM)


def kernel(x_prompt, x_sample, meta_tokens, norm_pre_mix, w_in, q_norm, k_norm, lru_conv_w, lru_conv_b,
           lru_w_r, lru_b_r, lru_w_i, lru_b_i, lru_lambda, attn_out_norm, lru_out_norm, w_out,
           norm_post_mix, norm_pre_ffn, w_up, ffn_conv_w, ffn_conv_b, w_down, norm_post_ffn):
    n_rot = N_Q_HEADS + N_KV_HEADS
    w_in_b = w_in[0].astype(BF16)
    w_in_b = jnp.concatenate([_permute_heads(w_in_b[:, :n_rot * HEAD_DIM], n_rot), w_in_b[:, n_rot * HEAD_DIM:]],
                             axis=1)
    w_out_b = w_out[0].astype(BF16)
    p = {
        "norm_pre_mix": norm_pre_mix[0][None], "w_in": w_in_b,
        "q_norm": _permute_heads(q_norm[0], 1)[None], "k_norm": _permute_heads(k_norm[0], 1)[None],
        "lru_conv_w": lru_conv_w[0], "lru_conv_b": lru_conv_b[0][None],
        "lru_w_r": (0.5 * lru_w_r[0]).astype(BF16), "lru_b_r": 0.5 * lru_b_r[0],
        "lru_w_i": (0.5 * lru_w_i[0]).astype(BF16), "lru_b_i": 0.5 * lru_b_i[0], "lru_lambda": lru_lambda[0],
        "attn_out_norm": attn_out_norm[0][None], "lru_out_norm": lru_out_norm[0][None],
        "w_out_a": w_out_b[:ATTN_WIDTH], "w_out_l": w_out_b[ATTN_WIDTH:],
        "norm_post_mix": norm_post_mix[0][None], "norm_pre_ffn": norm_pre_ffn[0][None],
        "w_up": _interleave_up(w_up[0].astype(BF16), _TILES["fc"]), "ffn_conv_w": ffn_conv_w[0], "ffn_conv_b": ffn_conv_b[0][None],
        "w_down": w_down[0].astype(BF16), "norm_post_ffn": norm_post_ffn[0][None],
    }
    ones = jnp.ones((LANES, HEAD_DIM), F32)
    meta_pad = jnp.pad(meta_tokens, ((0, LANES - N_META), (0, 0)))
    qm, km, vtm, xcm, gym, _ = _in_proj(meta_pad, jnp.zeros((SUBLANES_BF16, D_MODEL), F32), 1, p["norm_pre_mix"],
                                        p["w_in"], p["q_norm"], p["k_norm"], ones, jnp.zeros_like(ones),
                                        p["lru_conv_w"], p["lru_conv_b"], LANES)
    bound = (HEAD_DIM * Q_PRESCALE * BF16_ROUNDING_SLACK) * jnp.max(jnp.abs(q_norm[0])) * jnp.max(jnp.abs(k_norm[0]))
    mp = {"q": qm, "km": km[:, :N_META], "vmt": vtm[:, :N_META], "xc": xcm[:N_META], "gy": gym[:N_META],
          "bound": bound.reshape(1).astype(F32)}
    return _trunk(x_prompt, meta_tokens, mp, p), _trunk(x_sample, meta_tokens, mp, p)
```

```python
import functools

import jax
import jax.numpy as jnp
from jax import lax
from jax.experimental import pallas as pl
from jax.experimental.pallas import tpu as pltpu

D_MODEL = 2048
N_META = 16
GRID_W = 64
HEAD_DIM = 128
N_Q_HEADS = 8
N_KV_HEADS = 2
Q_PER_KV = N_Q_HEADS // N_KV_HEADS
ATTN_WIDTH = N_Q_HEADS * HEAD_DIM
KV_WIDTH = N_KV_HEADS * HEAD_DIM
LRU_WIDTH = D_MODEL - ATTN_WIDTH
LRU_BLOCKS = 8
LRU_BLOCK = LRU_WIDTH // LRU_BLOCKS
LRU_CONV_W = 4
LRU_C = 8.0
LRU_SUB = 64
OUT_PROJ_PARTS = 2
ATTN_KV_UNROLL = 4
IN_WIDTH = ATTN_WIDTH + 2 * KV_WIDTH + 2 * LRU_WIDTH
FFN_DIM = 5632
FFN_CONV_W = 3
ROPE_THETA = 10000.0
EPS = 1e-6
ATTN_SCALE = HEAD_DIM ** -0.5
LOG2E = 1.4426950408889634
LN2 = 0.6931471805599453
Q_PRESCALE = ATTN_SCALE * LOG2E
MAX_FIXED_SHIFT_RANGE = 120.0
BF16_ROUNDING_SLACK = 1.0 + 2.0 ** -6

LANES = 128
SUBLANES_F32 = 8
SUBLANES_BF16 = 16
VMEM_LIMIT_BYTES = 56 * 1024 * 1024

F32 = jnp.float32
BF16 = jnp.bfloat16


def _rms(x, g):
    ms = jnp.mean(x * x, axis=-1, keepdims=True)
    return x * lax.rsqrt(ms + EPS) * g


def _dot(a, b):
    return jnp.dot(a, b, preferred_element_type=F32)


def _const_spec(shape):
    zeros = (0,) * len(shape)
    return pl.BlockSpec(shape, lambda *_: zeros)


def _in_proj_kernel(x_ref, xprev_ref, xnext_ref, xhead_ref, g_ref, w_ref, qg_ref, kg_ref, cos_ref, sin_ref,
                    cw_ref, cb_ref, q_ref, k_ref, vt_ref, xc_ref, gy_ref, xch_ref, *, nt):
    tm = x_ref.shape[0]
    halo = SUBLANES_BF16
    pos = pl.program_id(0) % nt
    xprev = jnp.where(pos == 0, xhead_ref[...], xprev_ref[...])
    xnext = jnp.where(pos == nt - 1, jnp.zeros_like(xnext_ref[...]), xnext_ref[...])
    xn = _rms(x_ref[...], g_ref[...]).astype(BF16)
    xn_ext = jnp.concatenate([_rms(xprev, g_ref[...]).astype(BF16), xn, _rms(xnext, g_ref[...]).astype(BF16)], axis=0)
    cos = cos_ref[...]
    sin = sin_ref[...]

    def rope_head(zh, gain):
        y = _rms(zh, gain)
        return y * cos + pltpu.roll(y, HEAD_DIM // 2, axis=1) * sin

    o3 = ATTN_WIDTH + 2 * KV_WIDTH
    zq = _dot(xn, w_ref[:, 0:ATTN_WIDTH])
    for h in range(N_Q_HEADS):
        sl = slice(h * HEAD_DIM, (h + 1) * HEAD_DIM)
        q_ref[h] = (rope_head(zq[:, sl], qg_ref[...]) * Q_PRESCALE).astype(BF16)
    xr_ext = _dot(xn_ext, w_ref[:, o3:o3 + LRU_WIDTH])
    n_ext = tm + 2 * halo
    xc_ext = jnp.broadcast_to(cb_ref[...], (n_ext, LRU_WIDTH))
    for kk in range(LRU_CONV_W):
        shift = (LRU_CONV_W // 2 - kk) % n_ext
        xk = xr_ext if shift == 0 else pltpu.roll(xr_ext, shift, axis=0)
        xc_ext = xc_ext + xk * cw_ref[kk:kk + 1, :]
    xc_ref[...] = xc_ext[halo:halo + tm]
    xch_ref[...] = xc_ext[0:halo]
    zkv = _dot(xn, w_ref[:, ATTN_WIDTH:ATTN_WIDTH + 2 * KV_WIDTH])
    for h in range(N_KV_HEADS):
        sl = slice(h * HEAD_DIM, (h + 1) * HEAD_DIM)
        k_ref[h] = rope_head(zkv[:, sl], kg_ref[...]).astype(BF16)
    vt_ref[...] = zkv[:, KV_WIDTH:2 * KV_WIDTH].T.astype(BF16)
    gy_ref[...] = _dot(xn, w_ref[:, o3 + LRU_WIDTH:o3 + 2 * LRU_WIDTH])


def _in_proj(x2d, xhead, nt, gain, w_in, q_gain, k_gain, cos_t, sin_t, conv_w, conv_b, tm):
    rows = x2d.shape[0]
    n_tab = cos_t.shape[0] // tm
    halo = SUBLANES_BF16
    th = tm // halo
    nh = rows // halo
    row_spec = lambda w: pl.BlockSpec((tm, w), lambda i: (i, 0))
    tab_spec = pl.BlockSpec((tm, HEAD_DIM), lambda i: (i % n_tab, 0))
    return pl.pallas_call(
        functools.partial(_in_proj_kernel, nt=nt),
        grid=(rows // tm,),
        in_specs=[row_spec(D_MODEL),
                  pl.BlockSpec((halo, D_MODEL), lambda i: (jnp.maximum(i * th - 1, 0), 0)),
                  pl.BlockSpec((halo, D_MODEL), lambda i: (jnp.minimum((i + 1) * th, nh - 1), 0)),
                  _const_spec((halo, D_MODEL)), _const_spec((1, D_MODEL)),
                  pl.BlockSpec((D_MODEL, IN_WIDTH), lambda i: (0, 0), pipeline_mode=pl.Buffered(1)),
                  _const_spec((1, HEAD_DIM)), _const_spec((1, HEAD_DIM)), tab_spec, tab_spec,
                  _const_spec((LRU_CONV_W, LRU_WIDTH)), _const_spec((1, LRU_WIDTH))],
        out_specs=[pl.BlockSpec((N_Q_HEADS, tm, HEAD_DIM), lambda i: (0, i, 0)),
                   pl.BlockSpec((N_KV_HEADS, tm, HEAD_DIM), lambda i: (0, i, 0)),
                   pl.BlockSpec((KV_WIDTH, tm), lambda i: (0, i)),
                   row_spec(LRU_WIDTH), row_spec(LRU_WIDTH),
                   pl.BlockSpec((halo, LRU_WIDTH), lambda i: (i, 0))],
        out_shape=[jax.ShapeDtypeStruct((N_Q_HEADS, rows, HEAD_DIM), BF16),
                   jax.ShapeDtypeStruct((N_KV_HEADS, rows, HEAD_DIM), BF16),
                   jax.ShapeDtypeStruct((KV_WIDTH, rows), BF16),
                   jax.ShapeDtypeStruct((rows, LRU_WIDTH), F32),
                   jax.ShapeDtypeStruct((rows, LRU_WIDTH), F32),
                   jax.ShapeDtypeStruct((rows // tm * halo, LRU_WIDTH), F32)],
        compiler_params=pltpu.CompilerParams(dimension_semantics=("parallel",),
                                             vmem_limit_bytes=VMEM_LIMIT_BYTES),
        name="in_proj",
    )(x2d, x2d, x2d, xhead, gain, w_in, q_gain, k_gain, cos_t, sin_t, conv_w, conv_b)


def _attn_kernel(bound_ref, q_ref, k_ref, vt_ref, km_ref, vmt_ref, o_ref, l_ref, acc_ref, *m_scratch,
                 nkv, tk, running_max):
    tq = q_ref.shape[1]
    width = Q_PER_KV * tq
    qs = q_ref[...].reshape(width, HEAD_DIM)
    nt = (((1,), (1,)), ((), ()))

    def block(kb, vtb, first):
        st = lax.dot_general(kb, qs, nt, preferred_element_type=F32)
        if running_max:
            m_ref, = m_scratch
            m_new = jnp.max(st, axis=0, keepdims=True)
            if not first:
                m_prev = m_ref[...]
                m_new = jnp.maximum(m_prev, m_new)
                alpha = jnp.exp2(m_prev - m_new)
            m_ref[...] = m_new
            p = jnp.exp2(st - m_new)
        else:
            p = jnp.exp2(st - bound_ref[0])
        psum = jnp.sum(p.reshape(p.shape[0] // SUBLANES_F32, SUBLANES_F32, width), axis=0)
        pv = _dot(vtb, p.astype(BF16))
        if first:
            l_ref[...] = psum
            acc_ref[...] = pv
        elif running_max:
            l_ref[...] = alpha * l_ref[...] + psum
            acc_ref[...] = alpha * acc_ref[...] + pv
        else:
            l_ref[...] += psum
            acc_ref[...] += pv

    block(km_ref[0], vmt_ref[...], True)

    def body(j, carry):
        start = pl.multiple_of(j * tk, tk)
        block(k_ref[0, pl.ds(start, tk), :], vt_ref[:, pl.ds(start, tk)], False)
        return carry

    unroll = next(u for u in (ATTN_KV_UNROLL, 2, 1) if nkv % u == 0)
    lax.fori_loop(0, nkv, body, 0, unroll=unroll)

    o = acc_ref[...] * (1.0 / jnp.sum(l_ref[...], axis=0, keepdims=True))
    for g in range(Q_PER_KV):
        o_ref[0, :, g * HEAD_DIM:(g + 1) * HEAD_DIM] = o[:, g * tq:(g + 1) * tq].T.astype(BF16)


def _attention(bound, q, k, vt, km, vmt, batch, tq, tk, shared_q, running_max):
    n = k.shape[1] // batch
    nq = q.shape[1] if shared_q else q.shape[1] // batch
    nkv = n // tk
    nqt = nq // tq
    width = Q_PER_KV * tq
    if shared_q:
        q_map = lambda bi, h, i: (h, i, 0)
    else:
        q_map = lambda bi, h, i: (h, bi * nqt + i, 0)
    scratch = [pltpu.VMEM((SUBLANES_F32, width), F32), pltpu.VMEM((HEAD_DIM, width), F32)]
    if running_max:
        scratch.append(pltpu.VMEM((1, width), F32))
    return pl.pallas_call(
        functools.partial(_attn_kernel, nkv=nkv, tk=tk, running_max=running_max),
        grid=(batch, N_KV_HEADS, nqt),
        in_specs=[pl.BlockSpec(memory_space=pltpu.SMEM),
                  pl.BlockSpec((Q_PER_KV, tq, HEAD_DIM), q_map),
                  pl.BlockSpec((1, n, HEAD_DIM), lambda bi, h, i: (h, bi, 0)),
                  pl.BlockSpec((HEAD_DIM, n), lambda bi, h, i: (h, bi)),
                  pl.BlockSpec((1, N_META, HEAD_DIM), lambda bi, h, i: (h, 0, 0)),
                  pl.BlockSpec((HEAD_DIM, N_META), lambda bi, h, i: (h, 0))],
        out_specs=pl.BlockSpec((1, tq, Q_PER_KV * HEAD_DIM), lambda bi, h, i: (bi, i, h)),
        out_shape=jax.ShapeDtypeStruct((batch, nq, ATTN_WIDTH), BF16),
        scratch_shapes=scratch,
        compiler_params=pltpu.CompilerParams(
            dimension_semantics=("parallel", "parallel", "arbitrary"),
            vmem_limit_bytes=VMEM_LIMIT_BYTES),
        name="attention",
    )(bound, q, k, vt, km, vmt)


def _softplus(x):
    return jnp.maximum(x, 0.0) + jnp.log1p(jnp.exp(-jnp.abs(x)))


def _lru_prep(xc, rows, wr_ref, br_ref, wi_ref, bi_ref, lam_ref, a_ref, u_ref):
    xcb = xc.astype(BF16)
    r_parts, i_parts = [], []
    for blk in range(LRU_BLOCKS):
        xb = xcb[:, blk * LRU_BLOCK:(blk + 1) * LRU_BLOCK]
        r_parts.append(_dot(xb, wr_ref[blk]))
        i_parts.append(_dot(xb, wi_ref[blk]))
    tr = jnp.tanh(jnp.concatenate(r_parts, axis=-1) + br_ref[...])
    ti = jnp.tanh(jnp.concatenate(i_parts, axis=-1) + bi_ref[...])
    c1 = (-0.5 * LRU_C * LOG2E) * _softplus(-lam_ref[...])
    log2_a = c1 * tr + c1
    a = jnp.exp2(log2_a)
    a_ref[0:rows, :] = a
    gate2 = jnp.tanh((-LN2) * log2_a) * (1.0 + a * a)
    gate = jnp.where(gate2 > 0.0, gate2 * lax.rsqrt(gate2), 0.0)
    half_xc = 0.5 * xc
    u_ref[0:rows, :] = gate * (half_xc * ti + half_xc)


def _lru_scan(rows, a_ref, u_ref, h, out_ref, base, reverse):
    for r in (range(rows - 1, -1, -1) if reverse else range(rows)):
        h = a_ref[r:r + 1, :] * h + u_ref[r:r + 1, :]
        out_ref[pl.ds(base + r, 1), :] = h
    return h


def _lru_fwd_kernel(xc_ref, xcm_ref, wr_ref, br_ref, wi_ref, bi_ref,
                    lam_ref, hf_ref, hfm_ref, a0_ref, u0_ref, a1_ref, u1_ref, h_ref):
    j = pl.program_id(1)
    tc = xc_ref.shape[1]
    wts = (wr_ref, br_ref, wi_ref, bi_ref, lam_ref)
    nsub = tc // LRU_SUB

    @pl.when(j == 0)
    def _():
        _lru_prep(xcm_ref[0], N_META, *wts, a0_ref, u0_ref)
        h_ref[...] = _lru_scan(N_META, a0_ref, u0_ref, jnp.zeros(h_ref.shape, F32), hfm_ref.at[0], 0, False)

    @pl.when(j > 0)
    def _():
        def prep(s, a_ref, u_ref):
            start = pl.multiple_of(s * LRU_SUB, LRU_SUB)
            _lru_prep(xc_ref[0, pl.ds(start, LRU_SUB), :], LRU_SUB, *wts, a_ref, u_ref)

        prep(0, a0_ref, u0_ref)

        def pair(i, h):
            s = 2 * i
            prep(s + 1, a1_ref, u1_ref)
            h = _lru_scan(LRU_SUB, a0_ref, u0_ref, h, hf_ref.at[0], pl.multiple_of(s * LRU_SUB, LRU_SUB), False)
            prep(jnp.minimum(s + 2, nsub - 1), a0_ref, u0_ref)
            return _lru_scan(LRU_SUB, a1_ref, u1_ref, h, hf_ref.at[0],
                             pl.multiple_of((s + 1) * LRU_SUB, LRU_SUB), False)

        h_ref[...] = lax.fori_loop(0, nsub // 2, pair, h_ref[...], unroll=2 if nsub % 4 == 0 else 1)


def _lru_bwd_kernel(xc_ref, xcm_ref, gy_ref, gym_ref, hf_ref, hfm_ref,
                    wr_ref, br_ref, wi_ref, bi_ref, lam_ref, og_ref,
                    out_ref, outm_ref, a0_ref, u0_ref, a1_ref, u1_ref, h_ref, hb0_ref, hb1_ref, *, nc):
    j = pl.program_id(1)
    tc = xc_ref.shape[1]
    wts = (wr_ref, br_ref, wi_ref, bi_ref, lam_ref)
    nsub = tc // LRU_SUB

    def finish(hf, hb, gy):
        return _rms((hf + hb) * jax.nn.gelu(gy), og_ref[...]).astype(BF16)

    @pl.when(j == 0)
    def _():
        h_ref[...] = jnp.zeros(h_ref.shape, F32)

    @pl.when(j < nc)
    def _():
        def prep(s, a_ref, u_ref):
            start = pl.multiple_of(s * LRU_SUB, LRU_SUB)
            _lru_prep(xc_ref[0, pl.ds(start, LRU_SUB), :], LRU_SUB, *wts, a_ref, u_ref)

        def piece(s, a_ref, u_ref, hb_ref, h):
            h = _lru_scan(LRU_SUB, a_ref, u_ref, h, hb_ref, 0, True)
            rows = pl.ds(pl.multiple_of(s * LRU_SUB, LRU_SUB), LRU_SUB)
            out_ref[0, rows, :] = finish(hf_ref[0, rows, :], hb_ref[...], gy_ref[0, rows, :])
            return h

        prep(nsub - 1, a0_ref, u0_ref)

        def pair(i, h):
            s = nsub - 1 - 2 * i
            prep(s - 1, a1_ref, u1_ref)
            h = piece(s, a0_ref, u0_ref, hb0_ref, h)
            prep(jnp.maximum(s - 2, 0), a0_ref, u0_ref)
            return piece(s - 1, a1_ref, u1_ref, hb1_ref, h)

        h_ref[...] = lax.fori_loop(0, nsub // 2, pair, h_ref[...], unroll=2 if nsub % 4 == 0 else 1)

    @pl.when(j == nc)
    def _():
        _lru_prep(xcm_ref[0], N_META, *wts, a0_ref, u0_ref)
        h_ref[...] = _lru_scan(N_META, a0_ref, u0_ref, h_ref[...], hb0_ref, 0, True)
        outm_ref[0] = finish(hfm_ref[0], hb0_ref[0:N_META, :], gym_ref[...])


def _lru(xc, gy, xcm, gym, w_r, b_r, w_i, b_i, lam, out_gain, tc):
    b, n, c = xc.shape
    nc = n // tc
    wspec = _const_spec((LRU_BLOCKS, LRU_BLOCK, LRU_BLOCK))
    vspec = _const_spec((1, c))
    pspecs = [wspec, vspec, wspec, vspec, vspec]
    bmspec = pl.BlockSpec((1, N_META, c), lambda bi, j: (bi, 0, 0))
    slots = [pltpu.VMEM((LRU_SUB, c), F32)] * 4
    assert tc % (2 * LRU_SUB) == 0 and LRU_SUB >= N_META

    def chunk_f(j):
        return jnp.maximum(j - 1, 0)

    hf, hfm = pl.pallas_call(
        _lru_fwd_kernel,
        grid=(b, nc + 1),
        in_specs=[pl.BlockSpec((1, tc, c), lambda bi, j: (bi, chunk_f(j), 0)), bmspec] + pspecs,
        out_specs=[pl.BlockSpec((1, tc, c), lambda bi, j: (bi, chunk_f(j), 0)), bmspec],
        out_shape=[jax.ShapeDtypeStruct((b, n, c), F32), jax.ShapeDtypeStruct((b, N_META, c), F32)],
        scratch_shapes=slots + [pltpu.VMEM((1, c), F32)],
        compiler_params=pltpu.CompilerParams(dimension_semantics=("parallel", "arbitrary"),
                                             vmem_limit_bytes=VMEM_LIMIT_BYTES),
        name="lru_fwd",
    )(xc, xcm, w_r[0], b_r[0:1], w_i[0], b_i[0:1], lam[0:1])

    def chunk_b(j):
        return jnp.maximum(nc - 1 - j, 0)

    chunk_spec = pl.BlockSpec((1, tc, c), lambda bi, j: (bi, chunk_b(j), 0))
    out, outm = pl.pallas_call(
        functools.partial(_lru_bwd_kernel, nc=nc),
        grid=(b, nc + 1),
        in_specs=[chunk_spec, bmspec, chunk_spec, _const_spec((N_META, c)), chunk_spec, bmspec] + pspecs + [vspec],
        out_specs=[chunk_spec, bmspec],
        out_shape=[jax.ShapeDtypeStruct((b, n, c), BF16), jax.ShapeDtypeStruct((b, N_META, c), BF16)],
        scratch_shapes=slots + [pltpu.VMEM((1, c), F32),
                                pltpu.VMEM((LRU_SUB, c), F32), pltpu.VMEM((LRU_SUB, c), F32)],
        compiler_params=pltpu.CompilerParams(dimension_semantics=("parallel", "arbitrary"),
                                             vmem_limit_bytes=VMEM_LIMIT_BYTES),
        name="lru_bwd",
    )(xc, xcm, gy, gym, hf, hfm, w_r[1], b_r[1:2], w_i[1], b_i[1:2], lam[1:2], out_gain)
    return out, outm


def _out_proj_kernel(attn_ref, lru_ref, res_ref, ag_ref, wa_ref, wl_ref, pg_ref, fg_ref, h1_ref, hn_ref):
    tm = attn_ref.shape[0]
    parts = OUT_PROJ_PARTS if tm % (OUT_PROJ_PARTS * LANES) == 0 else 1
    part = tm // parts
    for s in range(parts):
        rows = slice(s * part, (s + 1) * part)
        an = _rms(attn_ref[rows, :].astype(F32), ag_ref[...]).astype(BF16)
        mixed = _dot(an, wa_ref[...]) + _dot(lru_ref[rows, :], wl_ref[...])
        h1 = res_ref[rows, :] + _rms(mixed, pg_ref[...])
        h1_ref[rows, :] = h1
        hn_ref[rows, :] = _rms(h1, fg_ref[...]).astype(BF16)


def _out_proj(attn2d, lru2d, res2d, attn_gain, w_a, w_l, post_gain, ffn_gain, tm):
    rows = attn2d.shape[0]
    row_spec = lambda w: pl.BlockSpec((tm, w), lambda i: (i, 0))
    wspec = pl.BlockSpec((ATTN_WIDTH, D_MODEL), lambda i: (0, 0), pipeline_mode=pl.Buffered(1))
    return pl.pallas_call(
        _out_proj_kernel,
        grid=(rows // tm,),
        in_specs=[row_spec(ATTN_WIDTH), row_spec(LRU_WIDTH), row_spec(D_MODEL), _const_spec((1, ATTN_WIDTH)),
                  wspec, wspec, _const_spec((1, D_MODEL)), _const_spec((1, D_MODEL))],
        out_specs=[row_spec(D_MODEL), row_spec(D_MODEL)],
        out_shape=[jax.ShapeDtypeStruct((rows, D_MODEL), F32), jax.ShapeDtypeStruct((rows, D_MODEL), BF16)],
        compiler_params=pltpu.CompilerParams(dimension_semantics=("parallel",),
                                             vmem_limit_bytes=VMEM_LIMIT_BYTES),
        name="out_proj",
    )(attn2d, lru2d, res2d, attn_gain, w_a, w_l, post_gain, ffn_gain)


def _ffn_kernel(hn_ref, prev_ref, next_ref, hm_ref, h1_ref, wg_ref, wv_ref, wd_ref, cw_ref, cb_ref, og_ref,
                out_ref, ext_ref, *, nt, nf):
    i = pl.program_id(1)
    f = pl.program_id(2)
    tm = hn_ref.shape[1]
    halo = SUBLANES_BF16
    acc_ref = out_ref.at[0]

    @pl.when(f == 0)
    def _():
        ext_ref[0:halo, :] = jnp.where(i == 0, hm_ref[0], prev_ref[0])
        ext_ref[halo:halo + tm, :] = hn_ref[0]
        ext_ref[halo + tm:2 * halo + tm, :] = jnp.where(i == nt - 1, jnp.zeros_like(next_ref[0]), next_ref[0])
        acc_ref[...] = jnp.zeros(acc_ref.shape, F32)

    n_ext = tm + 2 * halo
    gfull = _dot(ext_ref[...], wg_ref[...])
    g = jnp.broadcast_to(cb_ref[...], (tm, gfull.shape[1]))
    for kk in range(FFN_CONV_W):
        shift = (FFN_CONV_W // 2 - kk) % n_ext
        gk = gfull if shift == 0 else pltpu.roll(gfull, shift, axis=0)
        g = g + gk[halo:halo + tm] * cw_ref[kk:kk + 1, :]
    val = _dot(ext_ref[halo:halo + tm, :], wv_ref[...])
    act = (jax.nn.silu(g) * val).astype(BF16)
    acc_ref[...] += _dot(act, wd_ref[...])

    @pl.when(f == nf - 1)
    def _():
        acc_ref[...] = h1_ref[0] + _rms(acc_ref[...], og_ref[...])


def _ffn(hn, hnm, h1, w_up, w_down, conv_w, conv_b, out_gain, tm, fc):
    b, n, d = hn.shape
    nt = n // tm
    nf = FFN_DIM // fc
    halo = SUBLANES_BF16
    th = tm // halo
    nh = n // halo
    return pl.pallas_call(
        functools.partial(_ffn_kernel, nt=nt, nf=nf),
        grid=(b, nt, nf),
        in_specs=[pl.BlockSpec((1, tm, d), lambda bi, i, f: (bi, i, 0)),
                  pl.BlockSpec((1, halo, d), lambda bi, i, f: (bi, jnp.maximum(i * th - 1, 0), 0)),
                  pl.BlockSpec((1, halo, d), lambda bi, i, f: (bi, jnp.minimum((i + 1) * th, nh - 1), 0)),
                  pl.BlockSpec((1, N_META, d), lambda bi, i, f: (bi, 0, 0)),
                  pl.BlockSpec((1, tm, d), lambda bi, i, f: (bi, i, 0)),
                  pl.BlockSpec((d, fc), lambda bi, i, f: (0, f)),
                  pl.BlockSpec((d, fc), lambda bi, i, f: (0, f + nf)),
                  pl.BlockSpec((fc, d), lambda bi, i, f: (f, 0)),
                  pl.BlockSpec((FFN_CONV_W, fc), lambda bi, i, f: (0, f)),
                  pl.BlockSpec((1, fc), lambda bi, i, f: (0, f)),
                  _const_spec((1, d))],
        out_specs=pl.BlockSpec((1, tm, d), lambda bi, i, f: (bi, i, 0)),
        out_shape=jax.ShapeDtypeStruct((b, n, d), F32),
        scratch_shapes=[pltpu.VMEM((tm + 2 * halo, d), BF16)],
        compiler_params=pltpu.CompilerParams(
            dimension_semantics=("parallel", "parallel", "arbitrary"),
            vmem_limit_bytes=VMEM_LIMIT_BYTES),
        name="conv_ffn",
    )(hn, hn, hn, hnm, h1, w_up, w_up, w_down, conv_w, conv_b, out_gain)


def _rope_tables(n):
    rows = n // GRID_W
    t_row = jnp.repeat(jnp.arange(rows), GRID_W).astype(F32)
    t_col = jnp.tile(jnp.arange(GRID_W), rows).astype(F32)
    half = HEAD_DIM // 2
    inv = ROPE_THETA ** (-jnp.arange(0, half, 2, dtype=F32) / half)
    ang = jnp.concatenate([t_row[:, None] * inv, t_col[:, None] * inv], axis=-1)
    cos, sin = jnp.cos(ang), jnp.sin(ang)
    return jnp.concatenate([cos, cos], axis=-1), jnp.concatenate([-sin, sin], axis=-1)


_TILES = dict(tm=512, tq=512, tk=1024, tc=1024, tf=512, fc=512)


def _pick(n, pref):
    t = pref
    while n % t:
        t //= 2
    return t


def _trunk(x, meta, mp, p):
    b, n, d = x.shape
    rows = b * n
    cos_t, sin_t = _rope_tables(n)
    tm = _pick(n, _TILES["tm"])
    nt = n // tm
    q, k, vt, xc, gy, xc_head = _in_proj(x.reshape(rows, d), meta, nt, p["norm_pre_mix"], p["w_in"], p["q_norm"],
                                         p["k_norm"], cos_t, sin_t, p["lru_conv_w"], p["lru_conv_b"], tm)
    head = xc_head.reshape(b, nt, SUBLANES_BF16, LRU_WIDTH)[:, 0, SUBLANES_BF16 - 2:, :]
    xc_m = jnp.concatenate([jnp.broadcast_to(mp["xc"][None, :N_META - 2], (b, N_META - 2, LRU_WIDTH)), head], axis=1)
    tq = _pick(n, _TILES["tq"])
    tk = _pick(n, _TILES["tk"])
    bound = mp["bound"]

    def attend(running_max):
        def run(bound, q, qm, k, vt, km, vmt):
            real = _attention(bound, q, k, vt, km, vmt, b, tq, tk, False, running_max)
            meta_rows = _attention(bound, qm, k, vt, km, vmt, b, LANES, tk, True, running_max)
            return real, meta_rows
        return run

    attn, attn_m = lax.cond(2.0 * bound[0] <= MAX_FIXED_SHIFT_RANGE, attend(False), attend(True),
                            bound, q, mp["q"], k, vt, mp["km"], mp["vmt"])
    attn_m = attn_m[:, :N_META]
    lru, lru_m = _lru(xc.reshape(b, n, LRU_WIDTH), gy.reshape(b, n, LRU_WIDTH), xc_m, mp["gy"],
                      p["lru_w_r"], p["lru_b_r"], p["lru_w_i"], p["lru_b_i"],
                      p["lru_lambda"], p["lru_out_norm"], _pick(n, _TILES["tc"]))
    op = (p["attn_out_norm"], p["w_out_a"], p["w_out_l"], p["norm_post_mix"], p["norm_pre_ffn"])
    h1, hn = _out_proj(attn.reshape(rows, ATTN_WIDTH), lru.reshape(rows, LRU_WIDTH), x.reshape(rows, d), *op, tm)
    res_m = jnp.broadcast_to(meta[None], (b, N_META, d)).reshape(b * N_META, d)
    _, hn_m = _out_proj(attn_m.reshape(b * N_META, ATTN_WIDTH), lru_m.reshape(b * N_META, LRU_WIDTH), res_m,
                        *op, N_META)
    return _ffn(hn.reshape(b, n, d), hn_m.reshape(b, N_META, d), h1.reshape(b, n, d), p["w_up"], p["w_down"],
                p["ffn_conv_w"], p["ffn_conv_b"], p["norm_post_ffn"], _pick(n, _TILES["tf"]), _TILES["fc"])


def _head_perm():
    q4 = HEAD_DIM // 4
    idx = jnp.arange(HEAD_DIM).reshape(2, 2, q4)
    return idx.transpose(1, 0, 2).reshape(HEAD_DIM)


def kernel(x_prompt, x_sample, meta_tokens, norm_pre_mix, w_in, q_norm, k_norm, lru_conv_w, lru_conv_b,
           lru_w_r, lru_b_r, lru_w_i, lru_b_i, lru_lambda, attn_out_norm, lru_out_norm, w_out,
           norm_post_mix, norm_pre_ffn, w_up, ffn_conv_w, ffn_conv_b, w_down, norm_post_ffn):
    perm = _head_perm()
    n_rot = N_Q_HEADS + N_KV_HEADS
    cols = (jnp.arange(n_rot)[:, None] * HEAD_DIM + perm[None, :]).reshape(-1)
    cols = jnp.concatenate([cols, jnp.arange(n_rot * HEAD_DIM, IN_WIDTH)])
    w_out_b = w_out[0].astype(BF16)
    p = {
        "norm_pre_mix": norm_pre_mix[0][None], "w_in": w_in[0][:, cols].astype(BF16),
        "q_norm": q_norm[0][perm][None], "k_norm": k_norm[0][perm][None],
        "lru_conv_w": lru_conv_w[0], "lru_conv_b": lru_conv_b[0][None],
        "lru_w_r": (0.5 * lru_w_r[0]).astype(BF16), "lru_b_r": 0.5 * lru_b_r[0],
        "lru_w_i": (0.5 * lru_w_i[0]).astype(BF16), "lru_b_i": 0.5 * lru_b_i[0], "lru_lambda": lru_lambda[0],
        "attn_out_norm": attn_out_norm[0][None], "lru_out_norm": lru_out_norm[0][None],
        "w_out_a": w_out_b[:ATTN_WIDTH], "w_out_l": w_out_b[ATTN_WIDTH:],
        "norm_post_mix": norm_post_mix[0][None], "norm_pre_ffn": norm_pre_ffn[0][None],
        "w_up": w_up[0].astype(BF16), "ffn_conv_w": ffn_conv_w[0], "ffn_conv_b": ffn_conv_b[0][None],
        "w_down": w_down[0].astype(BF16), "norm_post_ffn": norm_post_ffn[0][None],
    }
    ones = jnp.ones((LANES, HEAD_DIM), F32)
    meta_pad = jnp.pad(meta_tokens, ((0, LANES - N_META), (0, 0)))
    qm, km, vtm, xcm, gym, _ = _in_proj(meta_pad, jnp.zeros((SUBLANES_BF16, D_MODEL), F32), 1, p["norm_pre_mix"],
                                        p["w_in"], p["q_norm"], p["k_norm"], ones, jnp.zeros_like(ones),
                                        p["lru_conv_w"], p["lru_conv_b"], LANES)
    bound = (HEAD_DIM * Q_PRESCALE * BF16_ROUNDING_SLACK) * jnp.max(jnp.abs(q_norm[0])) * jnp.max(jnp.abs(k_norm[0]))
    mp = {"q": qm, "km": km[:, :N_META], "vmt": vtm[:, :N_META], "xc": xcm[:N_META], "gy": gym[:N_META],
          "bound": bound.reshape(1).astype(F32)}
    return _trunk(x_prompt, meta_tokens, mp, p), _trunk(x_sample, meta_tokens, mp, p)
```

```python
import functools

import jax
import jax.numpy as jnp
from jax import lax
from jax.experimental import pallas as pl
from jax.experimental.pallas import tpu as pltpu

D_MODEL = 2048
N_META = 16
GRID_W = 64
HEAD_DIM = 128
N_Q_HEADS = 8
N_KV_HEADS = 2
Q_PER_KV = N_Q_HEADS // N_KV_HEADS
ATTN_WIDTH = N_Q_HEADS * HEAD_DIM
KV_WIDTH = N_KV_HEADS * HEAD_DIM
LRU_WIDTH = D_MODEL - ATTN_WIDTH
LRU_BLOCKS = 8
LRU_BLOCK = LRU_WIDTH // LRU_BLOCKS
LRU_CONV_W = 4
LRU_C = 8.0
LRU_SUB = 64
OUT_PROJ_PARTS = 2
ATTN_KV_UNROLL = 8
IN_WIDTH = ATTN_WIDTH + 2 * KV_WIDTH + 2 * LRU_WIDTH
FFN_DIM = 5632
FFN_CONV_W = 3
ROPE_THETA = 10000.0
EPS = 1e-6
ATTN_SCALE = HEAD_DIM ** -0.5
LOG2E = 1.4426950408889634
LN2 = 0.6931471805599453
Q_PRESCALE = ATTN_SCALE * LOG2E
MAX_FIXED_SHIFT_RANGE = 120.0
BF16_ROUNDING_SLACK = 1.0 + 2.0 ** -6

LANES = 128
SUBLANES_F32 = 8
SUBLANES_BF16 = 16
VMEM_LIMIT_BYTES = 56 * 1024 * 1024

F32 = jnp.float32
BF16 = jnp.bfloat16


def _rms(x, g):
    ms = jnp.mean(x * x, axis=-1, keepdims=True)
    return x * lax.rsqrt(ms + EPS) * g


def _dot(a, b):
    return jnp.dot(a, b, preferred_element_type=F32)


def _const_spec(shape):
    zeros = (0,) * len(shape)
    return pl.BlockSpec(shape, lambda *_: zeros)


def _in_proj_kernel(x_ref, xprev_ref, xnext_ref, xhead_ref, g_ref, w_ref, qg_ref, kg_ref, cos_ref, sin_ref,
                    cw_ref, cb_ref, q_ref, k_ref, vt_ref, xc_ref, gy_ref, xch_ref, *, nt):
    tm = x_ref.shape[0]
    halo = SUBLANES_BF16
    pos = pl.program_id(0) % nt
    xprev = jnp.where(pos == 0, xhead_ref[...], xprev_ref[...])
    xnext = jnp.where(pos == nt - 1, jnp.zeros_like(xnext_ref[...]), xnext_ref[...])
    xn = _rms(x_ref[...], g_ref[...]).astype(BF16)
    xn_ext = jnp.concatenate([_rms(xprev, g_ref[...]).astype(BF16), xn, _rms(xnext, g_ref[...]).astype(BF16)], axis=0)
    cos = cos_ref[...]
    sin = sin_ref[...]

    def rope_head(zh, gain):
        y = _rms(zh, gain)
        return y * cos + pltpu.roll(y, HEAD_DIM // 2, axis=1) * sin

    o3 = ATTN_WIDTH + 2 * KV_WIDTH
    zq = _dot(xn, w_ref[:, 0:ATTN_WIDTH])
    for h in range(N_Q_HEADS):
        sl = slice(h * HEAD_DIM, (h + 1) * HEAD_DIM)
        q_ref[h] = (rope_head(zq[:, sl], qg_ref[...]) * Q_PRESCALE).astype(BF16)
    xr_ext = _dot(xn_ext, w_ref[:, o3:o3 + LRU_WIDTH])
    n_ext = tm + 2 * halo
    xc_ext = jnp.broadcast_to(cb_ref[...], (n_ext, LRU_WIDTH))
    for kk in range(LRU_CONV_W):
        shift = (LRU_CONV_W // 2 - kk) % n_ext
        xk = xr_ext if shift == 0 else pltpu.roll(xr_ext, shift, axis=0)
        xc_ext = xc_ext + xk * cw_ref[kk:kk + 1, :]
    xc_ref[...] = xc_ext[halo:halo + tm]
    xch_ref[...] = xc_ext[0:halo]
    zkv = _dot(xn, w_ref[:, ATTN_WIDTH:ATTN_WIDTH + 2 * KV_WIDTH])
    for h in range(N_KV_HEADS):
        sl = slice(h * HEAD_DIM, (h + 1) * HEAD_DIM)
        k_ref[h] = rope_head(zkv[:, sl], kg_ref[...]).astype(BF16)
    vt_ref[...] = zkv[:, KV_WIDTH:2 * KV_WIDTH].T.astype(BF16)
    gy_ref[...] = _dot(xn, w_ref[:, o3 + LRU_WIDTH:o3 + 2 * LRU_WIDTH])


def _in_proj(x2d, xhead, nt, gain, w_in, q_gain, k_gain, cos_t, sin_t, conv_w, conv_b, tm):
    rows = x2d.shape[0]
    n_tab = cos_t.shape[0] // tm
    halo = SUBLANES_BF16
    th = tm // halo
    nh = rows // halo
    row_spec = lambda w: pl.BlockSpec((tm, w), lambda i: (i, 0))
    tab_spec = pl.BlockSpec((tm, HEAD_DIM), lambda i: (i % n_tab, 0))
    return pl.pallas_call(
        functools.partial(_in_proj_kernel, nt=nt),
        grid=(rows // tm,),
        in_specs=[row_spec(D_MODEL),
                  pl.BlockSpec((halo, D_MODEL), lambda i: (jnp.maximum(i * th - 1, 0), 0)),
                  pl.BlockSpec((halo, D_MODEL), lambda i: (jnp.minimum((i + 1) * th, nh - 1), 0)),
                  _const_spec((halo, D_MODEL)), _const_spec((1, D_MODEL)),
                  pl.BlockSpec((D_MODEL, IN_WIDTH), lambda i: (0, 0), pipeline_mode=pl.Buffered(1)),
                  _const_spec((1, HEAD_DIM)), _const_spec((1, HEAD_DIM)), tab_spec, tab_spec,
                  _const_spec((LRU_CONV_W, LRU_WIDTH)), _const_spec((1, LRU_WIDTH))],
        out_specs=[pl.BlockSpec((N_Q_HEADS, tm, HEAD_DIM), lambda i: (0, i, 0)),
                   pl.BlockSpec((N_KV_HEADS, tm, HEAD_DIM), lambda i: (0, i, 0)),
                   pl.BlockSpec((KV_WIDTH, tm), lambda i: (0, i)),
                   row_spec(LRU_WIDTH), row_spec(LRU_WIDTH),
                   pl.BlockSpec((halo, LRU_WIDTH), lambda i: (i, 0))],
        out_shape=[jax.ShapeDtypeStruct((N_Q_HEADS, rows, HEAD_DIM), BF16),
                   jax.ShapeDtypeStruct((N_KV_HEADS, rows, HEAD_DIM), BF16),
                   jax.ShapeDtypeStruct((KV_WIDTH, rows), BF16),
                   jax.ShapeDtypeStruct((rows, LRU_WIDTH), F32),
                   jax.ShapeDtypeStruct((rows, LRU_WIDTH), F32),
                   jax.ShapeDtypeStruct((rows // tm * halo, LRU_WIDTH), F32)],
        compiler_params=pltpu.CompilerParams(dimension_semantics=("parallel",),
                                             vmem_limit_bytes=VMEM_LIMIT_BYTES),
        name="in_proj",
    )(x2d, x2d, x2d, xhead, gain, w_in, q_gain, k_gain, cos_t, sin_t, conv_w, conv_b)


def _attn_kernel(bound_ref, q_ref, k_ref, vt_ref, km_ref, vmt_ref, o_ref, l_ref, acc_ref, *m_scratch,
                 nkv, tk, running_max):
    tq = q_ref.shape[1]
    width = Q_PER_KV * tq
    qs = q_ref[...].reshape(width, HEAD_DIM)
    nt = (((1,), (1,)), ((), ()))

    def block(kb, vtb, first):
        st = lax.dot_general(kb, qs, nt, preferred_element_type=F32)
        if running_max:
            m_ref, = m_scratch
            m_new = jnp.max(st, axis=0, keepdims=True)
            if not first:
                m_prev = m_ref[...]
                m_new = jnp.maximum(m_prev, m_new)
                alpha = jnp.exp2(m_prev - m_new)
            m_ref[...] = m_new
            p = jnp.exp2(st - m_new)
        else:
            p = jnp.exp2(st - bound_ref[0])
        psum = jnp.sum(p.reshape(p.shape[0] // SUBLANES_F32, SUBLANES_F32, width), axis=0)
        pv = _dot(vtb, p.astype(BF16))
        if first:
            l_ref[...] = psum
            acc_ref[...] = pv
        elif running_max:
            l_ref[...] = alpha * l_ref[...] + psum
            acc_ref[...] = alpha * acc_ref[...] + pv
        else:
            l_ref[...] += psum
            acc_ref[...] += pv

    block(km_ref[0], vmt_ref[...], True)

    def body(j, carry):
        start = pl.multiple_of(j * tk, tk)
        block(k_ref[0, pl.ds(start, tk), :], vt_ref[:, pl.ds(start, tk)], False)
        return carry

    unroll = 1 if running_max else next(u for u in (ATTN_KV_UNROLL, 4, 2, 1) if nkv % u == 0)
    lax.fori_loop(0, nkv, body, 0, unroll=unroll)

    o = acc_ref[...] * (1.0 / jnp.sum(l_ref[...], axis=0, keepdims=True))
    for g in range(Q_PER_KV):
        o_ref[0, :, g * HEAD_DIM:(g + 1) * HEAD_DIM] = o[:, g * tq:(g + 1) * tq].T.astype(BF16)


def _attention(bound, q, k, vt, km, vmt, batch, tq, tk, shared_q, running_max):
    n = k.shape[1] // batch
    nq = q.shape[1] if shared_q else q.shape[1] // batch
    nkv = n // tk
    nqt = nq // tq
    width = Q_PER_KV * tq
    if shared_q:
        q_map = lambda bi, h, i: (h, i, 0)
    else:
        q_map = lambda bi, h, i: (h, bi * nqt + i, 0)
    scratch = [pltpu.VMEM((SUBLANES_F32, width), F32), pltpu.VMEM((HEAD_DIM, width), F32)]
    if running_max:
        scratch.append(pltpu.VMEM((1, width), F32))
    return pl.pallas_call(
        functools.partial(_attn_kernel, nkv=nkv, tk=tk, running_max=running_max),
        grid=(batch, N_KV_HEADS, nqt),
        in_specs=[pl.BlockSpec(memory_space=pltpu.SMEM),
                  pl.BlockSpec((Q_PER_KV, tq, HEAD_DIM), q_map),
                  pl.BlockSpec((1, n, HEAD_DIM), lambda bi, h, i: (h, bi, 0)),
                  pl.BlockSpec((HEAD_DIM, n), lambda bi, h, i: (h, bi)),
                  pl.BlockSpec((1, N_META, HEAD_DIM), lambda bi, h, i: (h, 0, 0)),
                  pl.BlockSpec((HEAD_DIM, N_META), lambda bi, h, i: (h, 0))],
        out_specs=pl.BlockSpec((1, tq, Q_PER_KV * HEAD_DIM), lambda bi, h, i: (bi, i, h)),
        out_shape=jax.ShapeDtypeStruct((batch, nq, ATTN_WIDTH), BF16),
        scratch_shapes=scratch,
        compiler_params=pltpu.CompilerParams(
            dimension_semantics=("parallel", "parallel", "arbitrary"),
            vmem_limit_bytes=VMEM_LIMIT_BYTES),
        name="attention",
    )(bound, q, k, vt, km, vmt)


def _softplus(x):
    return jnp.maximum(x, 0.0) + jnp.log1p(jnp.exp(-jnp.abs(x)))


def _lru_prep(xc, rows, wr_ref, br_ref, wi_ref, bi_ref, lam_ref, a_ref, u_ref):
    xcb = xc.astype(BF16)
    r_parts, i_parts = [], []
    for blk in range(LRU_BLOCKS):
        xb = xcb[:, blk * LRU_BLOCK:(blk + 1) * LRU_BLOCK]
        r_parts.append(_dot(xb, wr_ref[blk]))
        i_parts.append(_dot(xb, wi_ref[blk]))
    tr = jnp.tanh(jnp.concatenate(r_parts, axis=-1) + br_ref[...])
    ti = jnp.tanh(jnp.concatenate(i_parts, axis=-1) + bi_ref[...])
    c1 = (-0.5 * LRU_C * LOG2E) * _softplus(-lam_ref[...])
    log2_a = c1 * tr + c1
    a = jnp.exp2(log2_a)
    a_ref[0:rows, :] = a
    gate2 = jnp.tanh((-LN2) * log2_a) * (1.0 + a * a)
    gate = jnp.where(gate2 > 0.0, gate2 * lax.rsqrt(gate2), 0.0)
    half_xc = 0.5 * xc
    u_ref[0:rows, :] = gate * (half_xc * ti + half_xc)


def _lru_scan(rows, a_ref, u_ref, h, out_ref, base, reverse):
    for r in (range(rows - 1, -1, -1) if reverse else range(rows)):
        h = a_ref[r:r + 1, :] * h + u_ref[r:r + 1, :]
        out_ref[pl.ds(base + r, 1), :] = h
    return h


def _lru_fwd_kernel(xc_ref, xcm_ref, wr_ref, br_ref, wi_ref, bi_ref,
                    lam_ref, hf_ref, hfm_ref, a0_ref, u0_ref, a1_ref, u1_ref, h_ref):
    j = pl.program_id(1)
    tc = xc_ref.shape[1]
    wts = (wr_ref, br_ref, wi_ref, bi_ref, lam_ref)
    nsub = tc // LRU_SUB

    @pl.when(j == 0)
    def _():
        _lru_prep(xcm_ref[0], N_META, *wts, a0_ref, u0_ref)
        h_ref[...] = _lru_scan(N_META, a0_ref, u0_ref, jnp.zeros(h_ref.shape, F32), hfm_ref.at[0], 0, False)

    @pl.when(j > 0)
    def _():
        def prep(s, a_ref, u_ref):
            start = pl.multiple_of(s * LRU_SUB, LRU_SUB)
            _lru_prep(xc_ref[0, pl.ds(start, LRU_SUB), :], LRU_SUB, *wts, a_ref, u_ref)

        prep(0, a0_ref, u0_ref)

        def pair(i, h):
            s = 2 * i
            prep(s + 1, a1_ref, u1_ref)
            h = _lru_scan(LRU_SUB, a0_ref, u0_ref, h, hf_ref.at[0], pl.multiple_of(s * LRU_SUB, LRU_SUB), False)
            prep(jnp.minimum(s + 2, nsub - 1), a0_ref, u0_ref)
            return _lru_scan(LRU_SUB, a1_ref, u1_ref, h, hf_ref.at[0],
                             pl.multiple_of((s + 1) * LRU_SUB, LRU_SUB), False)

        h_ref[...] = lax.fori_loop(0, nsub // 2, pair, h_ref[...])


def _lru_bwd_kernel(xc_ref, xcm_ref, gy_ref, gym_ref, hf_ref, hfm_ref,
                    wr_ref, br_ref, wi_ref, bi_ref, lam_ref, og_ref,
                    out_ref, outm_ref, a0_ref, u0_ref, a1_ref, u1_ref, h_ref, hb0_ref, hb1_ref, *, nc):
    j = pl.program_id(1)
    tc = xc_ref.shape[1]
    wts = (wr_ref, br_ref, wi_ref, bi_ref, lam_ref)
    nsub = tc // LRU_SUB

    def finish(hf, hb, gy):
        return _rms((hf + hb) * jax.nn.gelu(gy), og_ref[...]).astype(BF16)

    @pl.when(j == 0)
    def _():
        h_ref[...] = jnp.zeros(h_ref.shape, F32)

    @pl.when(j < nc)
    def _():
        def prep(s, a_ref, u_ref):
            start = pl.multiple_of(s * LRU_SUB, LRU_SUB)
            _lru_prep(xc_ref[0, pl.ds(start, LRU_SUB), :], LRU_SUB, *wts, a_ref, u_ref)

        def piece(s, a_ref, u_ref, hb_ref, h):
            h = _lru_scan(LRU_SUB, a_ref, u_ref, h, hb_ref, 0, True)
            rows = pl.ds(pl.multiple_of(s * LRU_SUB, LRU_SUB), LRU_SUB)
            out_ref[0, rows, :] = finish(hf_ref[0, rows, :], hb_ref[...], gy_ref[0, rows, :])
            return h

        prep(nsub - 1, a0_ref, u0_ref)

        def pair(i, h):
            s = nsub - 1 - 2 * i
            prep(s - 1, a1_ref, u1_ref)
            h = piece(s, a0_ref, u0_ref, hb0_ref, h)
            prep(jnp.maximum(s - 2, 0), a0_ref, u0_ref)
            return piece(s - 1, a1_ref, u1_ref, hb1_ref, h)

        h_ref[...] = lax.fori_loop(0, nsub // 2, pair, h_ref[...])

    @pl.when(j == nc)
    def _():
        _lru_prep(xcm_ref[0], N_META, *wts, a0_ref, u0_ref)
        h_ref[...] = _lru_scan(N_META, a0_ref, u0_ref, h_ref[...], hb0_ref, 0, True)
        outm_ref[0] = finish(hfm_ref[0], hb0_ref[0:N_META, :], gym_ref[...])


def _lru(xc, gy, xcm, gym, w_r, b_r, w_i, b_i, lam, out_gain, tc):
    b, n, c = xc.shape
    nc = n // tc
    wspec = _const_spec((LRU_BLOCKS, LRU_BLOCK, LRU_BLOCK))
    vspec = _const_spec((1, c))
    pspecs = [wspec, vspec, wspec, vspec, vspec]
    bmspec = pl.BlockSpec((1, N_META, c), lambda bi, j: (bi, 0, 0))
    slots = [pltpu.VMEM((LRU_SUB, c), F32)] * 4
    assert tc % (2 * LRU_SUB) == 0 and LRU_SUB >= N_META

    def chunk_f(j):
        return jnp.maximum(j - 1, 0)

    hf, hfm = pl.pallas_call(
        _lru_fwd_kernel,
        grid=(b, nc + 1),
        in_specs=[pl.BlockSpec((1, tc, c), lambda bi, j: (bi, chunk_f(j), 0)), bmspec] + pspecs,
        out_specs=[pl.BlockSpec((1, tc, c), lambda bi, j: (bi, chunk_f(j), 0)), bmspec],
        out_shape=[jax.ShapeDtypeStruct((b, n, c), F32), jax.ShapeDtypeStruct((b, N_META, c), F32)],
        scratch_shapes=slots + [pltpu.VMEM((1, c), F32)],
        compiler_params=pltpu.CompilerParams(dimension_semantics=("parallel", "arbitrary"),
                                             vmem_limit_bytes=VMEM_LIMIT_BYTES),
        name="lru_fwd",
    )(xc, xcm, w_r[0], b_r[0:1], w_i[0], b_i[0:1], lam[0:1])

    def chunk_b(j):
        return jnp.maximum(nc - 1 - j, 0)

    chunk_spec = pl.BlockSpec((1, tc, c), lambda bi, j: (bi, chunk_b(j), 0))
    out, outm = pl.pallas_call(
        functools.partial(_lru_bwd_kernel, nc=nc),
        grid=(b, nc + 1),
        in_specs=[chunk_spec, bmspec, chunk_spec, _const_spec((N_META, c)), chunk_spec, bmspec] + pspecs + [vspec],
        out_specs=[chunk_spec, bmspec],
        out_shape=[jax.ShapeDtypeStruct((b, n, c), BF16), jax.ShapeDtypeStruct((b, N_META, c), BF16)],
        scratch_shapes=slots + [pltpu.VMEM((1, c), F32),
                                pltpu.VMEM((LRU_SUB, c), F32), pltpu.VMEM((LRU_SUB, c), F32)],
        compiler_params=pltpu.CompilerParams(dimension_semantics=("parallel", "arbitrary"),
                                             vmem_limit_bytes=VMEM_LIMIT_BYTES),
        name="lru_bwd",
    )(xc, xcm, gy, gym, hf, hfm, w_r[1], b_r[1:2], w_i[1], b_i[1:2], lam[1:2], out_gain)
    return out, outm


def _out_proj_kernel(attn_ref, lru_ref, res_ref, ag_ref, wa_ref, wl_ref, pg_ref, fg_ref, h1_ref, hn_ref):
    tm = attn_ref.shape[0]
    parts = OUT_PROJ_PARTS if tm % (OUT_PROJ_PARTS * LANES) == 0 else 1
    part = tm // parts
    for s in range(parts):
        rows = slice(s * part, (s + 1) * part)
        an = _rms(attn_ref[rows, :].astype(F32), ag_ref[...]).astype(BF16)
        mixed = _dot(an, wa_ref[...]) + _dot(lru_ref[rows, :], wl_ref[...])
        h1 = res_ref[rows, :] + _rms(mixed, pg_ref[...])
        h1_ref[rows, :] = h1
        hn_ref[rows, :] = _rms(h1, fg_ref[...]).astype(BF16)


def _out_proj(attn2d, lru2d, res2d, attn_gain, w_a, w_l, post_gain, ffn_gain, tm):
    rows = attn2d.shape[0]
    row_spec = lambda w: pl.BlockSpec((tm, w), lambda i: (i, 0))
    wspec = pl.BlockSpec((ATTN_WIDTH, D_MODEL), lambda i: (0, 0), pipeline_mode=pl.Buffered(1))
    return pl.pallas_call(
        _out_proj_kernel,
        grid=(rows // tm,),
        in_specs=[row_spec(ATTN_WIDTH), row_spec(LRU_WIDTH), row_spec(D_MODEL), _const_spec((1, ATTN_WIDTH)),
                  wspec, wspec, _const_spec((1, D_MODEL)), _const_spec((1, D_MODEL))],
        out_specs=[row_spec(D_MODEL), row_spec(D_MODEL)],
        out_shape=[jax.ShapeDtypeStruct((rows, D_MODEL), F32), jax.ShapeDtypeStruct((rows, D_MODEL), BF16)],
        compiler_params=pltpu.CompilerParams(dimension_semantics=("parallel",),
                                             vmem_limit_bytes=VMEM_LIMIT_BYTES),
        name="out_proj",
    )(attn2d, lru2d, res2d, attn_gain, w_a, w_l, post_gain, ffn_gain)


def _ffn_kernel(hn_ref, prev_ref, next_ref, hm_ref, h1_ref, wg_ref, wv_ref, wd_ref, cw_ref, cb_ref, og_ref,
                out_ref, ext_ref, *, nt, nf):
    i = pl.program_id(1)
    f = pl.program_id(2)
    tm = hn_ref.shape[1]
    halo = SUBLANES_BF16
    acc_ref = out_ref.at[0]

    @pl.when(f == 0)
    def _():
        ext_ref[0:halo, :] = jnp.where(i == 0, hm_ref[0], prev_ref[0])
        ext_ref[halo:halo + tm, :] = hn_ref[0]
        ext_ref[halo + tm:2 * halo + tm, :] = jnp.where(i == nt - 1, jnp.zeros_like(next_ref[0]), next_ref[0])
        acc_ref[...] = jnp.zeros(acc_ref.shape, F32)

    n_ext = tm + 2 * halo
    gfull = _dot(ext_ref[...], wg_ref[...])
    g = jnp.broadcast_to(cb_ref[...], (tm, gfull.shape[1]))
    for kk in range(FFN_CONV_W):
        shift = (FFN_CONV_W // 2 - kk) % n_ext
        gk = gfull if shift == 0 else pltpu.roll(gfull, shift, axis=0)
        g = g + gk[halo:halo + tm] * cw_ref[kk:kk + 1, :]
    val = _dot(ext_ref[halo:halo + tm, :], wv_ref[...])
    act = (jax.nn.silu(g) * val).astype(BF16)
    acc_ref[...] += _dot(act, wd_ref[...])

    @pl.when(f == nf - 1)
    def _():
        acc_ref[...] = h1_ref[0] + _rms(acc_ref[...], og_ref[...])


def _ffn(hn, hnm, h1, w_up, w_down, conv_w, conv_b, out_gain, tm, fc):
    b, n, d = hn.shape
    nt = n // tm
    nf = FFN_DIM // fc
    halo = SUBLANES_BF16
    th = tm // halo
    nh = n // halo
    return pl.pallas_call(
        functools.partial(_ffn_kernel, nt=nt, nf=nf),
        grid=(b, nt, nf),
        in_specs=[pl.BlockSpec((1, tm, d), lambda bi, i, f: (bi, i, 0)),
                  pl.BlockSpec((1, halo, d), lambda bi, i, f: (bi, jnp.maximum(i * th - 1, 0), 0)),
                  pl.BlockSpec((1, halo, d), lambda bi, i, f: (bi, jnp.minimum((i + 1) * th, nh - 1), 0)),
                  pl.BlockSpec((1, N_META, d), lambda bi, i, f: (bi, 0, 0)),
                  pl.BlockSpec((1, tm, d), lambda bi, i, f: (bi, i, 0)),
                  pl.BlockSpec((d, fc), lambda bi, i, f: (0, f)),
                  pl.BlockSpec((d, fc), lambda bi, i, f: (0, f + nf)),
                  pl.BlockSpec((fc, d), lambda bi, i, f: (f, 0)),
                  pl.BlockSpec((FFN_CONV_W, fc), lambda bi, i, f: (0, f)),
                  pl.BlockSpec((1, fc), lambda bi, i, f: (0, f)),
                  _const_spec((1, d))],
        out_specs=pl.BlockSpec((1, tm, d), lambda bi, i, f: (bi, i, 0)),
        out_shape=jax.ShapeDtypeStruct((b, n, d), F32),
        scratch_shapes=[pltpu.VMEM((tm + 2 * halo, d), BF16)],
        compiler_params=pltpu.CompilerParams(
            dimension_semantics=("parallel", "parallel", "arbitrary"),
            vmem_limit_bytes=VMEM_LIMIT_BYTES),
        name="conv_ffn",
    )(hn, hn, hn, hnm, h1, w_up, w_up, w_down, conv_w, conv_b, out_gain)


def _rope_tables(n):
    rows = n // GRID_W
    t_row = jnp.repeat(jnp.arange(rows), GRID_W).astype(F32)
    t_col = jnp.tile(jnp.arange(GRID_W), rows).astype(F32)
    half = HEAD_DIM // 2
    inv = ROPE_THETA ** (-jnp.arange(0, half, 2, dtype=F32) / half)
    ang = jnp.concatenate([t_row[:, None] * inv, t_col[:, None] * inv], axis=-1)
    cos, sin = jnp.cos(ang), jnp.sin(ang)
    return jnp.concatenate([cos, cos], axis=-1), jnp.concatenate([-sin, sin], axis=-1)


_TILES = dict(tm=512, tq=512, tk=1024, tc=1024, tf=512, fc=512)


def _pick(n, pref):
    t = pref
    while n % t:
        t //= 2
    return t


def _trunk(x, meta, mp, p):
    b, n, d = x.shape
    rows = b * n
    cos_t, sin_t = _rope_tables(n)
    tm = _pick(n, _TILES["tm"])
    nt = n // tm
    q, k, vt, xc, gy, xc_head = _in_proj(x.reshape(rows, d), meta, nt, p["norm_pre_mix"], p["w_in"], p["q_norm"],
                                         p["k_norm"], cos_t, sin_t, p["lru_conv_w"], p["lru_conv_b"], tm)
    head = xc_head.reshape(b, nt, SUBLANES_BF16, LRU_WIDTH)[:, 0, SUBLANES_BF16 - 2:, :]
    xc_m = jnp.concatenate([jnp.broadcast_to(mp["xc"][None, :N_META - 2], (b, N_META - 2, LRU_WIDTH)), head], axis=1)
    tq = _pick(n, _TILES["tq"])
    tk = _pick(n, _TILES["tk"])
    bound = mp["bound"]

    def attend(running_max):
        def run(bound, q, qm, k, vt, km, vmt):
            real = _attention(bound, q, k, vt, km, vmt, b, tq, tk, False, running_max)
            meta_rows = _attention(bound, qm, k, vt, km, vmt, b, LANES, tk, True, running_max)
            return real, meta_rows
        return run

    attn, attn_m = lax.cond(2.0 * bound[0] <= MAX_FIXED_SHIFT_RANGE, attend(False), attend(True),
                            bound, q, mp["q"], k, vt, mp["km"], mp["vmt"])
    attn_m = attn_m[:, :N_META]
    lru, lru_m = _lru(xc.reshape(b, n, LRU_WIDTH), gy.reshape(b, n, LRU_WIDTH), xc_m, mp["gy"],
                      p["lru_w_r"], p["lru_b_r"], p["lru_w_i"], p["lru_b_i"],
                      p["lru_lambda"], p["lru_out_norm"], _pick(n, _TILES["tc"]))
    op = (p["attn_out_norm"], p["w_out_a"], p["w_out_l"], p["norm_post_mix"], p["norm_pre_ffn"])
    h1, hn = _out_proj(attn.reshape(rows, ATTN_WIDTH), lru.reshape(rows, LRU_WIDTH), x.reshape(rows, d), *op, tm)
    res_m = jnp.broadcast_to(meta[None], (b, N_META, d)).reshape(b * N_META, d)
    _, hn_m = _out_proj(attn_m.reshape(b * N_META, ATTN_WIDTH), lru_m.reshape(b * N_META, LRU_WIDTH), res_m,
                        *op, N_META)
    return _ffn(hn.reshape(b, n, d), hn_m.reshape(b, N_META, d), h1.reshape(b, n, d), p["w_up"], p["w_down"],
                p["ffn_conv_w"], p["ffn_conv_b"], p["norm_post_ffn"], _pick(n, _TILES["tf"]), _TILES["fc"])


def _head_perm():
    q4 = HEAD_DIM // 4
    idx = jnp.arange(HEAD_DIM).reshape(2, 2, q4)
    return idx.transpose(1, 0, 2).reshape(HEAD_DIM)


def kernel(x_prompt, x_sample, meta_tokens, norm_pre_mix, w_in, q_norm, k_norm, lru_conv_w, lru_conv_b,
           lru_w_r, lru_b_r, lru_w_i, lru_b_i, lru_lambda, attn_out_norm, lru_out_norm, w_out,
           norm_post_mix, norm_pre_ffn, w_up, ffn_conv_w, ffn_conv_b, w_down, norm_post_ffn):
    perm = _head_perm()
    n_rot = N_Q_HEADS + N_KV_HEADS
    cols = (jnp.arange(n_rot)[:, None] * HEAD_DIM + perm[None, :]).reshape(-1)
    cols = jnp.concatenate([cols, jnp.arange(n_rot * HEAD_DIM, IN_WIDTH)])
    w_out_b = w_out[0].astype(BF16)
    p = {
        "norm_pre_mix": norm_pre_mix[0][None], "w_in": w_in[0][:, cols].astype(BF16),
        "q_norm": q_norm[0][perm][None], "k_norm": k_norm[0][perm][None],
        "lru_conv_w": lru_conv_w[0], "lru_conv_b": lru_conv_b[0][None],
        "lru_w_r": (0.5 * lru_w_r[0]).astype(BF16), "lru_b_r": 0.5 * lru_b_r[0],
        "lru_w_i": (0.5 * lru_w_i[0]).astype(BF16), "lru_b_i": 0.5 * lru_b_i[0], "lru_lambda": lru_lambda[0],
        "attn_out_norm": attn_out_norm[0][None], "lru_out_norm": lru_out_norm[0][None],
        "w_out_a": w_out_b[:ATTN_WIDTH], "w_out_l": w_out_b[ATTN_WIDTH:],
        "norm_post_mix": norm_post_mix[0][None], "norm_pre_ffn": norm_pre_ffn[0][None],
        "w_up": w_up[0].astype(BF16), "ffn_conv_w": ffn_conv_w[0], "ffn_conv_b": ffn_conv_b[0][None],
        "w_down": w_down[0].astype(BF16), "norm_post_ffn": norm_post_ffn[0][None],
    }
    ones = jnp.ones((LANES, HEAD_DIM), F32)
    meta_pad = jnp.pad(meta_tokens, ((0, LANES - N_META), (0, 0)))
    qm, km, vtm, xcm, gym, _ = _in_proj(meta_pad, jnp.zeros((SUBLANES_BF16, D_MODEL), F32), 1, p["norm_pre_mix"],
                                        p["w_in"], p["q_norm"], p["k_norm"], ones, jnp.zeros_like(ones),
                                        p["lru_conv_w"], p["lru_conv_b"], LANES)
    bound = (HEAD_DIM * Q_PRESCALE * BF16_ROUNDING_SLACK) * jnp.max(jnp.abs(q_norm[0])) * jnp.max(jnp.abs(k_norm[0]))
    mp = {"q": qm, "km": km[:, :N_META], "vmt": vtm[:, :N_META], "xc": xcm[:N_META], "gy": gym[:N_META],
          "bound": bound.reshape(1).astype(F32)}
    return _trunk(x_prompt, meta_tokens, mp, p), _trunk(x_sample, meta_tokens, mp, p)
```

```python
import functools

import jax
import jax.numpy as jnp
from jax import lax
from jax.experimental import pallas as pl
from jax.experimental.pallas import tpu as pltpu

D_MODEL = 2048
N_META = 16
GRID_W = 64
HEAD_DIM = 128
N_Q_HEADS = 8
N_KV_HEADS = 2
Q_PER_KV = N_Q_HEADS // N_KV_HEADS
ATTN_WIDTH = N_Q_HEADS * HEAD_DIM
KV_WIDTH = N_KV_HEADS * HEAD_DIM
LRU_WIDTH = D_MODEL - ATTN_WIDTH
LRU_BLOCKS = 8
LRU_BLOCK = LRU_WIDTH // LRU_BLOCKS
LRU_CONV_W = 4
LRU_C = 8.0
LRU_SUB = 64
OUT_PROJ_PARTS = 2
ATTN_KV_UNROLL = 8
IN_WIDTH = ATTN_WIDTH + 2 * KV_WIDTH + 2 * LRU_WIDTH
FFN_DIM = 5632
FFN_CONV_W = 3
ROPE_THETA = 10000.0
EPS = 1e-6
ATTN_SCALE = HEAD_DIM ** -0.5
LOG2E = 1.4426950408889634
LN2 = 0.6931471805599453
Q_PRESCALE = ATTN_SCALE * LOG2E
MAX_FIXED_SHIFT_RANGE = 120.0
BF16_ROUNDING_SLACK = 1.0 + 2.0 ** -6

LANES = 128
SUBLANES_F32 = 8
SUBLANES_BF16 = 16
VMEM_LIMIT_BYTES = 56 * 1024 * 1024

F32 = jnp.float32
BF16 = jnp.bfloat16


def _rms(x, g):
    ms = jnp.mean(x * x, axis=-1, keepdims=True)
    return x * lax.rsqrt(ms + EPS) * g


def _dot(a, b):
    return jnp.dot(a, b, preferred_element_type=F32)


def _const_spec(shape):
    zeros = (0,) * len(shape)
    return pl.BlockSpec(shape, lambda *_: zeros)


def _in_proj_kernel(x_ref, xprev_ref, xnext_ref, xhead_ref, g_ref, w_ref, qg_ref, kg_ref, cos_ref, sin_ref,
                    cw_ref, cb_ref, q_ref, k_ref, vt_ref, xc_ref, gy_ref, xch_ref, *, nt):
    tm = x_ref.shape[0]
    halo = SUBLANES_BF16
    pos = pl.program_id(0) % nt
    xprev = jnp.where(pos == 0, xhead_ref[...], xprev_ref[...])
    xnext = jnp.where(pos == nt - 1, jnp.zeros_like(xnext_ref[...]), xnext_ref[...])
    xn = _rms(x_ref[...], g_ref[...]).astype(BF16)
    xn_ext = jnp.concatenate([_rms(xprev, g_ref[...]).astype(BF16), xn, _rms(xnext, g_ref[...]).astype(BF16)], axis=0)
    cos = cos_ref[...]
    sin = sin_ref[...]

    def rope_head(zh, gain):
        y = _rms(zh, gain)
        return y * cos + pltpu.roll(y, HEAD_DIM // 2, axis=1) * sin

    o3 = ATTN_WIDTH + 2 * KV_WIDTH
    zq = _dot(xn, w_ref[:, 0:ATTN_WIDTH])
    for h in range(N_Q_HEADS):
        sl = slice(h * HEAD_DIM, (h + 1) * HEAD_DIM)
        q_ref[h] = (rope_head(zq[:, sl], qg_ref[...]) * Q_PRESCALE).astype(BF16)
    xr_ext = _dot(xn_ext, w_ref[:, o3:o3 + LRU_WIDTH])
    n_ext = tm + 2 * halo
    xc_ext = jnp.broadcast_to(cb_ref[...], (n_ext, LRU_WIDTH))
    for kk in range(LRU_CONV_W):
        shift = (LRU_CONV_W // 2 - kk) % n_ext
        xk = xr_ext if shift == 0 else pltpu.roll(xr_ext, shift, axis=0)
        xc_ext = xc_ext + xk * cw_ref[kk:kk + 1, :]
    xc_ref[...] = xc_ext[halo:halo + tm]
    xch_ref[...] = xc_ext[0:halo]
    zkv = _dot(xn, w_ref[:, ATTN_WIDTH:ATTN_WIDTH + 2 * KV_WIDTH])
    for h in range(N_KV_HEADS):
        sl = slice(h * HEAD_DIM, (h + 1) * HEAD_DIM)
        k_ref[h] = rope_head(zkv[:, sl], kg_ref[...]).astype(BF16)
    vt_ref[...] = zkv[:, KV_WIDTH:2 * KV_WIDTH].T.astype(BF16)
    gy_ref[...] = _dot(xn, w_ref[:, o3 + LRU_WIDTH:o3 + 2 * LRU_WIDTH])


def _in_proj(x2d, xhead, nt, gain, w_in, q_gain, k_gain, cos_t, sin_t, conv_w, conv_b, tm):
    rows = x2d.shape[0]
    n_tab = cos_t.shape[0] // tm
    halo = SUBLANES_BF16
    th = tm // halo
    nh = rows // halo
    row_spec = lambda w: pl.BlockSpec((tm, w), lambda i: (i, 0))
    tab_spec = pl.BlockSpec((tm, HEAD_DIM), lambda i: (i % n_tab, 0))
    return pl.pallas_call(
        functools.partial(_in_proj_kernel, nt=nt),
        grid=(rows // tm,),
        in_specs=[row_spec(D_MODEL),
                  pl.BlockSpec((halo, D_MODEL), lambda i: (jnp.maximum(i * th - 1, 0), 0)),
                  pl.BlockSpec((halo, D_MODEL), lambda i: (jnp.minimum((i + 1) * th, nh - 1), 0)),
                  _const_spec((halo, D_MODEL)), _const_spec((1, D_MODEL)),
                  pl.BlockSpec((D_MODEL, IN_WIDTH), lambda i: (0, 0), pipeline_mode=pl.Buffered(1)),
                  _const_spec((1, HEAD_DIM)), _const_spec((1, HEAD_DIM)), tab_spec, tab_spec,
                  _const_spec((LRU_CONV_W, LRU_WIDTH)), _const_spec((1, LRU_WIDTH))],
        out_specs=[pl.BlockSpec((N_Q_HEADS, tm, HEAD_DIM), lambda i: (0, i, 0)),
                   pl.BlockSpec((N_KV_HEADS, tm, HEAD_DIM), lambda i: (0, i, 0)),
                   pl.BlockSpec((KV_WIDTH, tm), lambda i: (0, i)),
                   row_spec(LRU_WIDTH), row_spec(LRU_WIDTH),
                   pl.BlockSpec((halo, LRU_WIDTH), lambda i: (i, 0))],
        out_shape=[jax.ShapeDtypeStruct((N_Q_HEADS, rows, HEAD_DIM), BF16),
                   jax.ShapeDtypeStruct((N_KV_HEADS, rows, HEAD_DIM), BF16),
                   jax.ShapeDtypeStruct((KV_WIDTH, rows), BF16),
                   jax.ShapeDtypeStruct((rows, LRU_WIDTH), F32),
                   jax.ShapeDtypeStruct((rows, LRU_WIDTH), F32),
                   jax.ShapeDtypeStruct((rows // tm * halo, LRU_WIDTH), F32)],
        compiler_params=pltpu.CompilerParams(dimension_semantics=("parallel",),
                                             vmem_limit_bytes=VMEM_LIMIT_BYTES),
        name="in_proj",
    )(x2d, x2d, x2d, xhead, gain, w_in, q_gain, k_gain, cos_t, sin_t, conv_w, conv_b)


def _attn_kernel(bound_ref, q_ref, k_ref, vt_ref, km_ref, vmt_ref, o_ref, l_ref, acc_ref, *m_scratch,
                 nkv, tk, running_max):
    tq = q_ref.shape[1]
    width = Q_PER_KV * tq
    qs = q_ref[...].reshape(width, HEAD_DIM)
    nt = (((1,), (1,)), ((), ()))

    def block(kb, vtb, first):
        st = lax.dot_general(kb, qs, nt, preferred_element_type=F32)
        if running_max:
            m_ref, = m_scratch
            m_new = jnp.max(st, axis=0, keepdims=True)
            if not first:
                m_prev = m_ref[...]
                m_new = jnp.maximum(m_prev, m_new)
                alpha = jnp.exp2(m_prev - m_new)
            m_ref[...] = m_new
            p = jnp.exp2(st - m_new)
        else:
            p = jnp.exp2(st - bound_ref[0])
        psum = jnp.sum(p.reshape(p.shape[0] // SUBLANES_F32, SUBLANES_F32, width), axis=0)
        pv = _dot(vtb, p.astype(BF16))
        if first:
            l_ref[...] = psum
            acc_ref[...] = pv
        elif running_max:
            l_ref[...] = alpha * l_ref[...] + psum
            acc_ref[...] = alpha * acc_ref[...] + pv
        else:
            l_ref[...] += psum
            acc_ref[...] += pv

    block(km_ref[0], vmt_ref[...], True)

    def body(j, carry):
        start = pl.multiple_of(j * tk, tk)
        block(k_ref[0, pl.ds(start, tk), :], vt_ref[:, pl.ds(start, tk)], False)
        return carry

    unroll = 1 if running_max else next(u for u in (ATTN_KV_UNROLL, 4, 2, 1) if nkv % u == 0)
    lax.fori_loop(0, nkv, body, 0, unroll=unroll)

    o = acc_ref[...] * (1.0 / jnp.sum(l_ref[...], axis=0, keepdims=True))
    for g in range(Q_PER_KV):
        o_ref[0, :, g * HEAD_DIM:(g + 1) * HEAD_DIM] = o[:, g * tq:(g + 1) * tq].T.astype(BF16)


def _attention(bound, q, k, vt, km, vmt, batch, tq, tk, shared_q, running_max):
    n = k.shape[1] // batch
    nq = q.shape[1] if shared_q else q.shape[1] // batch
    nkv = n // tk
    nqt = nq // tq
    width = Q_PER_KV * tq
    if shared_q:
        q_map = lambda bi, h, i: (h, i, 0)
    else:
        q_map = lambda bi, h, i: (h, bi * nqt + i, 0)
    scratch = [pltpu.VMEM((SUBLANES_F32, width), F32), pltpu.VMEM((HEAD_DIM, width), F32)]
    if running_max:
        scratch.append(pltpu.VMEM((1, width), F32))
    return pl.pallas_call(
        functools.partial(_attn_kernel, nkv=nkv, tk=tk, running_max=running_max),
        grid=(batch, N_KV_HEADS, nqt),
        in_specs=[pl.BlockSpec(memory_space=pltpu.SMEM),
                  pl.BlockSpec((Q_PER_KV, tq, HEAD_DIM), q_map),
                  pl.BlockSpec((1, n, HEAD_DIM), lambda bi, h, i: (h, bi, 0)),
                  pl.BlockSpec((HEAD_DIM, n), lambda bi, h, i: (h, bi)),
                  pl.BlockSpec((1, N_META, HEAD_DIM), lambda bi, h, i: (h, 0, 0)),
                  pl.BlockSpec((HEAD_DIM, N_META), lambda bi, h, i: (h, 0))],
        out_specs=pl.BlockSpec((1, tq, Q_PER_KV * HEAD_DIM), lambda bi, h, i: (bi, i, h)),
        out_shape=jax.ShapeDtypeStruct((batch, nq, ATTN_WIDTH), BF16),
        scratch_shapes=scratch,
        compiler_params=pltpu.CompilerParams(
            dimension_semantics=("parallel", "parallel", "arbitrary"),
            vmem_limit_bytes=VMEM_LIMIT_BYTES),
        name="attention",
    )(bound, q, k, vt, km, vmt)


def _softplus(x):
    return jnp.maximum(x, 0.0) + jnp.log1p(jnp.exp(-jnp.abs(x)))


def _lru_prep(xc, rows, wr_ref, br_ref, wi_ref, bi_ref, lam_ref, a_ref, u_ref):
    xcb = xc.astype(BF16)
    r_parts, i_parts = [], []
    for blk in range(LRU_BLOCKS):
        xb = xcb[:, blk * LRU_BLOCK:(blk + 1) * LRU_BLOCK]
        r_parts.append(_dot(xb, wr_ref[blk]))
        i_parts.append(_dot(xb, wi_ref[blk]))
    tr = jnp.tanh(jnp.concatenate(r_parts, axis=-1) + br_ref[...])
    ti = jnp.tanh(jnp.concatenate(i_parts, axis=-1) + bi_ref[...])
    c1 = (-0.5 * LRU_C * LOG2E) * _softplus(-lam_ref[...])
    log2_a = c1 * tr + c1
    a = jnp.exp2(log2_a)
    a_ref[0:rows, :] = a
    gate2 = jnp.tanh((-LN2) * log2_a) * (1.0 + a * a)
    gate = jnp.where(gate2 > 0.0, gate2 * lax.rsqrt(gate2), 0.0)
    half_xc = 0.5 * xc
    u_ref[0:rows, :] = gate * (half_xc * ti + half_xc)


def _lru_scan(rows, a_ref, u_ref, h, out_ref, base, reverse):
    for r in (range(rows - 1, -1, -1) if reverse else range(rows)):
        h = a_ref[r:r + 1, :] * h + u_ref[r:r + 1, :]
        out_ref[pl.ds(base + r, 1), :] = h
    return h


def _lru_fwd_kernel(xc_ref, xcm_ref, wr_ref, br_ref, wi_ref, bi_ref,
                    lam_ref, hf_ref, hfm_ref, a0_ref, u0_ref, a1_ref, u1_ref, h_ref):
    j = pl.program_id(1)
    tc = xc_ref.shape[1]
    wts = (wr_ref, br_ref, wi_ref, bi_ref, lam_ref)
    nsub = tc // LRU_SUB

    @pl.when(j == 0)
    def _():
        _lru_prep(xcm_ref[0], N_META, *wts, a0_ref, u0_ref)
        h_ref[...] = _lru_scan(N_META, a0_ref, u0_ref, jnp.zeros(h_ref.shape, F32), hfm_ref.at[0], 0, False)

    @pl.when(j > 0)
    def _():
        def prep(s, a_ref, u_ref):
            start = pl.multiple_of(s * LRU_SUB, LRU_SUB)
            _lru_prep(xc_ref[0, pl.ds(start, LRU_SUB), :], LRU_SUB, *wts, a_ref, u_ref)

        prep(0, a0_ref, u0_ref)

        def pair(i, h):
            s = 2 * i
            prep(s + 1, a1_ref, u1_ref)
            h = _lru_scan(LRU_SUB, a0_ref, u0_ref, h, hf_ref.at[0], pl.multiple_of(s * LRU_SUB, LRU_SUB), False)
            prep(jnp.minimum(s + 2, nsub - 1), a0_ref, u0_ref)
            return _lru_scan(LRU_SUB, a1_ref, u1_ref, h, hf_ref.at[0],
                             pl.multiple_of((s + 1) * LRU_SUB, LRU_SUB), False)

        h_ref[...] = lax.fori_loop(0, nsub // 2, pair, h_ref[...])


def _lru_bwd_kernel(xc_ref, xcm_ref, gy_ref, gym_ref, hf_ref, hfm_ref,
                    wr_ref, br_ref, wi_ref, bi_ref, lam_ref, og_ref,
                    out_ref, outm_ref, a0_ref, u0_ref, a1_ref, u1_ref, h_ref, hb0_ref, hb1_ref, *, nc):
    j = pl.program_id(1)
    tc = xc_ref.shape[1]
    wts = (wr_ref, br_ref, wi_ref, bi_ref, lam_ref)
    nsub = tc // LRU_SUB

    def finish(hf, hb, gy):
        return _rms((hf + hb) * jax.nn.gelu(gy), og_ref[...]).astype(BF16)

    @pl.when(j == 0)
    def _():
        h_ref[...] = jnp.zeros(h_ref.shape, F32)

    @pl.when(j < nc)
    def _():
        def prep(s, a_ref, u_ref):
            start = pl.multiple_of(s * LRU_SUB, LRU_SUB)
            _lru_prep(xc_ref[0, pl.ds(start, LRU_SUB), :], LRU_SUB, *wts, a_ref, u_ref)

        def piece(s, a_ref, u_ref, hb_ref, h):
            h = _lru_scan(LRU_SUB, a_ref, u_ref, h, hb_ref, 0, True)
            rows = pl.ds(pl.multiple_of(s * LRU_SUB, LRU_SUB), LRU_SUB)
            out_ref[0, rows, :] = finish(hf_ref[0, rows, :], hb_ref[...], gy_ref[0, rows, :])
            return h

        prep(nsub - 1, a0_ref, u0_ref)

        def pair(i, h):
            s = nsub - 1 - 2 * i
            prep(s - 1, a1_ref, u1_ref)
            h = piece(s, a0_ref, u0_ref, hb0_ref, h)
            prep(jnp.maximum(s - 2, 0), a0_ref, u0_ref)
            return piece(s - 1, a1_ref, u1_ref, hb1_ref, h)

        h_ref[...] = lax.fori_loop(0, nsub // 2, pair, h_ref[...])

    @pl.when(j == nc)
    def _():
        _lru_prep(xcm_ref[0], N_META, *wts, a0_ref, u0_ref)
        h_ref[...] = _lru_scan(N_META, a0_ref, u0_ref, h_ref[...], hb0_ref, 0, True)
        outm_ref[0] = finish(hfm_ref[0], hb0_ref[0:N_META, :], gym_ref[...])


def _lru(xc, gy, xcm, gym, w_r, b_r, w_i, b_i, lam, out_gain, tc):
    b, n, c = xc.shape
    nc = n // tc
    wspec = _const_spec((LRU_BLOCKS, LRU_BLOCK, LRU_BLOCK))
    vspec = _const_spec((1, c))
    pspecs = [wspec, vspec, wspec, vspec, vspec]
    bmspec = pl.BlockSpec((1, N_META, c), lambda bi, j: (bi, 0, 0))
    slots = [pltpu.VMEM((LRU_SUB, c), F32)] * 4
    assert tc % (2 * LRU_SUB) == 0 and LRU_SUB >= N_META

    def chunk_f(j):
        return jnp.maximum(j - 1, 0)

    hf, hfm = pl.pallas_call(
        _lru_fwd_kernel,
        grid=(b, nc + 1),
        in_specs=[pl.BlockSpec((1, tc, c), lambda bi, j: (bi, chunk_f(j), 0)), bmspec] + pspecs,
        out_specs=[pl.BlockSpec((1, tc, c), lambda bi, j: (bi, chunk_f(j), 0)), bmspec],
        out_shape=[jax.ShapeDtypeStruct((b, n, c), F32), jax.ShapeDtypeStruct((b, N_META, c), F32)],
        scratch_shapes=slots + [pltpu.VMEM((1, c), F32)],
        compiler_params=pltpu.CompilerParams(dimension_semantics=("parallel", "arbitrary"),
                                             vmem_limit_bytes=VMEM_LIMIT_BYTES),
        name="lru_fwd",
    )(xc, xcm, w_r[0], b_r[0:1], w_i[0], b_i[0:1], lam[0:1])

    def chunk_b(j):
        return jnp.maximum(nc - 1 - j, 0)

    chunk_spec = pl.BlockSpec((1, tc, c), lambda bi, j: (bi, chunk_b(j), 0))
    out, outm = pl.pallas_call(
        functools.partial(_lru_bwd_kernel, nc=nc),
        grid=(b, nc + 1),
        in_specs=[chunk_spec, bmspec, chunk_spec, _const_spec((N_META, c)), chunk_spec, bmspec] + pspecs + [vspec],
        out_specs=[chunk_spec, bmspec],
        out_shape=[jax.ShapeDtypeStruct((b, n, c), BF16), jax.ShapeDtypeStruct((b, N_META, c), BF16)],
        scratch_shapes=slots + [pltpu.VMEM((1, c), F32),
                                pltpu.VMEM((LRU_SUB, c), F32), pltpu.VMEM((LRU_SUB, c), F32)],
        compiler_params=pltpu.CompilerParams(dimension_semantics=("parallel", "arbitrary"),
                                             vmem_limit_bytes=VMEM_LIMIT_BYTES),
        name="lru_bwd",
    )(xc, xcm, gy, gym, hf, hfm, w_r[1], b_r[1:2], w_i[1], b_i[1:2], lam[1:2], out_gain)
    return out, outm


def _out_proj_kernel(attn_ref, lru_ref, res_ref, ag_ref, wa_ref, wl_ref, pg_ref, fg_ref, h1_ref, hn_ref):
    tm = attn_ref.shape[0]
    parts = OUT_PROJ_PARTS if tm % (OUT_PROJ_PARTS * LANES) == 0 else 1
    part = tm // parts
    for s in range(parts):
        rows = slice(s * part, (s + 1) * part)
        an = _rms(attn_ref[rows, :].astype(F32), ag_ref[...]).astype(BF16)
        mixed = _dot(an, wa_ref[...]) + _dot(lru_ref[rows, :], wl_ref[...])
        h1 = res_ref[rows, :] + _rms(mixed, pg_ref[...])
        h1_ref[rows, :] = h1
        hn_ref[rows, :] = _rms(h1, fg_ref[...]).astype(BF16)


def _out_proj(attn2d, lru2d, res2d, attn_gain, w_a, w_l, post_gain, ffn_gain, tm):
    rows = attn2d.shape[0]
    row_spec = lambda w: pl.BlockSpec((tm, w), lambda i: (i, 0))
    wspec = pl.BlockSpec((ATTN_WIDTH, D_MODEL), lambda i: (0, 0), pipeline_mode=pl.Buffered(1))
    return pl.pallas_call(
        _out_proj_kernel,
        grid=(rows // tm,),
        in_specs=[row_spec(ATTN_WIDTH), row_spec(LRU_WIDTH), row_spec(D_MODEL), _const_spec((1, ATTN_WIDTH)),
                  wspec, wspec, _const_spec((1, D_MODEL)), _const_spec((1, D_MODEL))],
        out_specs=[row_spec(D_MODEL), row_spec(D_MODEL)],
        out_shape=[jax.ShapeDtypeStruct((rows, D_MODEL), F32), jax.ShapeDtypeStruct((rows, D_MODEL), BF16)],
        compiler_params=pltpu.CompilerParams(dimension_semantics=("parallel",),
                                             vmem_limit_bytes=VMEM_LIMIT_BYTES),
        name="out_proj",
    )(attn2d, lru2d, res2d, attn_gain, w_a, w_l, post_gain, ffn_gain)


def _ffn_kernel(hn_ref, prev_ref, next_ref, hm_ref, h1_ref, wga_ref, wva_ref, wda_ref, cwa_ref, cba_ref,
                wgb_ref, wvb_ref, wdb_ref, cwb_ref, cbb_ref, og_ref, out_ref, ext_ref, *, nt, nsteps, single_first):
    i = pl.program_id(1)
    f = pl.program_id(2)
    tm = hn_ref.shape[1]
    halo = SUBLANES_BF16
    n_ext = tm + 2 * halo
    acc_ref = out_ref.at[0]
    chunk_a = (wga_ref, wva_ref, wda_ref, cwa_ref, cba_ref)
    chunk_b = (wgb_ref, wvb_ref, wdb_ref, cwb_ref, cbb_ref)

    def chunk(wg_ref, wv_ref, wd_ref, cw_ref, cb_ref):
        gfull = _dot(ext_ref[...], wg_ref[...])
        g = jnp.broadcast_to(cb_ref[...], (tm, gfull.shape[1]))
        for kk in range(FFN_CONV_W):
            shift = (FFN_CONV_W // 2 - kk) % n_ext
            gk = gfull if shift == 0 else pltpu.roll(gfull, shift, axis=0)
            g = g + gk[halo:halo + tm] * cw_ref[kk:kk + 1, :]
        val = _dot(ext_ref[halo:halo + tm, :], wv_ref[...])
        act = (jax.nn.silu(g) * val).astype(BF16)
        acc_ref[...] += _dot(act, wd_ref[...])

    @pl.when(f == 0)
    def _():
        ext_ref[0:halo, :] = jnp.where(i == 0, hm_ref[0], prev_ref[0])
        ext_ref[halo:halo + tm, :] = hn_ref[0]
        ext_ref[halo + tm:2 * halo + tm, :] = jnp.where(i == nt - 1, jnp.zeros_like(next_ref[0]), next_ref[0])
        acc_ref[...] = jnp.zeros(acc_ref.shape, F32)
        chunk(*chunk_a)
        if not single_first:
            chunk(*chunk_b)

    @pl.when(f > 0)
    def _():
        chunk(*chunk_a)
        chunk(*chunk_b)

    @pl.when(f == nsteps - 1)
    def _():
        acc_ref[...] = h1_ref[0] + _rms(acc_ref[...], og_ref[...])


def _ffn(hn, hnm, h1, w_up, w_down, conv_w, conv_b, out_gain, tm, fc):
    b, n, d = hn.shape
    nt = n // tm
    nf = FFN_DIM // fc
    single_first = nf % 2 == 1
    nsteps = (nf + 1) // 2
    halo = SUBLANES_BF16
    th = tm // halo
    nh = n // halo

    def chunk_specs(which):
        if single_first:
            cid = (lambda f: jnp.maximum(2 * f - 1, 0)) if which == 0 else (lambda f: jnp.maximum(2 * f, 1))
        else:
            cid = lambda f: 2 * f + which
        return [pl.BlockSpec((d, fc), lambda bi, i, f: (0, cid(f))),
                pl.BlockSpec((d, fc), lambda bi, i, f: (0, cid(f) + nf)),
                pl.BlockSpec((fc, d), lambda bi, i, f: (cid(f), 0)),
                pl.BlockSpec((FFN_CONV_W, fc), lambda bi, i, f: (0, cid(f))),
                pl.BlockSpec((1, fc), lambda bi, i, f: (0, cid(f)))]

    weights = (w_up, w_up, w_down, conv_w, conv_b)
    return pl.pallas_call(
        functools.partial(_ffn_kernel, nt=nt, nsteps=nsteps, single_first=single_first),
        grid=(b, nt, nsteps),
        in_specs=[pl.BlockSpec((1, tm, d), lambda bi, i, f: (bi, i, 0)),
                  pl.BlockSpec((1, halo, d), lambda bi, i, f: (bi, jnp.maximum(i * th - 1, 0), 0)),
                  pl.BlockSpec((1, halo, d), lambda bi, i, f: (bi, jnp.minimum((i + 1) * th, nh - 1), 0)),
                  pl.BlockSpec((1, N_META, d), lambda bi, i, f: (bi, 0, 0)),
                  pl.BlockSpec((1, tm, d), lambda bi, i, f: (bi, i, 0))]
                 + chunk_specs(0) + chunk_specs(1) + [_const_spec((1, d))],
        out_specs=pl.BlockSpec((1, tm, d), lambda bi, i, f: (bi, i, 0)),
        out_shape=jax.ShapeDtypeStruct((b, n, d), F32),
        scratch_shapes=[pltpu.VMEM((tm + 2 * halo, d), BF16)],
        compiler_params=pltpu.CompilerParams(
            dimension_semantics=("parallel", "parallel", "arbitrary"),
            vmem_limit_bytes=VMEM_LIMIT_BYTES),
        name="conv_ffn",
    )(hn, hn, hn, hnm, h1, *weights, *weights, out_gain)


def _rope_tables(n):
    rows = n // GRID_W
    t_row = jnp.repeat(jnp.arange(rows), GRID_W).astype(F32)
    t_col = jnp.tile(jnp.arange(GRID_W), rows).astype(F32)
    half = HEAD_DIM // 2
    inv = ROPE_THETA ** (-jnp.arange(0, half, 2, dtype=F32) / half)
    ang = jnp.concatenate([t_row[:, None] * inv, t_col[:, None] * inv], axis=-1)
    cos, sin = jnp.cos(ang), jnp.sin(ang)
    return jnp.concatenate([cos, cos], axis=-1), jnp.concatenate([-sin, sin], axis=-1)


_TILES = dict(tm=512, tq=512, tk=1024, tc=1024, tf=512, fc=512)


def _pick(n, pref):
    t = pref
    while n % t:
        t //= 2
    return t


def _trunk(x, meta, mp, p):
    b, n, d = x.shape
    rows = b * n
    cos_t, sin_t = _rope_tables(n)
    tm = _pick(n, _TILES["tm"])
    nt = n // tm
    q, k, vt, xc, gy, xc_head = _in_proj(x.reshape(rows, d), meta, nt, p["norm_pre_mix"], p["w_in"], p["q_norm"],
                                         p["k_norm"], cos_t, sin_t, p["lru_conv_w"], p["lru_conv_b"], tm)
    head = xc_head.reshape(b, nt, SUBLANES_BF16, LRU_WIDTH)[:, 0, SUBLANES_BF16 - 2:, :]
    xc_m = jnp.concatenate([jnp.broadcast_to(mp["xc"][None, :N_META - 2], (b, N_META - 2, LRU_WIDTH)), head], axis=1)
    tq = _pick(n, _TILES["tq"])
    tk = _pick(n, _TILES["tk"])
    bound = mp["bound"]

    def attend(running_max):
        def run(bound, q, qm, k, vt, km, vmt):
            real = _attention(bound, q, k, vt, km, vmt, b, tq, tk, False, running_max)
            meta_rows = _attention(bound, qm, k, vt, km, vmt, b, LANES, tk, True, running_max)
            return real, meta_rows
        return run

    attn, attn_m = lax.cond(2.0 * bound[0] <= MAX_FIXED_SHIFT_RANGE, attend(False), attend(True),
                            bound, q, mp["q"], k, vt, mp["km"], mp["vmt"])
    attn_m = attn_m[:, :N_META]
    lru, lru_m = _lru(xc.reshape(b, n, LRU_WIDTH), gy.reshape(b, n, LRU_WIDTH), xc_m, mp["gy"],
                      p["lru_w_r"], p["lru_b_r"], p["lru_w_i"], p["lru_b_i"],
                      p["lru_lambda"], p["lru_out_norm"], _pick(n, _TILES["tc"]))
    op = (p["attn_out_norm"], p["w_out_a"], p["w_out_l"], p["norm_post_mix"], p["norm_pre_ffn"])
    h1, hn = _out_proj(attn.reshape(rows, ATTN_WIDTH), lru.reshape(rows, LRU_WIDTH), x.reshape(rows, d), *op, tm)
    res_m = jnp.broadcast_to(meta[None], (b, N_META, d)).reshape(b * N_META, d)
    _, hn_m = _out_proj(attn_m.reshape(b * N_META, ATTN_WIDTH), lru_m.reshape(b * N_META, LRU_WIDTH), res_m,
                        *op, N_META)
    return _ffn(hn.reshape(b, n, d), hn_m.reshape(b, N_META, d), h1.reshape(b, n, d), p["w_up"], p["w_down"],
                p["ffn_conv_w"], p["ffn_conv_b"], p["norm_post_ffn"], _pick(n, _TILES["tf"]), _TILES["fc"])


def _head_perm():
    q4 = HEAD_DIM // 4
    idx = jnp.arange(HEAD_DIM).reshape(2, 2, q4)
    return idx.transpose(1, 0, 2).reshape(HEAD_DIM)


def kernel(x_prompt, x_sample, meta_tokens, norm_pre_mix, w_in, q_norm, k_norm, lru_conv_w, lru_conv_b,
           lru_w_r, lru_b_r, lru_w_i, lru_b_i, lru_lambda, attn_out_norm, lru_out_norm, w_out,
           norm_post_mix, norm_pre_ffn, w_up, ffn_conv_w, ffn_conv_b, w_down, norm_post_ffn):
    perm = _head_perm()
    n_rot = N_Q_HEADS + N_KV_HEADS
    cols = (jnp.arange(n_rot)[:, None] * HEAD_DIM + perm[None, :]).reshape(-1)
    cols = jnp.concatenate([cols, jnp.arange(n_rot * HEAD_DIM, IN_WIDTH)])
    w_out_b = w_out[0].astype(BF16)
    p = {
        "norm_pre_mix": norm_pre_mix[0][None], "w_in": w_in[0][:, cols].astype(BF16),
        "q_norm": q_norm[0][perm][None], "k_norm": k_norm[0][perm][None],
        "lru_conv_w": lru_conv_w[0], "lru_conv_b": lru_conv_b[0][None],
        "lru_w_r": (0.5 * lru_w_r[0]).astype(BF16), "lru_b_r": 0.5 * lru_b_r[0],
        "lru_w_i": (0.5 * lru_w_i[0]).astype(BF16), "lru_b_i": 0.5 * lru_b_i[0], "lru_lambda": lru_lambda[0],
        "attn_out_norm": attn_out_norm[0][None], "lru_out_norm": lru_out_norm[0][None],
        "w_out_a": w_out_b[:ATTN_WIDTH], "w_out_l": w_out_b[ATTN_WIDTH:],
        "norm_post_mix": norm_post_mix[0][None], "norm_pre_ffn": norm_pre_ffn[0][None],
        "w_up": w_up[0].astype(BF16), "ffn_conv_w": ffn_conv_w[0], "ffn_conv_b": ffn_conv_b[0][None],
        "w_down": w_down[0].astype(BF16), "norm_post_ffn": norm_post_ffn[0][None],
    }
    ones = jnp.ones((LANES, HEAD_DIM), F32)
    meta_pad = jnp.pad(meta_tokens, ((0, LANES - N_META), (0, 0)))
    qm, km, vtm, xcm, gym, _ = _in_proj(meta_pad, jnp.zeros((SUBLANES_BF16, D_MODEL), F32), 1, p["norm_pre_mix"],
                                        p["w_in"], p["q_norm"], p["k_norm"], ones, jnp.zeros_like(ones),
                                        p["lru_conv_w"], p["lru_conv_b"], LANES)
    bound = (HEAD_DIM * Q_PRESCALE * BF16_ROUNDING_SLACK) * jnp.max(jnp.abs(q_norm[0])) * jnp.max(jnp.abs(k_norm[0]))
    mp = {"q": qm, "km": km[:, :N_META], "vmt": vtm[:, :N_META], "xc": xcm[:N_META], "gy": gym[:N_META],
          "bound": bound.reshape(1).astype(F32)}
    return _trunk(x_prompt, meta_tokens, mp, p), _trunk(x_sample, meta_tokens, mp, p)
```

```python
import functools

import jax
import jax.numpy as jnp
from jax import lax
from jax.experimental import pallas as pl
from jax.experimental.pallas import tpu as pltpu

D_MODEL = 2048
N_META = 16
GRID_W = 64
HEAD_DIM = 128
N_Q_HEADS = 8
N_KV_HEADS = 2
Q_PER_KV = N_Q_HEADS // N_KV_HEADS
ATTN_WIDTH = N_Q_HEADS * HEAD_DIM
KV_WIDTH = N_KV_HEADS * HEAD_DIM
LRU_WIDTH = D_MODEL - ATTN_WIDTH
LRU_BLOCKS = 8
LRU_BLOCK = LRU_WIDTH // LRU_BLOCKS
LRU_CONV_W = 4
LRU_C = 8.0
LRU_SUB = 64
OUT_PROJ_PARTS = 2
ATTN_KV_UNROLL = 8
IN_WIDTH = ATTN_WIDTH + 2 * KV_WIDTH + 2 * LRU_WIDTH
FFN_DIM = 5632
FFN_CONV_W = 3
ROPE_THETA = 10000.0
EPS = 1e-6
ATTN_SCALE = HEAD_DIM ** -0.5
LOG2E = 1.4426950408889634
LN2 = 0.6931471805599453
Q_PRESCALE = ATTN_SCALE * LOG2E
MAX_FIXED_SHIFT_RANGE = 120.0
BF16_ROUNDING_SLACK = 1.0 + 2.0 ** -6

LANES = 128
SUBLANES_F32 = 8
SUBLANES_BF16 = 16
VMEM_LIMIT_BYTES = 56 * 1024 * 1024

F32 = jnp.float32
BF16 = jnp.bfloat16


def _rms(x, g):
    ms = jnp.mean(x * x, axis=-1, keepdims=True)
    return x * lax.rsqrt(ms + EPS) * g


def _dot(a, b):
    return jnp.dot(a, b, preferred_element_type=F32)


def _const_spec(shape):
    zeros = (0,) * len(shape)
    return pl.BlockSpec(shape, lambda *_: zeros)


def _in_proj_kernel(x_ref, xprev_ref, xnext_ref, xhead_ref, g_ref, w_ref, qg_ref, kg_ref, cos_ref, sin_ref,
                    cw_ref, cb_ref, q_ref, k_ref, vt_ref, xc_ref, gy_ref, xch_ref, *, nt):
    tm = x_ref.shape[0]
    halo = SUBLANES_BF16
    pos = pl.program_id(0) % nt
    xprev = jnp.where(pos == 0, xhead_ref[...], xprev_ref[...])
    xnext = jnp.where(pos == nt - 1, jnp.zeros_like(xnext_ref[...]), xnext_ref[...])
    xn = _rms(x_ref[...], g_ref[...]).astype(BF16)
    xn_ext = jnp.concatenate([_rms(xprev, g_ref[...]).astype(BF16), xn, _rms(xnext, g_ref[...]).astype(BF16)], axis=0)
    cos = cos_ref[...]
    sin = sin_ref[...]

    def rope_head(zh, gain):
        y = _rms(zh, gain)
        return y * cos + pltpu.roll(y, HEAD_DIM // 2, axis=1) * sin

    o3 = ATTN_WIDTH + 2 * KV_WIDTH
    zq = _dot(xn, w_ref[:, 0:ATTN_WIDTH])
    for h in range(N_Q_HEADS):
        sl = slice(h * HEAD_DIM, (h + 1) * HEAD_DIM)
        q_ref[h] = (rope_head(zq[:, sl], qg_ref[...]) * Q_PRESCALE).astype(BF16)
    xr_ext = _dot(xn_ext, w_ref[:, o3:o3 + LRU_WIDTH])
    n_ext = tm + 2 * halo
    xc_ext = jnp.broadcast_to(cb_ref[...], (n_ext, LRU_WIDTH))
    for kk in range(LRU_CONV_W):
        shift = (LRU_CONV_W // 2 - kk) % n_ext
        xk = xr_ext if shift == 0 else pltpu.roll(xr_ext, shift, axis=0)
        xc_ext = xc_ext + xk * cw_ref[kk:kk + 1, :]
    xc_ref[...] = xc_ext[halo:halo + tm]
    xch_ref[...] = xc_ext[0:halo]
    zkv = _dot(xn, w_ref[:, ATTN_WIDTH:ATTN_WIDTH + 2 * KV_WIDTH])
    for h in range(N_KV_HEADS):
        sl = slice(h * HEAD_DIM, (h + 1) * HEAD_DIM)
        k_ref[h] = rope_head(zkv[:, sl], kg_ref[...]).astype(BF16)
    vt_ref[...] = zkv[:, KV_WIDTH:2 * KV_WIDTH].T.astype(BF16)
    gy_ref[...] = _dot(xn, w_ref[:, o3 + LRU_WIDTH:o3 + 2 * LRU_WIDTH])


def _in_proj(x2d, xhead, nt, gain, w_in, q_gain, k_gain, cos_t, sin_t, conv_w, conv_b, tm):
    rows = x2d.shape[0]
    n_tab = cos_t.shape[0] // tm
    halo = SUBLANES_BF16
    th = tm // halo
    nh = rows // halo
    row_spec = lambda w: pl.BlockSpec((tm, w), lambda i: (i, 0))
    tab_spec = pl.BlockSpec((tm, HEAD_DIM), lambda i: (i % n_tab, 0))
    return pl.pallas_call(
        functools.partial(_in_proj_kernel, nt=nt),
        grid=(rows // tm,),
        in_specs=[row_spec(D_MODEL),
                  pl.BlockSpec((halo, D_MODEL), lambda i: (jnp.maximum(i * th - 1, 0), 0)),
                  pl.BlockSpec((halo, D_MODEL), lambda i: (jnp.minimum((i + 1) * th, nh - 1), 0)),
                  _const_spec((halo, D_MODEL)), _const_spec((1, D_MODEL)),
                  pl.BlockSpec((D_MODEL, IN_WIDTH), lambda i: (0, 0), pipeline_mode=pl.Buffered(1)),
                  _const_spec((1, HEAD_DIM)), _const_spec((1, HEAD_DIM)), tab_spec, tab_spec,
                  _const_spec((LRU_CONV_W, LRU_WIDTH)), _const_spec((1, LRU_WIDTH))],
        out_specs=[pl.BlockSpec((N_Q_HEADS, tm, HEAD_DIM), lambda i: (0, i, 0)),
                   pl.BlockSpec((N_KV_HEADS, tm, HEAD_DIM), lambda i: (0, i, 0)),
                   pl.BlockSpec((KV_WIDTH, tm), lambda i: (0, i)),
                   row_spec(LRU_WIDTH), row_spec(LRU_WIDTH),
                   pl.BlockSpec((halo, LRU_WIDTH), lambda i: (i, 0))],
        out_shape=[jax.ShapeDtypeStruct((N_Q_HEADS, rows, HEAD_DIM), BF16),
                   jax.ShapeDtypeStruct((N_KV_HEADS, rows, HEAD_DIM), BF16),
                   jax.ShapeDtypeStruct((KV_WIDTH, rows), BF16),
                   jax.ShapeDtypeStruct((rows, LRU_WIDTH), F32),
                   jax.ShapeDtypeStruct((rows, LRU_WIDTH), F32),
                   jax.ShapeDtypeStruct((rows // tm * halo, LRU_WIDTH), F32)],
        compiler_params=pltpu.CompilerParams(dimension_semantics=("parallel",),
                                             vmem_limit_bytes=VMEM_LIMIT_BYTES),
        name="in_proj",
    )(x2d, x2d, x2d, xhead, gain, w_in, q_gain, k_gain, cos_t, sin_t, conv_w, conv_b)


def _attn_kernel(bound_ref, q_ref, k_ref, vt_ref, km_ref, vmt_ref, o_ref, l_ref, acc_ref, *m_scratch,
                 nkv, tk, running_max):
    tq = q_ref.shape[1]
    width = Q_PER_KV * tq
    qs = q_ref[...].reshape(width, HEAD_DIM)
    nt = (((1,), (1,)), ((), ()))

    def block(kb, vtb, first):
        st = lax.dot_general(kb, qs, nt, preferred_element_type=F32)
        if running_max:
            m_ref, = m_scratch
            m_new = jnp.max(st, axis=0, keepdims=True)
            if not first:
                m_prev = m_ref[...]
                m_new = jnp.maximum(m_prev, m_new)
                alpha = jnp.exp2(m_prev - m_new)
            m_ref[...] = m_new
            p = jnp.exp2(st - m_new)
        else:
            p = jnp.exp2(st - bound_ref[0])
        psum = jnp.sum(p.reshape(p.shape[0] // SUBLANES_F32, SUBLANES_F32, width), axis=0)
        pv = _dot(vtb, p.astype(BF16))
        if first:
            l_ref[...] = psum
            acc_ref[...] = pv
        elif running_max:
            l_ref[...] = alpha * l_ref[...] + psum
            acc_ref[...] = alpha * acc_ref[...] + pv
        else:
            l_ref[...] += psum
            acc_ref[...] += pv

    block(km_ref[0], vmt_ref[...], True)

    def body(j, carry):
        start = pl.multiple_of(j * tk, tk)
        block(k_ref[0, pl.ds(start, tk), :], vt_ref[:, pl.ds(start, tk)], False)
        return carry

    unroll = 1 if running_max else next(u for u in (ATTN_KV_UNROLL, 4, 2, 1) if nkv % u == 0)
    lax.fori_loop(0, nkv, body, 0, unroll=unroll)

    o = acc_ref[...] * (1.0 / jnp.sum(l_ref[...], axis=0, keepdims=True))
    for g in range(Q_PER_KV):
        o_ref[0, :, g * HEAD_DIM:(g + 1) * HEAD_DIM] = o[:, g * tq:(g + 1) * tq].T.astype(BF16)


def _attention(bound, q, k, vt, km, vmt, batch, tq, tk, shared_q, running_max):
    n = k.shape[1] // batch
    nq = q.shape[1] if shared_q else q.shape[1] // batch
    nkv = n // tk
    nqt = nq // tq
    width = Q_PER_KV * tq
    if shared_q:
        q_map = lambda bi, h, i: (h, i, 0)
    else:
        q_map = lambda bi, h, i: (h, bi * nqt + i, 0)
    scratch = [pltpu.VMEM((SUBLANES_F32, width), F32), pltpu.VMEM((HEAD_DIM, width), F32)]
    if running_max:
        scratch.append(pltpu.VMEM((1, width), F32))
    return pl.pallas_call(
        functools.partial(_attn_kernel, nkv=nkv, tk=tk, running_max=running_max),
        grid=(batch, N_KV_HEADS, nqt),
        in_specs=[pl.BlockSpec(memory_space=pltpu.SMEM),
                  pl.BlockSpec((Q_PER_KV, tq, HEAD_DIM), q_map),
                  pl.BlockSpec((1, n, HEAD_DIM), lambda bi, h, i: (h, bi, 0)),
                  pl.BlockSpec((HEAD_DIM, n), lambda bi, h, i: (h, bi)),
                  pl.BlockSpec((1, N_META, HEAD_DIM), lambda bi, h, i: (h, 0, 0)),
                  pl.BlockSpec((HEAD_DIM, N_META), lambda bi, h, i: (h, 0))],
        out_specs=pl.BlockSpec((1, tq, Q_PER_KV * HEAD_DIM), lambda bi, h, i: (bi, i, h)),
        out_shape=jax.ShapeDtypeStruct((batch, nq, ATTN_WIDTH), BF16),
        scratch_shapes=scratch,
        compiler_params=pltpu.CompilerParams(
            dimension_semantics=("parallel", "parallel", "arbitrary"),
            vmem_limit_bytes=VMEM_LIMIT_BYTES),
        name="attention",
    )(bound, q, k, vt, km, vmt)


def _softplus(x):
    return jnp.maximum(x, 0.0) + jnp.log1p(jnp.exp(-jnp.abs(x)))


def _lru_prep(xc, rows, wr_ref, br_ref, wi_ref, bi_ref, lam_ref, a_ref, u_ref):
    xcb = xc.astype(BF16)
    r_parts, i_parts = [], []
    for blk in range(LRU_BLOCKS):
        xb = xcb[:, blk * LRU_BLOCK:(blk + 1) * LRU_BLOCK]
        r_parts.append(_dot(xb, wr_ref[blk]))
        i_parts.append(_dot(xb, wi_ref[blk]))
    tr = jnp.tanh(jnp.concatenate(r_parts, axis=-1) + br_ref[...])
    ti = jnp.tanh(jnp.concatenate(i_parts, axis=-1) + bi_ref[...])
    c1 = (-0.5 * LRU_C * LOG2E) * _softplus(-lam_ref[...])
    log2_a = c1 * tr + c1
    a = jnp.exp2(log2_a)
    a_ref[0:rows, :] = a
    gate2 = jnp.tanh((-LN2) * log2_a) * (1.0 + a * a)
    gate = jnp.where(gate2 > 0.0, gate2 * lax.rsqrt(gate2), 0.0)
    half_xc = 0.5 * xc
    u_ref[0:rows, :] = gate * (half_xc * ti + half_xc)


def _lru_scan(rows, a_ref, u_ref, h, out_ref, base, reverse):
    for r in (range(rows - 1, -1, -1) if reverse else range(rows)):
        h = a_ref[r:r + 1, :] * h + u_ref[r:r + 1, :]
        out_ref[pl.ds(base + r, 1), :] = h
    return h


def _lru_fwd_kernel(xc_ref, xcm_ref, wr_ref, br_ref, wi_ref, bi_ref,
                    lam_ref, hf_ref, hfm_ref, a0_ref, u0_ref, a1_ref, u1_ref, h_ref):
    j = pl.program_id(1)
    tc = xc_ref.shape[1]
    wts = (wr_ref, br_ref, wi_ref, bi_ref, lam_ref)
    nsub = tc // LRU_SUB

    @pl.when(j == 0)
    def _():
        _lru_prep(xcm_ref[0], N_META, *wts, a0_ref, u0_ref)
        h_ref[...] = _lru_scan(N_META, a0_ref, u0_ref, jnp.zeros(h_ref.shape, F32), hfm_ref.at[0], 0, False)

    @pl.when(j > 0)
    def _():
        def prep(s, a_ref, u_ref):
            start = pl.multiple_of(s * LRU_SUB, LRU_SUB)
            _lru_prep(xc_ref[0, pl.ds(start, LRU_SUB), :], LRU_SUB, *wts, a_ref, u_ref)

        prep(0, a0_ref, u0_ref)

        def pair(i, h):
            s = 2 * i
            prep(s + 1, a1_ref, u1_ref)
            h = _lru_scan(LRU_SUB, a0_ref, u0_ref, h, hf_ref.at[0], pl.multiple_of(s * LRU_SUB, LRU_SUB), False)
            prep(jnp.minimum(s + 2, nsub - 1), a0_ref, u0_ref)
            return _lru_scan(LRU_SUB, a1_ref, u1_ref, h, hf_ref.at[0],
                             pl.multiple_of((s + 1) * LRU_SUB, LRU_SUB), False)

        h_ref[...] = lax.fori_loop(0, nsub // 2, pair, h_ref[...])


def _lru_bwd_kernel(xc_ref, xcm_ref, gy_ref, gym_ref, hf_ref, hfm_ref,
                    wr_ref, br_ref, wi_ref, bi_ref, lam_ref, og_ref,
                    out_ref, outm_ref, a0_ref, u0_ref, a1_ref, u1_ref, h_ref, hb0_ref, hb1_ref, *, nc):
    j = pl.program_id(1)
    tc = xc_ref.shape[1]
    wts = (wr_ref, br_ref, wi_ref, bi_ref, lam_ref)
    nsub = tc // LRU_SUB

    def finish(hf, hb, gy):
        return _rms((hf + hb) * jax.nn.gelu(gy), og_ref[...]).astype(BF16)

    @pl.when(j == 0)
    def _():
        h_ref[...] = jnp.zeros(h_ref.shape, F32)

    @pl.when(j < nc)
    def _():
        def prep(s, a_ref, u_ref):
            start = pl.multiple_of(s * LRU_SUB, LRU_SUB)
            _lru_prep(xc_ref[0, pl.ds(start, LRU_SUB), :], LRU_SUB, *wts, a_ref, u_ref)

        def piece(s, a_ref, u_ref, hb_ref, h):
            h = _lru_scan(LRU_SUB, a_ref, u_ref, h, hb_ref, 0, True)
            rows = pl.ds(pl.multiple_of(s * LRU_SUB, LRU_SUB), LRU_SUB)
            out_ref[0, rows, :] = finish(hf_ref[0, rows, :], hb_ref[...], gy_ref[0, rows, :])
            return h

        prep(nsub - 1, a0_ref, u0_ref)

        def pair(i, h):
            s = nsub - 1 - 2 * i
            prep(s - 1, a1_ref, u1_ref)
            h = piece(s, a0_ref, u0_ref, hb0_ref, h)
            prep(jnp.maximum(s - 2, 0), a0_ref, u0_ref)
            return piece(s - 1, a1_ref, u1_ref, hb1_ref, h)

        h_ref[...] = lax.fori_loop(0, nsub // 2, pair, h_ref[...])

    @pl.when(j == nc)
    def _():
        _lru_prep(xcm_ref[0], N_META, *wts, a0_ref, u0_ref)
        h_ref[...] = _lru_scan(N_META, a0_ref, u0_ref, h_ref[...], hb0_ref, 0, True)
        outm_ref[0] = finish(hfm_ref[0], hb0_ref[0:N_META, :], gym_ref[...])


def _lru(xc, gy, xcm, gym, w_r, b_r, w_i, b_i, lam, out_gain, tc):
    b, n, c = xc.shape
    nc = n // tc
    wspec = _const_spec((LRU_BLOCKS, LRU_BLOCK, LRU_BLOCK))
    vspec = _const_spec((1, c))
    pspecs = [wspec, vspec, wspec, vspec, vspec]
    bmspec = pl.BlockSpec((1, N_META, c), lambda bi, j: (bi, 0, 0))
    slots = [pltpu.VMEM((LRU_SUB, c), F32)] * 4
    assert tc % (2 * LRU_SUB) == 0 and LRU_SUB >= N_META

    def chunk_f(j):
        return jnp.maximum(j - 1, 0)

    hf, hfm = pl.pallas_call(
        _lru_fwd_kernel,
        grid=(b, nc + 1),
        in_specs=[pl.BlockSpec((1, tc, c), lambda bi, j: (bi, chunk_f(j), 0)), bmspec] + pspecs,
        out_specs=[pl.BlockSpec((1, tc, c), lambda bi, j: (bi, chunk_f(j), 0)), bmspec],
        out_shape=[jax.ShapeDtypeStruct((b, n, c), F32), jax.ShapeDtypeStruct((b, N_META, c), F32)],
        scratch_shapes=slots + [pltpu.VMEM((1, c), F32)],
        compiler_params=pltpu.CompilerParams(dimension_semantics=("parallel", "arbitrary"),
                                             vmem_limit_bytes=VMEM_LIMIT_BYTES),
        name="lru_fwd",
    )(xc, xcm, w_r[0], b_r[0:1], w_i[0], b_i[0:1], lam[0:1])

    def chunk_b(j):
        return jnp.maximum(nc - 1 - j, 0)

    chunk_spec = pl.BlockSpec((1, tc, c), lambda bi, j: (bi, chunk_b(j), 0))
    out, outm = pl.pallas_call(
        functools.partial(_lru_bwd_kernel, nc=nc),
        grid=(b, nc + 1),
        in_specs=[chunk_spec, bmspec, chunk_spec, _const_spec((N_META, c)), chunk_spec, bmspec] + pspecs + [vspec],
        out_specs=[chunk_spec, bmspec],
        out_shape=[jax.ShapeDtypeStruct((b, n, c), BF16), jax.ShapeDtypeStruct((b, N_META, c), BF16)],
        scratch_shapes=slots + [pltpu.VMEM((1, c), F32),
                                pltpu.VMEM((LRU_SUB, c), F32), pltpu.VMEM((LRU_SUB, c), F32)],
        compiler_params=pltpu.CompilerParams(dimension_semantics=("parallel", "arbitrary"),
                                             vmem_limit_bytes=VMEM_LIMIT_BYTES),
        name="lru_bwd",
    )(xc, xcm, gy, gym, hf, hfm, w_r[1], b_r[1:2], w_i[1], b_i[1:2], lam[1:2], out_gain)
    return out, outm


def _out_proj_kernel(attn_ref, lru_ref, res_ref, ag_ref, wa_ref, wl_ref, pg_ref, fg_ref, h1_ref, hn_ref):
    tm = attn_ref.shape[0]
    parts = OUT_PROJ_PARTS if tm % (OUT_PROJ_PARTS * LANES) == 0 else 1
    part = tm // parts
    for s in range(parts):
        rows = slice(s * part, (s + 1) * part)
        an = _rms(attn_ref[rows, :].astype(F32), ag_ref[...]).astype(BF16)
        mixed = _dot(an, wa_ref[...]) + _dot(lru_ref[rows, :], wl_ref[...])
        h1 = res_ref[rows, :] + _rms(mixed, pg_ref[...])
        h1_ref[rows, :] = h1
        hn_ref[rows, :] = _rms(h1, fg_ref[...]).astype(BF16)


def _out_proj(attn2d, lru2d, res2d, attn_gain, w_a, w_l, post_gain, ffn_gain, tm):
    rows = attn2d.shape[0]
    row_spec = lambda w: pl.BlockSpec((tm, w), lambda i: (i, 0))
    wspec = pl.BlockSpec((ATTN_WIDTH, D_MODEL), lambda i: (0, 0), pipeline_mode=pl.Buffered(1))
    return pl.pallas_call(
        _out_proj_kernel,
        grid=(rows // tm,),
        in_specs=[row_spec(ATTN_WIDTH), row_spec(LRU_WIDTH), row_spec(D_MODEL), _const_spec((1, ATTN_WIDTH)),
                  wspec, wspec, _const_spec((1, D_MODEL)), _const_spec((1, D_MODEL))],
        out_specs=[row_spec(D_MODEL), row_spec(D_MODEL)],
        out_shape=[jax.ShapeDtypeStruct((rows, D_MODEL), F32), jax.ShapeDtypeStruct((rows, D_MODEL), BF16)],
        compiler_params=pltpu.CompilerParams(dimension_semantics=("parallel",),
                                             vmem_limit_bytes=VMEM_LIMIT_BYTES),
        name="out_proj",
    )(attn2d, lru2d, res2d, attn_gain, w_a, w_l, post_gain, ffn_gain)


def _ffn_kernel(hn_ref, prev_ref, next_ref, hm_ref, h1_ref, wg_ref, wv_ref, wd_ref, cw_ref, cb_ref, og_ref,
                out_ref, ext_ref, *, nt, nf):
    i = pl.program_id(1)
    f = pl.program_id(2)
    tm = hn_ref.shape[1]
    halo = SUBLANES_BF16
    acc_ref = out_ref.at[0]

    @pl.when(f == 0)
    def _():
        before = jnp.where(i == 0, hm_ref[0], prev_ref[0]).astype(F32)[halo - 1:halo, :]
        after = jnp.where(i == nt - 1, jnp.zeros_like(next_ref[0]), next_ref[0]).astype(F32)[0:1, :]
        hrow = lax.broadcasted_iota(jnp.int32, (halo, 1), 0)
        group = jnp.where(hrow == 0, before, jnp.where(hrow == 1, after, 0.0))
        ext_ref[0:tm, :] = hn_ref[0]
        ext_ref[tm:tm + halo, :] = group.astype(BF16)
        acc_ref[...] = jnp.zeros(acc_ref.shape, F32)

    gfull = _dot(ext_ref[...], wg_ref[...])
    gmid = gfull[0:tm]
    row = lax.broadcasted_iota(jnp.int32, (tm, 1), 0)
    gprev = jnp.where(row == 0, gfull[tm:tm + 1], pltpu.roll(gmid, 1, axis=0))
    gnext = jnp.where(row == tm - 1, gfull[tm + 1:tm + 2], pltpu.roll(gmid, tm - 1, axis=0))
    g = cb_ref[...] + gprev * cw_ref[0:1, :] + gmid * cw_ref[1:2, :] + gnext * cw_ref[2:3, :]
    val = _dot(ext_ref[0:tm, :], wv_ref[...])
    act = (jax.nn.silu(g) * val).astype(BF16)
    acc_ref[...] += _dot(act, wd_ref[...])

    @pl.when(f == nf - 1)
    def _():
        acc_ref[...] = h1_ref[0] + _rms(acc_ref[...], og_ref[...])


def _ffn(hn, hnm, h1, w_up, w_down, conv_w, conv_b, out_gain, tm, fc):
    b, n, d = hn.shape
    nt = n // tm
    nf = FFN_DIM // fc
    halo = SUBLANES_BF16
    th = tm // halo
    nh = n // halo
    return pl.pallas_call(
        functools.partial(_ffn_kernel, nt=nt, nf=nf),
        grid=(b, nt, nf),
        in_specs=[pl.BlockSpec((1, tm, d), lambda bi, i, f: (bi, i, 0)),
                  pl.BlockSpec((1, halo, d), lambda bi, i, f: (bi, jnp.maximum(i * th - 1, 0), 0)),
                  pl.BlockSpec((1, halo, d), lambda bi, i, f: (bi, jnp.minimum((i + 1) * th, nh - 1), 0)),
                  pl.BlockSpec((1, N_META, d), lambda bi, i, f: (bi, 0, 0)),
                  pl.BlockSpec((1, tm, d), lambda bi, i, f: (bi, i, 0)),
                  pl.BlockSpec((d, fc), lambda bi, i, f: (0, f)),
                  pl.BlockSpec((d, fc), lambda bi, i, f: (0, f + nf)),
                  pl.BlockSpec((fc, d), lambda bi, i, f: (f, 0)),
                  pl.BlockSpec((FFN_CONV_W, fc), lambda bi, i, f: (0, f)),
                  pl.BlockSpec((1, fc), lambda bi, i, f: (0, f)),
                  _const_spec((1, d))],
        out_specs=pl.BlockSpec((1, tm, d), lambda bi, i, f: (bi, i, 0)),
        out_shape=jax.ShapeDtypeStruct((b, n, d), F32),
        scratch_shapes=[pltpu.VMEM((tm + halo, d), BF16)],
        compiler_params=pltpu.CompilerParams(
            dimension_semantics=("parallel", "parallel", "arbitrary"),
            vmem_limit_bytes=VMEM_LIMIT_BYTES),
        name="conv_ffn",
    )(hn, hn, hn, hnm, h1, w_up, w_up, w_down, conv_w, conv_b, out_gain)


def _rope_tables(n):
    rows = n // GRID_W
    t_row = jnp.repeat(jnp.arange(rows), GRID_W).astype(F32)
    t_col = jnp.tile(jnp.arange(GRID_W), rows).astype(F32)
    half = HEAD_DIM // 2
    inv = ROPE_THETA ** (-jnp.arange(0, half, 2, dtype=F32) / half)
    ang = jnp.concatenate([t_row[:, None] * inv, t_col[:, None] * inv], axis=-1)
    cos, sin = jnp.cos(ang), jnp.sin(ang)
    return jnp.concatenate([cos, cos], axis=-1), jnp.concatenate([-sin, sin], axis=-1)


_TILES = dict(tm=512, tq=512, tk=1024, tc=1024, tf=512, fc=512)


def _pick(n, pref):
    t = pref
    while n % t:
        t //= 2
    return t


def _trunk(x, meta, mp, p):
    b, n, d = x.shape
    rows = b * n
    cos_t, sin_t = _rope_tables(n)
    tm = _pick(n, _TILES["tm"])
    nt = n // tm
    q, k, vt, xc, gy, xc_head = _in_proj(x.reshape(rows, d), meta, nt, p["norm_pre_mix"], p["w_in"], p["q_norm"],
                                         p["k_norm"], cos_t, sin_t, p["lru_conv_w"], p["lru_conv_b"], tm)
    head = xc_head.reshape(b, nt, SUBLANES_BF16, LRU_WIDTH)[:, 0, SUBLANES_BF16 - 2:, :]
    xc_m = jnp.concatenate([jnp.broadcast_to(mp["xc"][None, :N_META - 2], (b, N_META - 2, LRU_WIDTH)), head], axis=1)
    tq = _pick(n, _TILES["tq"])
    tk = _pick(n, _TILES["tk"])
    bound = mp["bound"]

    def attend(running_max):
        def run(bound, q, qm, k, vt, km, vmt):
            real = _attention(bound, q, k, vt, km, vmt, b, tq, tk, False, running_max)
            meta_rows = _attention(bound, qm, k, vt, km, vmt, b, LANES, tk, True, running_max)
            return real, meta_rows
        return run

    attn, attn_m = lax.cond(2.0 * bound[0] <= MAX_FIXED_SHIFT_RANGE, attend(False), attend(True),
                            bound, q, mp["q"], k, vt, mp["km"], mp["vmt"])
    attn_m = attn_m[:, :N_META]
    lru, lru_m = _lru(xc.reshape(b, n, LRU_WIDTH), gy.reshape(b, n, LRU_WIDTH), xc_m, mp["gy"],
                      p["lru_w_r"], p["lru_b_r"], p["lru_w_i"], p["lru_b_i"],
                      p["lru_lambda"], p["lru_out_norm"], _pick(n, _TILES["tc"]))
    op = (p["attn_out_norm"], p["w_out_a"], p["w_out_l"], p["norm_post_mix"], p["norm_pre_ffn"])
    h1, hn = _out_proj(attn.reshape(rows, ATTN_WIDTH), lru.reshape(rows, LRU_WIDTH), x.reshape(rows, d), *op, tm)
    res_m = jnp.broadcast_to(meta[None], (b, N_META, d)).reshape(b * N_META, d)
    _, hn_m = _out_proj(attn_m.reshape(b * N_META, ATTN_WIDTH), lru_m.reshape(b * N_META, LRU_WIDTH), res_m,
                        *op, N_META)
    return _ffn(hn.reshape(b, n, d), hn_m.reshape(b, N_META, d), h1.reshape(b, n, d), p["w_up"], p["w_down"],
                p["ffn_conv_w"], p["ffn_conv_b"], p["norm_post_ffn"], _pick(n, _TILES["tf"]), _TILES["fc"])


def _head_perm():
    q4 = HEAD_DIM // 4
    idx = jnp.arange(HEAD_DIM).reshape(2, 2, q4)
    return idx.transpose(1, 0, 2).reshape(HEAD_DIM)


def kernel(x_prompt, x_sample, meta_tokens, norm_pre_mix, w_in, q_norm, k_norm, lru_conv_w, lru_conv_b,
           lru_w_r, lru_b_r, lru_w_i, lru_b_i, lru_lambda, attn_out_norm, lru_out_norm, w_out,
           norm_post_mix, norm_pre_ffn, w_up, ffn_conv_w, ffn_conv_b, w_down, norm_post_ffn):
    perm = _head_perm()
    n_rot = N_Q_HEADS + N_KV_HEADS
    cols = (jnp.arange(n_rot)[:, None] * HEAD_DIM + perm[None, :]).reshape(-1)
    cols = jnp.concatenate([cols, jnp.arange(n_rot * HEAD_DIM, IN_WIDTH)])
    w_out_b = w_out[0].astype(BF16)
    p = {
        "norm_pre_mix": norm_pre_mix[0][None], "w_in": w_in[0][:, cols].astype(BF16),
        "q_norm": q_norm[0][perm][None], "k_norm": k_norm[0][perm][None],
        "lru_conv_w": lru_conv_w[0], "lru_conv_b": lru_conv_b[0][None],
        "lru_w_r": (0.5 * lru_w_r[0]).astype(BF16), "lru_b_r": 0.5 * lru_b_r[0],
        "lru_w_i": (0.5 * lru_w_i[0]).astype(BF16), "lru_b_i": 0.5 * lru_b_i[0], "lru_lambda": lru_lambda[0],
        "attn_out_norm": attn_out_norm[0][None], "lru_out_norm": lru_out_norm[0][None],
        "w_out_a": w_out_b[:ATTN_WIDTH], "w_out_l": w_out_b[ATTN_WIDTH:],
        "norm_post_mix": norm_post_mix[0][None], "norm_pre_ffn": norm_pre_ffn[0][None],
        "w_up": w_up[0].astype(BF16), "ffn_conv_w": ffn_conv_w[0], "ffn_conv_b": ffn_conv_b[0][None],
        "w_down": w_down[0].astype(BF16), "norm_post_ffn": norm_post_ffn[0][None],
    }
    ones = jnp.ones((LANES, HEAD_DIM), F32)
    meta_pad = jnp.pad(meta_tokens, ((0, LANES - N_META), (0, 0)))
    qm, km, vtm, xcm, gym, _ = _in_proj(meta_pad, jnp.zeros((SUBLANES_BF16, D_MODEL), F32), 1, p["norm_pre_mix"],
                                        p["w_in"], p["q_norm"], p["k_norm"], ones, jnp.zeros_like(ones),
                                        p["lru_conv_w"], p["lru_conv_b"], LANES)
    bound = (HEAD_DIM * Q_PRESCALE * BF16_ROUNDING_SLACK) * jnp.max(jnp.abs(q_norm[0])) * jnp.max(jnp.abs(k_norm[0]))
    mp = {"q": qm, "km": km[:, :N_META], "vmt": vtm[:, :N_META], "xc": xcm[:N_META], "gy": gym[:N_META],
          "bound": bound.reshape(1).astype(F32)}
    return _trunk(x_prompt, meta_tokens, mp, p), _trunk(x_sample, meta_tokens, mp, p)
```

```python
import functools

import jax
import jax.numpy as jnp
from jax import lax
from jax.experimental import pallas as pl
from jax.experimental.pallas import tpu as pltpu

D_MODEL = 2048
N_META = 16
GRID_W = 64
HEAD_DIM = 128
N_Q_HEADS = 8
N_KV_HEADS = 2
Q_PER_KV = N_Q_HEADS // N_KV_HEADS
ATTN_WIDTH = N_Q_HEADS * HEAD_DIM
KV_WIDTH = N_KV_HEADS * HEAD_DIM
LRU_WIDTH = D_MODEL - ATTN_WIDTH
LRU_BLOCKS = 8
LRU_BLOCK = LRU_WIDTH // LRU_BLOCKS
LRU_CONV_W = 4
LRU_C = 8.0
LRU_SUB = 64
OUT_PROJ_PARTS = 2
ATTN_KV_UNROLL = 8
FFN_DOWN_COLS = 256
IN_WIDTH = ATTN_WIDTH + 2 * KV_WIDTH + 2 * LRU_WIDTH
FFN_DIM = 5632
FFN_CONV_W = 3
ROPE_THETA = 10000.0
EPS = 1e-6
ATTN_SCALE = HEAD_DIM ** -0.5
LOG2E = 1.4426950408889634
LN2 = 0.6931471805599453
Q_PRESCALE = ATTN_SCALE * LOG2E
MAX_FIXED_SHIFT_RANGE = 120.0
BF16_ROUNDING_SLACK = 1.0 + 2.0 ** -6

LANES = 128
SUBLANES_F32 = 8
SUBLANES_BF16 = 16
VMEM_LIMIT_BYTES = 56 * 1024 * 1024

F32 = jnp.float32
BF16 = jnp.bfloat16


def _rms(x, g):
    ms = jnp.mean(x * x, axis=-1, keepdims=True)
    return x * lax.rsqrt(ms + EPS) * g


def _dot(a, b):
    return jnp.dot(a, b, preferred_element_type=F32)


def _const_spec(shape):
    zeros = (0,) * len(shape)
    return pl.BlockSpec(shape, lambda *_: zeros)


def _in_proj_kernel(x_ref, xprev_ref, xnext_ref, xhead_ref, g_ref, w_ref, qg_ref, kg_ref, cos_ref, sin_ref,
                    cw_ref, cb_ref, q_ref, k_ref, vt_ref, xc_ref, gy_ref, xch_ref, *, nt):
    tm = x_ref.shape[0]
    halo = SUBLANES_BF16
    pos = pl.program_id(0) % nt
    xprev = jnp.where(pos == 0, xhead_ref[...], xprev_ref[...])
    xnext = jnp.where(pos == nt - 1, jnp.zeros_like(xnext_ref[...]), xnext_ref[...])
    xn = _rms(x_ref[...], g_ref[...]).astype(BF16)
    xn_ext = jnp.concatenate([_rms(xprev, g_ref[...]).astype(BF16), xn, _rms(xnext, g_ref[...]).astype(BF16)], axis=0)
    cos = cos_ref[...]
    sin = sin_ref[...]

    def rope_head(zh, gain):
        y = _rms(zh, gain)
        return y * cos + pltpu.roll(y, HEAD_DIM // 2, axis=1) * sin

    o3 = ATTN_WIDTH + 2 * KV_WIDTH
    zq = _dot(xn, w_ref[:, 0:ATTN_WIDTH])
    for h in range(N_Q_HEADS):
        sl = slice(h * HEAD_DIM, (h + 1) * HEAD_DIM)
        q_ref[h] = (rope_head(zq[:, sl], qg_ref[...]) * Q_PRESCALE).astype(BF16)
    xr_ext = _dot(xn_ext, w_ref[:, o3:o3 + LRU_WIDTH])
    n_ext = tm + 2 * halo
    xc_ext = jnp.broadcast_to(cb_ref[...], (n_ext, LRU_WIDTH))
    for kk in range(LRU_CONV_W):
        shift = (LRU_CONV_W // 2 - kk) % n_ext
        xk = xr_ext if shift == 0 else pltpu.roll(xr_ext, shift, axis=0)
        xc_ext = xc_ext + xk * cw_ref[kk:kk + 1, :]
    xc_ref[...] = xc_ext[halo:halo + tm]
    xch_ref[...] = xc_ext[0:halo]
    zkv = _dot(xn, w_ref[:, ATTN_WIDTH:ATTN_WIDTH + 2 * KV_WIDTH])
    for h in range(N_KV_HEADS):
        sl = slice(h * HEAD_DIM, (h + 1) * HEAD_DIM)
        k_ref[h] = rope_head(zkv[:, sl], kg_ref[...]).astype(BF16)
    vt_ref[...] = zkv[:, KV_WIDTH:2 * KV_WIDTH].T.astype(BF16)
    gy_ref[...] = _dot(xn, w_ref[:, o3 + LRU_WIDTH:o3 + 2 * LRU_WIDTH])


def _in_proj(x2d, xhead, nt, gain, w_in, q_gain, k_gain, cos_t, sin_t, conv_w, conv_b, tm):
    rows = x2d.shape[0]
    n_tab = cos_t.shape[0] // tm
    halo = SUBLANES_BF16
    th = tm // halo
    nh = rows // halo
    row_spec = lambda w: pl.BlockSpec((tm, w), lambda i: (i, 0))
    tab_spec = pl.BlockSpec((tm, HEAD_DIM), lambda i: (i % n_tab, 0))
    return pl.pallas_call(
        functools.partial(_in_proj_kernel, nt=nt),
        grid=(rows // tm,),
        in_specs=[row_spec(D_MODEL),
                  pl.BlockSpec((halo, D_MODEL), lambda i: (jnp.maximum(i * th - 1, 0), 0)),
                  pl.BlockSpec((halo, D_MODEL), lambda i: (jnp.minimum((i + 1) * th, nh - 1), 0)),
                  _const_spec((halo, D_MODEL)), _const_spec((1, D_MODEL)),
                  pl.BlockSpec((D_MODEL, IN_WIDTH), lambda i: (0, 0), pipeline_mode=pl.Buffered(1)),
                  _const_spec((1, HEAD_DIM)), _const_spec((1, HEAD_DIM)), tab_spec, tab_spec,
                  _const_spec((LRU_CONV_W, LRU_WIDTH)), _const_spec((1, LRU_WIDTH))],
        out_specs=[pl.BlockSpec((N_Q_HEADS, tm, HEAD_DIM), lambda i: (0, i, 0)),
                   pl.BlockSpec((N_KV_HEADS, tm, HEAD_DIM), lambda i: (0, i, 0)),
                   pl.BlockSpec((KV_WIDTH, tm), lambda i: (0, i)),
                   row_spec(LRU_WIDTH), row_spec(LRU_WIDTH),
                   pl.BlockSpec((halo, LRU_WIDTH), lambda i: (i, 0))],
        out_shape=[jax.ShapeDtypeStruct((N_Q_HEADS, rows, HEAD_DIM), BF16),
                   jax.ShapeDtypeStruct((N_KV_HEADS, rows, HEAD_DIM), BF16),
                   jax.ShapeDtypeStruct((KV_WIDTH, rows), BF16),
                   jax.ShapeDtypeStruct((rows, LRU_WIDTH), F32),
                   jax.ShapeDtypeStruct((rows, LRU_WIDTH), F32),
                   jax.ShapeDtypeStruct((rows // tm * halo, LRU_WIDTH), F32)],
        compiler_params=pltpu.CompilerParams(dimension_semantics=("parallel",),
                                             vmem_limit_bytes=VMEM_LIMIT_BYTES),
        name="in_proj",
    )(x2d, x2d, x2d, xhead, gain, w_in, q_gain, k_gain, cos_t, sin_t, conv_w, conv_b)


def _attn_kernel(bound_ref, q_ref, k_ref, vt_ref, km_ref, vmt_ref, o_ref, l_ref, acc_ref, *m_scratch,
                 nkv, tk, running_max):
    tq = q_ref.shape[1]
    width = Q_PER_KV * tq
    qs = q_ref[...].reshape(width, HEAD_DIM)
    nt = (((1,), (1,)), ((), ()))

    def block(kb, vtb, first):
        st = lax.dot_general(kb, qs, nt, preferred_element_type=F32)
        if running_max:
            m_ref, = m_scratch
            m_new = jnp.max(st, axis=0, keepdims=True)
            if not first:
                m_prev = m_ref[...]
                m_new = jnp.maximum(m_prev, m_new)
                alpha = jnp.exp2(m_prev - m_new)
            m_ref[...] = m_new
            p = jnp.exp2(st - m_new)
        else:
            p = jnp.exp2(st - bound_ref[0])
        psum = jnp.sum(p.reshape(p.shape[0] // SUBLANES_F32, SUBLANES_F32, width), axis=0)
        pv = _dot(vtb, p.astype(BF16))
        if first:
            l_ref[...] = psum
            acc_ref[...] = pv
        elif running_max:
            l_ref[...] = alpha * l_ref[...] + psum
            acc_ref[...] = alpha * acc_ref[...] + pv
        else:
            l_ref[...] += psum
            acc_ref[...] += pv

    block(km_ref[0], vmt_ref[...], True)

    def body(j, carry):
        start = pl.multiple_of(j * tk, tk)
        block(k_ref[0, pl.ds(start, tk), :], vt_ref[:, pl.ds(start, tk)], False)
        return carry

    unroll = 1 if running_max else next(u for u in (ATTN_KV_UNROLL, 4, 2, 1) if nkv % u == 0)
    lax.fori_loop(0, nkv, body, 0, unroll=unroll)

    o = acc_ref[...] * (1.0 / jnp.sum(l_ref[...], axis=0, keepdims=True))
    for g in range(Q_PER_KV):
        o_ref[0, :, g * HEAD_DIM:(g + 1) * HEAD_DIM] = o[:, g * tq:(g + 1) * tq].T.astype(BF16)


def _attention(bound, q, k, vt, km, vmt, batch, tq, tk, shared_q, running_max):
    n = k.shape[1] // batch
    nq = q.shape[1] if shared_q else q.shape[1] // batch
    nkv = n // tk
    nqt = nq // tq
    width = Q_PER_KV * tq
    if shared_q:
        q_map = lambda bi, h, i: (h, i, 0)
    else:
        q_map = lambda bi, h, i: (h, bi * nqt + i, 0)
    scratch = [pltpu.VMEM((SUBLANES_F32, width), F32), pltpu.VMEM((HEAD_DIM, width), F32)]
    if running_max:
        scratch.append(pltpu.VMEM((1, width), F32))
    return pl.pallas_call(
        functools.partial(_attn_kernel, nkv=nkv, tk=tk, running_max=running_max),
        grid=(batch, N_KV_HEADS, nqt),
        in_specs=[pl.BlockSpec(memory_space=pltpu.SMEM),
                  pl.BlockSpec((Q_PER_KV, tq, HEAD_DIM), q_map),
                  pl.BlockSpec((1, n, HEAD_DIM), lambda bi, h, i: (h, bi, 0)),
                  pl.BlockSpec((HEAD_DIM, n), lambda bi, h, i: (h, bi)),
                  pl.BlockSpec((1, N_META, HEAD_DIM), lambda bi, h, i: (h, 0, 0)),
                  pl.BlockSpec((HEAD_DIM, N_META), lambda bi, h, i: (h, 0))],
        out_specs=pl.BlockSpec((1, tq, Q_PER_KV * HEAD_DIM), lambda bi, h, i: (bi, i, h)),
        out_shape=jax.ShapeDtypeStruct((batch, nq, ATTN_WIDTH), BF16),
        scratch_shapes=scratch,
        compiler_params=pltpu.CompilerParams(
            dimension_semantics=("parallel", "parallel", "arbitrary"),
            vmem_limit_bytes=VMEM_LIMIT_BYTES),
        name="attention",
    )(bound, q, k, vt, km, vmt)


def _softplus(x):
    return jnp.maximum(x, 0.0) + jnp.log1p(jnp.exp(-jnp.abs(x)))


def _lru_prep(xc, rows, wr_ref, br_ref, wi_ref, bi_ref, lam_ref, a_ref, u_ref):
    xcb = xc.astype(BF16)
    r_parts, i_parts = [], []
    for blk in range(LRU_BLOCKS):
        xb = xcb[:, blk * LRU_BLOCK:(blk + 1) * LRU_BLOCK]
        r_parts.append(_dot(xb, wr_ref[blk]))
        i_parts.append(_dot(xb, wi_ref[blk]))
    tr = jnp.tanh(jnp.concatenate(r_parts, axis=-1) + br_ref[...])
    ti = jnp.tanh(jnp.concatenate(i_parts, axis=-1) + bi_ref[...])
    c1 = (-0.5 * LRU_C * LOG2E) * _softplus(-lam_ref[...])
    log2_a = c1 * tr + c1
    a = jnp.exp2(log2_a)
    a_ref[0:rows, :] = a
    gate2 = jnp.tanh((-LN2) * log2_a) * (1.0 + a * a)
    gate = jnp.where(gate2 > 0.0, gate2 * lax.rsqrt(gate2), 0.0)
    half_xc = 0.5 * xc
    u_ref[0:rows, :] = gate * (half_xc * ti + half_xc)


def _lru_scan(rows, a_ref, u_ref, h, out_ref, base, reverse):
    for r in (range(rows - 1, -1, -1) if reverse else range(rows)):
        h = a_ref[r:r + 1, :] * h + u_ref[r:r + 1, :]
        out_ref[pl.ds(base + r, 1), :] = h
    return h


def _lru_fwd_kernel(xc_ref, xcm_ref, wr_ref, br_ref, wi_ref, bi_ref,
                    lam_ref, hf_ref, hfm_ref, a0_ref, u0_ref, a1_ref, u1_ref, h_ref):
    j = pl.program_id(1)
    tc = xc_ref.shape[1]
    wts = (wr_ref, br_ref, wi_ref, bi_ref, lam_ref)
    nsub = tc // LRU_SUB

    @pl.when(j == 0)
    def _():
        _lru_prep(xcm_ref[0], N_META, *wts, a0_ref, u0_ref)
        h_ref[...] = _lru_scan(N_META, a0_ref, u0_ref, jnp.zeros(h_ref.shape, F32), hfm_ref.at[0], 0, False)

    @pl.when(j > 0)
    def _():
        def prep(s, a_ref, u_ref):
            start = pl.multiple_of(s * LRU_SUB, LRU_SUB)
            _lru_prep(xc_ref[0, pl.ds(start, LRU_SUB), :], LRU_SUB, *wts, a_ref, u_ref)

        prep(0, a0_ref, u0_ref)

        def pair(i, h):
            s = 2 * i
            prep(s + 1, a1_ref, u1_ref)
            h = _lru_scan(LRU_SUB, a0_ref, u0_ref, h, hf_ref.at[0], pl.multiple_of(s * LRU_SUB, LRU_SUB), False)
            prep(jnp.minimum(s + 2, nsub - 1), a0_ref, u0_ref)
            return _lru_scan(LRU_SUB, a1_ref, u1_ref, h, hf_ref.at[0],
                             pl.multiple_of((s + 1) * LRU_SUB, LRU_SUB), False)

        h_ref[...] = lax.fori_loop(0, nsub // 2, pair, h_ref[...])


def _lru_bwd_kernel(xc_ref, xcm_ref, gy_ref, gym_ref, hf_ref, hfm_ref,
                    wr_ref, br_ref, wi_ref, bi_ref, lam_ref, og_ref,
                    out_ref, outm_ref, a0_ref, u0_ref, a1_ref, u1_ref, h_ref, hb0_ref, hb1_ref, *, nc):
    j = pl.program_id(1)
    tc = xc_ref.shape[1]
    wts = (wr_ref, br_ref, wi_ref, bi_ref, lam_ref)
    nsub = tc // LRU_SUB

    def finish(hf, hb, gy):
        return _rms((hf + hb) * jax.nn.gelu(gy), og_ref[...]).astype(BF16)

    @pl.when(j == 0)
    def _():
        h_ref[...] = jnp.zeros(h_ref.shape, F32)

    @pl.when(j < nc)
    def _():
        def prep(s, a_ref, u_ref):
            start = pl.multiple_of(s * LRU_SUB, LRU_SUB)
            _lru_prep(xc_ref[0, pl.ds(start, LRU_SUB), :], LRU_SUB, *wts, a_ref, u_ref)

        def piece(s, a_ref, u_ref, hb_ref, h):
            h = _lru_scan(LRU_SUB, a_ref, u_ref, h, hb_ref, 0, True)
            rows = pl.ds(pl.multiple_of(s * LRU_SUB, LRU_SUB), LRU_SUB)
            out_ref[0, rows, :] = finish(hf_ref[0, rows, :], hb_ref[...], gy_ref[0, rows, :])
            return h

        prep(nsub - 1, a0_ref, u0_ref)

        def pair(i, h):
            s = nsub - 1 - 2 * i
            prep(s - 1, a1_ref, u1_ref)
            h = piece(s, a0_ref, u0_ref, hb0_ref, h)
            prep(jnp.maximum(s - 2, 0), a0_ref, u0_ref)
            return piece(s - 1, a1_ref, u1_ref, hb1_ref, h)

        h_ref[...] = lax.fori_loop(0, nsub // 2, pair, h_ref[...])

    @pl.when(j == nc)
    def _():
        _lru_prep(xcm_ref[0], N_META, *wts, a0_ref, u0_ref)
        h_ref[...] = _lru_scan(N_META, a0_ref, u0_ref, h_ref[...], hb0_ref, 0, True)
        outm_ref[0] = finish(hfm_ref[0], hb0_ref[0:N_META, :], gym_ref[...])


def _lru(xc, gy, xcm, gym, w_r, b_r, w_i, b_i, lam, out_gain, tc):
    b, n, c = xc.shape
    nc = n // tc
    wspec = _const_spec((LRU_BLOCKS, LRU_BLOCK, LRU_BLOCK))
    vspec = _const_spec((1, c))
    pspecs = [wspec, vspec, wspec, vspec, vspec]
    bmspec = pl.BlockSpec((1, N_META, c), lambda bi, j: (bi, 0, 0))
    slots = [pltpu.VMEM((LRU_SUB, c), F32)] * 4
    assert tc % (2 * LRU_SUB) == 0 and LRU_SUB >= N_META

    def chunk_f(j):
        return jnp.maximum(j - 1, 0)

    hf, hfm = pl.pallas_call(
        _lru_fwd_kernel,
        grid=(b, nc + 1),
        in_specs=[pl.BlockSpec((1, tc, c), lambda bi, j: (bi, chunk_f(j), 0)), bmspec] + pspecs,
        out_specs=[pl.BlockSpec((1, tc, c), lambda bi, j: (bi, chunk_f(j), 0)), bmspec],
        out_shape=[jax.ShapeDtypeStruct((b, n, c), F32), jax.ShapeDtypeStruct((b, N_META, c), F32)],
        scratch_shapes=slots + [pltpu.VMEM((1, c), F32)],
        compiler_params=pltpu.CompilerParams(dimension_semantics=("parallel", "arbitrary"),
                                             vmem_limit_bytes=VMEM_LIMIT_BYTES),
        name="lru_fwd",
    )(xc, xcm, w_r[0], b_r[0:1], w_i[0], b_i[0:1], lam[0:1])

    def chunk_b(j):
        return jnp.maximum(nc - 1 - j, 0)

    chunk_spec = pl.BlockSpec((1, tc, c), lambda bi, j: (bi, chunk_b(j), 0))
    out, outm = pl.pallas_call(
        functools.partial(_lru_bwd_kernel, nc=nc),
        grid=(b, nc + 1),
        in_specs=[chunk_spec, bmspec, chunk_spec, _const_spec((N_META, c)), chunk_spec, bmspec] + pspecs + [vspec],
        out_specs=[chunk_spec, bmspec],
        out_shape=[jax.ShapeDtypeStruct((b, n, c), BF16), jax.ShapeDtypeStruct((b, N_META, c), BF16)],
        scratch_shapes=slots + [pltpu.VMEM((1, c), F32),
                                pltpu.VMEM((LRU_SUB, c), F32), pltpu.VMEM((LRU_SUB, c), F32)],
        compiler_params=pltpu.CompilerParams(dimension_semantics=("parallel", "arbitrary"),
                                             vmem_limit_bytes=VMEM_LIMIT_BYTES),
        name="lru_bwd",
    )(xc, xcm, gy, gym, hf, hfm, w_r[1], b_r[1:2], w_i[1], b_i[1:2], lam[1:2], out_gain)
    return out, outm


def _out_proj_kernel(attn_ref, lru_ref, res_ref, ag_ref, wa_ref, wl_ref, pg_ref, fg_ref, h1_ref, hn_ref):
    tm = attn_ref.shape[0]
    parts = OUT_PROJ_PARTS if tm % (OUT_PROJ_PARTS * LANES) == 0 else 1
    part = tm // parts
    for s in range(parts):
        rows = slice(s * part, (s + 1) * part)
        an = _rms(attn_ref[rows, :].astype(F32), ag_ref[...]).astype(BF16)
        mixed = _dot(an, wa_ref[...]) + _dot(lru_ref[rows, :], wl_ref[...])
        h1 = res_ref[rows, :] + _rms(mixed, pg_ref[...])
        h1_ref[rows, :] = h1
        hn_ref[rows, :] = _rms(h1, fg_ref[...]).astype(BF16)


def _out_proj(attn2d, lru2d, res2d, attn_gain, w_a, w_l, post_gain, ffn_gain, tm):
    rows = attn2d.shape[0]
    row_spec = lambda w: pl.BlockSpec((tm, w), lambda i: (i, 0))
    wspec = pl.BlockSpec((ATTN_WIDTH, D_MODEL), lambda i: (0, 0), pipeline_mode=pl.Buffered(1))
    return pl.pallas_call(
        _out_proj_kernel,
        grid=(rows // tm,),
        in_specs=[row_spec(ATTN_WIDTH), row_spec(LRU_WIDTH), row_spec(D_MODEL), _const_spec((1, ATTN_WIDTH)),
                  wspec, wspec, _const_spec((1, D_MODEL)), _const_spec((1, D_MODEL))],
        out_specs=[row_spec(D_MODEL), row_spec(D_MODEL)],
        out_shape=[jax.ShapeDtypeStruct((rows, D_MODEL), F32), jax.ShapeDtypeStruct((rows, D_MODEL), BF16)],
        compiler_params=pltpu.CompilerParams(dimension_semantics=("parallel",),
                                             vmem_limit_bytes=VMEM_LIMIT_BYTES),
        name="out_proj",
    )(attn2d, lru2d, res2d, attn_gain, w_a, w_l, post_gain, ffn_gain)


def _ffn_kernel(hn_ref, prev_ref, next_ref, hm_ref, h1_ref, wg_ref, wv_ref, wd_ref, cw_ref, cb_ref, og_ref,
                out_ref, ext_ref, act_ref, *, nt, nf, nd):
    i = pl.program_id(1)
    f = pl.program_id(2)
    tm = hn_ref.shape[1]
    fc = wg_ref.shape[1]
    dc = wd_ref.shape[1]
    halo = SUBLANES_BF16

    @pl.when(f == 0)
    def _():
        ext_ref[0:halo, :] = jnp.where(i == 0, hm_ref[0], prev_ref[0])
        ext_ref[halo:halo + tm, :] = hn_ref[0]
        ext_ref[halo + tm:2 * halo + tm, :] = jnp.where(i == nt - 1, jnp.zeros_like(next_ref[0]), next_ref[0])

    @pl.when(f < nf)
    def _():
        n_ext = tm + 2 * halo
        gfull = _dot(ext_ref[...], wg_ref[...])
        g = jnp.broadcast_to(cb_ref[...], (tm, fc))
        for kk in range(FFN_CONV_W):
            shift = (FFN_CONV_W // 2 - kk) % n_ext
            gk = gfull if shift == 0 else pltpu.roll(gfull, shift, axis=0)
            g = g + gk[halo:halo + tm] * cw_ref[kk:kk + 1, :]
        val = _dot(ext_ref[halo:halo + tm, :], wv_ref[...])
        act_ref[:, pl.ds(pl.multiple_of(f * fc, fc), fc)] = (jax.nn.silu(g) * val).astype(BF16)

    @pl.when(f >= nf)
    def _():
        cols = pl.ds(pl.multiple_of((f - nf) * dc, dc), dc)
        out_ref[0, :, cols] = _dot(act_ref[...], wd_ref[...])

    @pl.when(f == nf + nd - 1)
    def _():
        out_ref[0] = h1_ref[0] + _rms(out_ref[0], og_ref[...])


def _ffn(hn, hnm, h1, w_up, w_down, conv_w, conv_b, out_gain, tm, fc):
    b, n, d = hn.shape
    nt = n // tm
    nf = FFN_DIM // fc
    dc = FFN_DOWN_COLS if d % FFN_DOWN_COLS == 0 else d
    nd = d // dc
    halo = SUBLANES_BF16
    th = tm // halo
    nh = n // halo
    up = lambda f: jnp.minimum(f, nf - 1)
    return pl.pallas_call(
        functools.partial(_ffn_kernel, nt=nt, nf=nf, nd=nd),
        grid=(b, nt, nf + nd),
        in_specs=[pl.BlockSpec((1, tm, d), lambda bi, i, f: (bi, i, 0)),
                  pl.BlockSpec((1, halo, d), lambda bi, i, f: (bi, jnp.maximum(i * th - 1, 0), 0)),
                  pl.BlockSpec((1, halo, d), lambda bi, i, f: (bi, jnp.minimum((i + 1) * th, nh - 1), 0)),
                  pl.BlockSpec((1, N_META, d), lambda bi, i, f: (bi, 0, 0)),
                  pl.BlockSpec((1, tm, d), lambda bi, i, f: (bi, i, 0)),
                  pl.BlockSpec((d, fc), lambda bi, i, f: (0, up(f))),
                  pl.BlockSpec((d, fc), lambda bi, i, f: (0, up(f) + nf)),
                  pl.BlockSpec((FFN_DIM, dc), lambda bi, i, f: (0, jnp.maximum(f - nf, 0))),
                  pl.BlockSpec((FFN_CONV_W, fc), lambda bi, i, f: (0, up(f))),
                  pl.BlockSpec((1, fc), lambda bi, i, f: (0, up(f))),
                  _const_spec((1, d))],
        out_specs=pl.BlockSpec((1, tm, d), lambda bi, i, f: (bi, i, 0)),
        out_shape=jax.ShapeDtypeStruct((b, n, d), F32),
        scratch_shapes=[pltpu.VMEM((tm + 2 * halo, d), BF16), pltpu.VMEM((tm, FFN_DIM), BF16)],
        compiler_params=pltpu.CompilerParams(
            dimension_semantics=("parallel", "parallel", "arbitrary"),
            vmem_limit_bytes=VMEM_LIMIT_BYTES),
        name="conv_ffn",
    )(hn, hn, hn, hnm, h1, w_up, w_up, w_down, conv_w, conv_b, out_gain)


def _rope_tables(n):
    rows = n // GRID_W
    t_row = jnp.repeat(jnp.arange(rows), GRID_W).astype(F32)
    t_col = jnp.tile(jnp.arange(GRID_W), rows).astype(F32)
    half = HEAD_DIM // 2
    inv = ROPE_THETA ** (-jnp.arange(0, half, 2, dtype=F32) / half)
    ang = jnp.concatenate([t_row[:, None] * inv, t_col[:, None] * inv], axis=-1)
    cos, sin = jnp.cos(ang), jnp.sin(ang)
    return jnp.concatenate([cos, cos], axis=-1), jnp.concatenate([-sin, sin], axis=-1)


_TILES = dict(tm=512, tq=512, tk=1024, tc=1024, tf=512, fc=512)


def _pick(n, pref):
    t = pref
    while n % t:
        t //= 2
    return t


def _trunk(x, meta, mp, p):
    b, n, d = x.shape
    rows = b * n
    cos_t, sin_t = _rope_tables(n)
    tm = _pick(n, _TILES["tm"])
    nt = n // tm
    q, k, vt, xc, gy, xc_head = _in_proj(x.reshape(rows, d), meta, nt, p["norm_pre_mix"], p["w_in"], p["q_norm"],
                                         p["k_norm"], cos_t, sin_t, p["lru_conv_w"], p["lru_conv_b"], tm)
    head = xc_head.reshape(b, nt, SUBLANES_BF16, LRU_WIDTH)[:, 0, SUBLANES_BF16 - 2:, :]
    xc_m = jnp.concatenate([jnp.broadcast_to(mp["xc"][None, :N_META - 2], (b, N_META - 2, LRU_WIDTH)), head], axis=1)
    tq = _pick(n, _TILES["tq"])
    tk = _pick(n, _TILES["tk"])
    bound = mp["bound"]

    def attend(running_max):
        def run(bound, q, qm, k, vt, km, vmt):
            real = _attention(bound, q, k, vt, km, vmt, b, tq, tk, False, running_max)
            meta_rows = _attention(bound, qm, k, vt, km, vmt, b, LANES, tk, True, running_max)
            return real, meta_rows
        return run

    attn, attn_m = lax.cond(2.0 * bound[0] <= MAX_FIXED_SHIFT_RANGE, attend(False), attend(True),
                            bound, q, mp["q"], k, vt, mp["km"], mp["vmt"])
    attn_m = attn_m[:, :N_META]
    lru, lru_m = _lru(xc.reshape(b, n, LRU_WIDTH), gy.reshape(b, n, LRU_WIDTH), xc_m, mp["gy"],
                      p["lru_w_r"], p["lru_b_r"], p["lru_w_i"], p["lru_b_i"],
                      p["lru_lambda"], p["lru_out_norm"], _pick(n, _TILES["tc"]))
    op = (p["attn_out_norm"], p["w_out_a"], p["w_out_l"], p["norm_post_mix"], p["norm_pre_ffn"])
    h1, hn = _out_proj(attn.reshape(rows, ATTN_WIDTH), lru.reshape(rows, LRU_WIDTH), x.reshape(rows, d), *op, tm)
    res_m = jnp.broadcast_to(meta[None], (b, N_META, d)).reshape(b * N_META, d)
    _, hn_m = _out_proj(attn_m.reshape(b * N_META, ATTN_WIDTH), lru_m.reshape(b * N_META, LRU_WIDTH), res_m,
                        *op, N_META)
    return _ffn(hn.reshape(b, n, d), hn_m.reshape(b, N_META, d), h1.reshape(b, n, d), p["w_up"], p["w_down"],
                p["ffn_conv_w"], p["ffn_conv_b"], p["norm_post_ffn"], _pick(n, _TILES["tf"]), _TILES["fc"])


def _head_perm():
    q4 = HEAD_DIM // 4
    idx = jnp.arange(HEAD_DIM).reshape(2, 2, q4)
    return idx.transpose(1, 0, 2).reshape(HEAD_DIM)


def kernel(x_prompt, x_sample, meta_tokens, norm_pre_mix, w_in, q_norm, k_norm, lru_conv_w, lru_conv_b,
           lru_w_r, lru_b_r, lru_w_i, lru_b_i, lru_lambda, attn_out_norm, lru_out_norm, w_out,
           norm_post_mix, norm_pre_ffn, w_up, ffn_conv_w, ffn_conv_b, w_down, norm_post_ffn):
    perm = _head_perm()
    n_rot = N_Q_HEADS + N_KV_HEADS
    cols = (jnp.arange(n_rot)[:, None] * HEAD_DIM + perm[None, :]).reshape(-1)
    cols = jnp.concatenate([cols, jnp.arange(n_rot * HEAD_DIM, IN_WIDTH)])
    w_out_b = w_out[0].astype(BF16)
    p = {
        "norm_pre_mix": norm_pre_mix[0][None], "w_in": w_in[0][:, cols].astype(BF16),
        "q_norm": q_norm[0][perm][None], "k_norm": k_norm[0][perm][None],
        "lru_conv_w": lru_conv_w[0], "lru_conv_b": lru_conv_b[0][None],
        "lru_w_r": (0.5 * lru_w_r[0]).astype(BF16), "lru_b_r": 0.5 * lru_b_r[0],
        "lru_w_i": (0.5 * lru_w_i[0]).astype(BF16), "lru_b_i": 0.5 * lru_b_i[0], "lru_lambda": lru_lambda[0],
        "attn_out_norm": attn_out_norm[0][None], "lru_out_norm": lru_out_norm[0][None],
        "w_out_a": w_out_b[:ATTN_WIDTH], "w_out_l": w_out_b[ATTN_WIDTH:],
        "norm_post_mix": norm_post_mix[0][None], "norm_pre_ffn": norm_pre_ffn[0][None],
        "w_up": w_up[0].astype(BF16), "ffn_conv_w": ffn_conv_w[0], "ffn_conv_b": ffn_conv_b[0][None],
        "w_down": w_down[0].astype(BF16), "norm_post_ffn": norm_post_ffn[0][None],
    }
    ones = jnp.ones((LANES, HEAD_DIM), F32)
    meta_pad = jnp.pad(meta_tokens, ((0, LANES - N_META), (0, 0)))
    qm, km, vtm, xcm, gym, _ = _in_proj(meta_pad, jnp.zeros((SUBLANES_BF16, D_MODEL), F32), 1, p["norm_pre_mix"],
                                        p["w_in"], p["q_norm"], p["k_norm"], ones, jnp.zeros_like(ones),
                                        p["lru_conv_w"], p["lru_conv_b"], LANES)
    bound = (HEAD_DIM * Q_PRESCALE * BF16_ROUNDING_SLACK) * jnp.max(jnp.abs(q_norm[0])) * jnp.max(jnp.abs(k_norm[0]))
    mp = {"q": qm, "km": km[:, :N_META], "vmt": vtm[:, :N_META], "xc": xcm[:N_META], "gy": gym[:N_META],
          "bound": bound.reshape(1).astype(F32)}
    return _trunk(x_prompt, meta_tokens, mp, p), _trunk(x_sample, meta_tokens, mp, p)
```

```python
import functools

import jax
import jax.numpy as jnp
from jax import lax
from jax.experimental import pallas as pl
from jax.experimental.pallas import tpu as pltpu

D_MODEL = 2048
N_META = 16
GRID_W = 64
HEAD_DIM = 128
N_Q_HEADS = 8
N_KV_HEADS = 2
Q_PER_KV = N_Q_HEADS // N_KV_HEADS
ATTN_WIDTH = N_Q_HEADS * HEAD_DIM
KV_WIDTH = N_KV_HEADS * HEAD_DIM
LRU_WIDTH = D_MODEL - ATTN_WIDTH
LRU_BLOCKS = 8
LRU_BLOCK = LRU_WIDTH // LRU_BLOCKS
LRU_CONV_W = 4
LRU_C = 8.0
LRU_SUB = 64
OUT_PROJ_PARTS = 2
ATTN_KV_UNROLL = 8
META_Q_ROWS = 32
IN_WIDTH = ATTN_WIDTH + 2 * KV_WIDTH + 2 * LRU_WIDTH
FFN_DIM = 5632
FFN_CONV_W = 3
ROPE_THETA = 10000.0
EPS = 1e-6
ATTN_SCALE = HEAD_DIM ** -0.5
LOG2E = 1.4426950408889634
LN2 = 0.6931471805599453
Q_PRESCALE = ATTN_SCALE * LOG2E
MAX_FIXED_SHIFT_RANGE = 120.0
BF16_ROUNDING_SLACK = 1.0 + 2.0 ** -6

LANES = 128
SUBLANES_F32 = 8
SUBLANES_BF16 = 16
VMEM_LIMIT_BYTES = 56 * 1024 * 1024

F32 = jnp.float32
BF16 = jnp.bfloat16


def _rms(x, g):
    ms = jnp.mean(x * x, axis=-1, keepdims=True)
    return x * lax.rsqrt(ms + EPS) * g


def _dot(a, b):
    return jnp.dot(a, b, preferred_element_type=F32)


def _const_spec(shape):
    zeros = (0,) * len(shape)
    return pl.BlockSpec(shape, lambda *_: zeros)


def _in_proj_kernel(x_ref, xprev_ref, xnext_ref, xhead_ref, g_ref, w_ref, qg_ref, kg_ref, cos_ref, sin_ref,
                    cw_ref, cb_ref, q_ref, k_ref, vt_ref, xc_ref, gy_ref, xch_ref, *, nt):
    tm = x_ref.shape[0]
    halo = SUBLANES_BF16
    pos = pl.program_id(0) % nt
    xprev = jnp.where(pos == 0, xhead_ref[...], xprev_ref[...])
    xnext = jnp.where(pos == nt - 1, jnp.zeros_like(xnext_ref[...]), xnext_ref[...])
    xn = _rms(x_ref[...], g_ref[...]).astype(BF16)
    xn_ext = jnp.concatenate([_rms(xprev, g_ref[...]).astype(BF16), xn, _rms(xnext, g_ref[...]).astype(BF16)], axis=0)
    cos = cos_ref[...]
    sin = sin_ref[...]

    def rope_head(zh, gain):
        y = _rms(zh, gain)
        return y * cos + pltpu.roll(y, HEAD_DIM // 2, axis=1) * sin

    o3 = ATTN_WIDTH + 2 * KV_WIDTH
    zq = _dot(xn, w_ref[:, 0:ATTN_WIDTH])
    for h in range(N_Q_HEADS):
        sl = slice(h * HEAD_DIM, (h + 1) * HEAD_DIM)
        q_ref[h] = (rope_head(zq[:, sl], qg_ref[...]) * Q_PRESCALE).astype(BF16)
    xr_ext = _dot(xn_ext, w_ref[:, o3:o3 + LRU_WIDTH])
    n_ext = tm + 2 * halo
    xc_ext = jnp.broadcast_to(cb_ref[...], (n_ext, LRU_WIDTH))
    for kk in range(LRU_CONV_W):
        shift = (LRU_CONV_W // 2 - kk) % n_ext
        xk = xr_ext if shift == 0 else pltpu.roll(xr_ext, shift, axis=0)
        xc_ext = xc_ext + xk * cw_ref[kk:kk + 1, :]
    xc_ref[...] = xc_ext[halo:halo + tm]
    xch_ref[...] = xc_ext[0:halo]
    zkv = _dot(xn, w_ref[:, ATTN_WIDTH:ATTN_WIDTH + 2 * KV_WIDTH])
    for h in range(N_KV_HEADS):
        sl = slice(h * HEAD_DIM, (h + 1) * HEAD_DIM)
        k_ref[h] = rope_head(zkv[:, sl], kg_ref[...]).astype(BF16)
    vt_ref[...] = zkv[:, KV_WIDTH:2 * KV_WIDTH].T.astype(BF16)
    gy_ref[...] = _dot(xn, w_ref[:, o3 + LRU_WIDTH:o3 + 2 * LRU_WIDTH])


def _in_proj(x2d, xhead, nt, gain, w_in, q_gain, k_gain, cos_t, sin_t, conv_w, conv_b, tm):
    rows = x2d.shape[0]
    n_tab = cos_t.shape[0] // tm
    halo = SUBLANES_BF16
    th = tm // halo
    nh = rows // halo
    row_spec = lambda w: pl.BlockSpec((tm, w), lambda i: (i, 0))
    tab_spec = pl.BlockSpec((tm, HEAD_DIM), lambda i: (i % n_tab, 0))
    return pl.pallas_call(
        functools.partial(_in_proj_kernel, nt=nt),
        grid=(rows // tm,),
        in_specs=[row_spec(D_MODEL),
                  pl.BlockSpec((halo, D_MODEL), lambda i: (jnp.maximum(i * th - 1, 0), 0)),
                  pl.BlockSpec((halo, D_MODEL), lambda i: (jnp.minimum((i + 1) * th, nh - 1), 0)),
                  _const_spec((halo, D_MODEL)), _const_spec((1, D_MODEL)),
                  pl.BlockSpec((D_MODEL, IN_WIDTH), lambda i: (0, 0), pipeline_mode=pl.Buffered(1)),
                  _const_spec((1, HEAD_DIM)), _const_spec((1, HEAD_DIM)), tab_spec, tab_spec,
                  _const_spec((LRU_CONV_W, LRU_WIDTH)), _const_spec((1, LRU_WIDTH))],
        out_specs=[pl.BlockSpec((N_Q_HEADS, tm, HEAD_DIM), lambda i: (0, i, 0)),
                   pl.BlockSpec((N_KV_HEADS, tm, HEAD_DIM), lambda i: (0, i, 0)),
                   pl.BlockSpec((KV_WIDTH, tm), lambda i: (0, i)),
                   row_spec(LRU_WIDTH), row_spec(LRU_WIDTH),
                   pl.BlockSpec((halo, LRU_WIDTH), lambda i: (i, 0))],
        out_shape=[jax.ShapeDtypeStruct((N_Q_HEADS, rows, HEAD_DIM), BF16),
                   jax.ShapeDtypeStruct((N_KV_HEADS, rows, HEAD_DIM), BF16),
                   jax.ShapeDtypeStruct((KV_WIDTH, rows), BF16),
                   jax.ShapeDtypeStruct((rows, LRU_WIDTH), F32),
                   jax.ShapeDtypeStruct((rows, LRU_WIDTH), F32),
                   jax.ShapeDtypeStruct((rows // tm * halo, LRU_WIDTH), F32)],
        compiler_params=pltpu.CompilerParams(dimension_semantics=("parallel",),
                                             vmem_limit_bytes=VMEM_LIMIT_BYTES),
        name="in_proj",
    )(x2d, x2d, x2d, xhead, gain, w_in, q_gain, k_gain, cos_t, sin_t, conv_w, conv_b)


def _attn_kernel(bound_ref, q_ref, k_ref, vt_ref, km_ref, vmt_ref, o_ref, l_ref, acc_ref, *m_scratch,
                 nkv, tk, running_max):
    tq = q_ref.shape[1]
    width = Q_PER_KV * tq
    qs = q_ref[...].reshape(width, HEAD_DIM)
    nt = (((1,), (1,)), ((), ()))

    def block(kb, vtb, first):
        st = lax.dot_general(kb, qs, nt, preferred_element_type=F32)
        if running_max:
            m_ref, = m_scratch
            m_new = jnp.max(st, axis=0, keepdims=True)
            if not first:
                m_prev = m_ref[...]
                m_new = jnp.maximum(m_prev, m_new)
                alpha = jnp.exp2(m_prev - m_new)
            m_ref[...] = m_new
            p = jnp.exp2(st - m_new)
        else:
            p = jnp.exp2(st - bound_ref[0])
        psum = jnp.sum(p.reshape(p.shape[0] // SUBLANES_F32, SUBLANES_F32, width), axis=0)
        pv = _dot(vtb, p.astype(BF16))
        if first:
            l_ref[...] = psum
            acc_ref[...] = pv
        elif running_max:
            l_ref[...] = alpha * l_ref[...] + psum
            acc_ref[...] = alpha * acc_ref[...] + pv
        else:
            l_ref[...] += psum
            acc_ref[...] += pv

    block(km_ref[0], vmt_ref[...], True)

    def body(j, carry):
        start = pl.multiple_of(j * tk, tk)
        block(k_ref[0, pl.ds(start, tk), :], vt_ref[:, pl.ds(start, tk)], False)
        return carry

    unroll = 1 if running_max else next(u for u in (ATTN_KV_UNROLL, 4, 2, 1) if nkv % u == 0)
    lax.fori_loop(0, nkv, body, 0, unroll=unroll)

    o = acc_ref[...] * (1.0 / jnp.sum(l_ref[...], axis=0, keepdims=True))
    for g in range(Q_PER_KV):
        o_ref[0, :, g * HEAD_DIM:(g + 1) * HEAD_DIM] = o[:, g * tq:(g + 1) * tq].T.astype(BF16)


def _attention(bound, q, k, vt, km, vmt, batch, tq, tk, shared_q, running_max):
    n = k.shape[1] // batch
    nq = q.shape[1] if shared_q else q.shape[1] // batch
    nkv = n // tk
    nqt = nq // tq
    width = Q_PER_KV * tq
    if shared_q:
        q_map = lambda bi, h, i: (h, i, 0)
    else:
        q_map = lambda bi, h, i: (h, bi * nqt + i, 0)
    scratch = [pltpu.VMEM((SUBLANES_F32, width), F32), pltpu.VMEM((HEAD_DIM, width), F32)]
    if running_max:
        scratch.append(pltpu.VMEM((1, width), F32))
    return pl.pallas_call(
        functools.partial(_attn_kernel, nkv=nkv, tk=tk, running_max=running_max),
        grid=(batch, N_KV_HEADS, nqt),
        in_specs=[pl.BlockSpec(memory_space=pltpu.SMEM),
                  pl.BlockSpec((Q_PER_KV, tq, HEAD_DIM), q_map),
                  pl.BlockSpec((1, n, HEAD_DIM), lambda bi, h, i: (h, bi, 0)),
                  pl.BlockSpec((HEAD_DIM, n), lambda bi, h, i: (h, bi)),
                  pl.BlockSpec((1, N_META, HEAD_DIM), lambda bi, h, i: (h, 0, 0)),
                  pl.BlockSpec((HEAD_DIM, N_META), lambda bi, h, i: (h, 0))],
        out_specs=pl.BlockSpec((1, tq, Q_PER_KV * HEAD_DIM), lambda bi, h, i: (bi, i, h)),
        out_shape=jax.ShapeDtypeStruct((batch, nq, ATTN_WIDTH), BF16),
        scratch_shapes=scratch,
        compiler_params=pltpu.CompilerParams(
            dimension_semantics=("parallel", "parallel", "arbitrary"),
            vmem_limit_bytes=VMEM_LIMIT_BYTES),
        name="attention",
    )(bound, q, k, vt, km, vmt)


def _softplus(x):
    return jnp.maximum(x, 0.0) + jnp.log1p(jnp.exp(-jnp.abs(x)))


def _lru_prep(xc, rows, wr_ref, br_ref, wi_ref, bi_ref, lam_ref, a_ref, u_ref):
    xcb = xc.astype(BF16)
    r_parts, i_parts = [], []
    for blk in range(LRU_BLOCKS):
        xb = xcb[:, blk * LRU_BLOCK:(blk + 1) * LRU_BLOCK]
        r_parts.append(_dot(xb, wr_ref[blk]))
        i_parts.append(_dot(xb, wi_ref[blk]))
    tr = jnp.tanh(jnp.concatenate(r_parts, axis=-1) + br_ref[...])
    ti = jnp.tanh(jnp.concatenate(i_parts, axis=-1) + bi_ref[...])
    c1 = (-0.5 * LRU_C * LOG2E) * _softplus(-lam_ref[...])
    log2_a = c1 * tr + c1
    a = jnp.exp2(log2_a)
    a_ref[0:rows, :] = a
    gate2 = jnp.tanh((-LN2) * log2_a) * (1.0 + a * a)
    gate = jnp.where(gate2 > 0.0, gate2 * lax.rsqrt(gate2), 0.0)
    half_xc = 0.5 * xc
    u_ref[0:rows, :] = gate * (half_xc * ti + half_xc)


def _lru_scan(rows, a_ref, u_ref, h, out_ref, base, reverse):
    for r in (range(rows - 1, -1, -1) if reverse else range(rows)):
        h = a_ref[r:r + 1, :] * h + u_ref[r:r + 1, :]
        out_ref[pl.ds(base + r, 1), :] = h
    return h


def _lru_fwd_kernel(xc_ref, xcm_ref, wr_ref, br_ref, wi_ref, bi_ref,
                    lam_ref, hf_ref, hfm_ref, a0_ref, u0_ref, a1_ref, u1_ref, h_ref):
    j = pl.program_id(1)
    tc = xc_ref.shape[1]
    wts = (wr_ref, br_ref, wi_ref, bi_ref, lam_ref)
    nsub = tc // LRU_SUB

    @pl.when(j == 0)
    def _():
        _lru_prep(xcm_ref[0], N_META, *wts, a0_ref, u0_ref)
        h_ref[...] = _lru_scan(N_META, a0_ref, u0_ref, jnp.zeros(h_ref.shape, F32), hfm_ref.at[0], 0, False)

    @pl.when(j > 0)
    def _():
        def prep(s, a_ref, u_ref):
            start = pl.multiple_of(s * LRU_SUB, LRU_SUB)
            _lru_prep(xc_ref[0, pl.ds(start, LRU_SUB), :], LRU_SUB, *wts, a_ref, u_ref)

        prep(0, a0_ref, u0_ref)

        def pair(i, h):
            s = 2 * i
            prep(s + 1, a1_ref, u1_ref)
            h = _lru_scan(LRU_SUB, a0_ref, u0_ref, h, hf_ref.at[0], pl.multiple_of(s * LRU_SUB, LRU_SUB), False)
            prep(jnp.minimum(s + 2, nsub - 1), a0_ref, u0_ref)
            return _lru_scan(LRU_SUB, a1_ref, u1_ref, h, hf_ref.at[0],
                             pl.multiple_of((s + 1) * LRU_SUB, LRU_SUB), False)

        h_ref[...] = lax.fori_loop(0, nsub // 2, pair, h_ref[...])


def _lru_bwd_kernel(xc_ref, xcm_ref, gy_ref, gym_ref, hf_ref, hfm_ref,
                    wr_ref, br_ref, wi_ref, bi_ref, lam_ref, og_ref,
                    out_ref, outm_ref, a0_ref, u0_ref, a1_ref, u1_ref, h_ref, hb0_ref, hb1_ref, *, nc):
    j = pl.program_id(1)
    tc = xc_ref.shape[1]
    wts = (wr_ref, br_ref, wi_ref, bi_ref, lam_ref)
    nsub = tc // LRU_SUB

    def finish(hf, hb, gy):
        return _rms((hf + hb) * jax.nn.gelu(gy), og_ref[...]).astype(BF16)

    @pl.when(j == 0)
    def _():
        h_ref[...] = jnp.zeros(h_ref.shape, F32)

    @pl.when(j < nc)
    def _():
        def prep(s, a_ref, u_ref):
            start = pl.multiple_of(s * LRU_SUB, LRU_SUB)
            _lru_prep(xc_ref[0, pl.ds(start, LRU_SUB), :], LRU_SUB, *wts, a_ref, u_ref)

        def piece(s, a_ref, u_ref, hb_ref, h):
            h = _lru_scan(LRU_SUB, a_ref, u_ref, h, hb_ref, 0, True)
            rows = pl.ds(pl.multiple_of(s * LRU_SUB, LRU_SUB), LRU_SUB)
            out_ref[0, rows, :] = finish(hf_ref[0, rows, :], hb_ref[...], gy_ref[0, rows, :])
            return h

        prep(nsub - 1, a0_ref, u0_ref)

        def pair(i, h):
            s = nsub - 1 - 2 * i
            prep(s - 1, a1_ref, u1_ref)
            h = piece(s, a0_ref, u0_ref, hb0_ref, h)
            prep(jnp.maximum(s - 2, 0), a0_ref, u0_ref)
            return piece(s - 1, a1_ref, u1_ref, hb1_ref, h)

        h_ref[...] = lax.fori_loop(0, nsub // 2, pair, h_ref[...])

    @pl.when(j == nc)
    def _():
        _lru_prep(xcm_ref[0], N_META, *wts, a0_ref, u0_ref)
        h_ref[...] = _lru_scan(N_META, a0_ref, u0_ref, h_ref[...], hb0_ref, 0, True)
        outm_ref[0] = finish(hfm_ref[0], hb0_ref[0:N_META, :], gym_ref[...])


def _lru(xc, gy, xcm, gym, w_r, b_r, w_i, b_i, lam, out_gain, tc):
    b, n, c = xc.shape
    nc = n // tc
    wspec = _const_spec((LRU_BLOCKS, LRU_BLOCK, LRU_BLOCK))
    vspec = _const_spec((1, c))
    pspecs = [wspec, vspec, wspec, vspec, vspec]
    bmspec = pl.BlockSpec((1, N_META, c), lambda bi, j: (bi, 0, 0))
    slots = [pltpu.VMEM((LRU_SUB, c), F32)] * 4
    assert tc % (2 * LRU_SUB) == 0 and LRU_SUB >= N_META

    def chunk_f(j):
        return jnp.maximum(j - 1, 0)

    hf, hfm = pl.pallas_call(
        _lru_fwd_kernel,
        grid=(b, nc + 1),
        in_specs=[pl.BlockSpec((1, tc, c), lambda bi, j: (bi, chunk_f(j), 0)), bmspec] + pspecs,
        out_specs=[pl.BlockSpec((1, tc, c), lambda bi, j: (bi, chunk_f(j), 0)), bmspec],
        out_shape=[jax.ShapeDtypeStruct((b, n, c), F32), jax.ShapeDtypeStruct((b, N_META, c), F32)],
        scratch_shapes=slots + [pltpu.VMEM((1, c), F32)],
        compiler_params=pltpu.CompilerParams(dimension_semantics=("parallel", "arbitrary"),
                                             vmem_limit_bytes=VMEM_LIMIT_BYTES),
        name="lru_fwd",
    )(xc, xcm, w_r[0], b_r[0:1], w_i[0], b_i[0:1], lam[0:1])

    def chunk_b(j):
        return jnp.maximum(nc - 1 - j, 0)

    chunk_spec = pl.BlockSpec((1, tc, c), lambda bi, j: (bi, chunk_b(j), 0))
    out, outm = pl.pallas_call(
        functools.partial(_lru_bwd_kernel, nc=nc),
        grid=(b, nc + 1),
        in_specs=[chunk_spec, bmspec, chunk_spec, _const_spec((N_META, c)), chunk_spec, bmspec] + pspecs + [vspec],
        out_specs=[chunk_spec, bmspec],
        out_shape=[jax.ShapeDtypeStruct((b, n, c), BF16), jax.ShapeDtypeStruct((b, N_META, c), BF16)],
        scratch_shapes=slots + [pltpu.VMEM((1, c), F32),
                                pltpu.VMEM((LRU_SUB, c), F32), pltpu.VMEM((LRU_SUB, c), F32)],
        compiler_params=pltpu.CompilerParams(dimension_semantics=("parallel", "arbitrary"),
                                             vmem_limit_bytes=VMEM_LIMIT_BYTES),
        name="lru_bwd",
    )(xc, xcm, gy, gym, hf, hfm, w_r[1], b_r[1:2], w_i[1], b_i[1:2], lam[1:2], out_gain)
    return out, outm


def _out_proj_kernel(attn_ref, lru_ref, res_ref, ag_ref, wa_ref, wl_ref, pg_ref, fg_ref, h1_ref, hn_ref):
    tm = attn_ref.shape[0]
    parts = OUT_PROJ_PARTS if tm % (OUT_PROJ_PARTS * LANES) == 0 else 1
    part = tm // parts
    for s in range(parts):
        rows = slice(s * part, (s + 1) * part)
        an = _rms(attn_ref[rows, :].astype(F32), ag_ref[...]).astype(BF16)
        mixed = _dot(an, wa_ref[...]) + _dot(lru_ref[rows, :], wl_ref[...])
        h1 = res_ref[rows, :] + _rms(mixed, pg_ref[...])
        h1_ref[rows, :] = h1
        hn_ref[rows, :] = _rms(h1, fg_ref[...]).astype(BF16)


def _out_proj(attn2d, lru2d, res2d, attn_gain, w_a, w_l, post_gain, ffn_gain, tm):
    rows = attn2d.shape[0]
    row_spec = lambda w: pl.BlockSpec((tm, w), lambda i: (i, 0))
    wspec = pl.BlockSpec((ATTN_WIDTH, D_MODEL), lambda i: (0, 0), pipeline_mode=pl.Buffered(1))
    return pl.pallas_call(
        _out_proj_kernel,
        grid=(rows // tm,),
        in_specs=[row_spec(ATTN_WIDTH), row_spec(LRU_WIDTH), row_spec(D_MODEL), _const_spec((1, ATTN_WIDTH)),
                  wspec, wspec, _const_spec((1, D_MODEL)), _const_spec((1, D_MODEL))],
        out_specs=[row_spec(D_MODEL), row_spec(D_MODEL)],
        out_shape=[jax.ShapeDtypeStruct((rows, D_MODEL), F32), jax.ShapeDtypeStruct((rows, D_MODEL), BF16)],
        compiler_params=pltpu.CompilerParams(dimension_semantics=("parallel",),
                                             vmem_limit_bytes=VMEM_LIMIT_BYTES),
        name="out_proj",
    )(attn2d, lru2d, res2d, attn_gain, w_a, w_l, post_gain, ffn_gain)


def _ffn_kernel(hn_ref, prev_ref, next_ref, hm_ref, h1_ref, wg_ref, wv_ref, wd_ref, cw_ref, cb_ref, og_ref,
                out_ref, ext_ref, *, nt, nf):
    i = pl.program_id(1)
    f = pl.program_id(2)
    tm = hn_ref.shape[1]
    halo = SUBLANES_BF16
    acc_ref = out_ref.at[0]

    @pl.when(f == 0)
    def _():
        ext_ref[0:halo, :] = jnp.where(i == 0, hm_ref[0], prev_ref[0])
        ext_ref[halo:halo + tm, :] = hn_ref[0]
        ext_ref[halo + tm:2 * halo + tm, :] = jnp.where(i == nt - 1, jnp.zeros_like(next_ref[0]), next_ref[0])
        acc_ref[...] = jnp.zeros(acc_ref.shape, F32)

    n_ext = tm + 2 * halo
    gfull = _dot(ext_ref[...], wg_ref[...])
    g = jnp.broadcast_to(cb_ref[...], (tm, gfull.shape[1]))
    for kk in range(FFN_CONV_W):
        shift = (FFN_CONV_W // 2 - kk) % n_ext
        gk = gfull if shift == 0 else pltpu.roll(gfull, shift, axis=0)
        g = g + gk[halo:halo + tm] * cw_ref[kk:kk + 1, :]
    val = _dot(ext_ref[halo:halo + tm, :], wv_ref[...])
    act = (jax.nn.silu(g) * val).astype(BF16)
    acc_ref[...] += _dot(act, wd_ref[...])

    @pl.when(f == nf - 1)
    def _():
        acc_ref[...] = h1_ref[0] + _rms(acc_ref[...], og_ref[...])


def _ffn(hn, hnm, h1, w_up, w_down, conv_w, conv_b, out_gain, tm, fc):
    b, n, d = hn.shape
    nt = n // tm
    nf = FFN_DIM // fc
    halo = SUBLANES_BF16
    th = tm // halo
    nh = n // halo
    return pl.pallas_call(
        functools.partial(_ffn_kernel, nt=nt, nf=nf),
        grid=(b, nt, nf),
        in_specs=[pl.BlockSpec((1, tm, d), lambda bi, i, f: (bi, i, 0)),
                  pl.BlockSpec((1, halo, d), lambda bi, i, f: (bi, jnp.maximum(i * th - 1, 0), 0)),
                  pl.BlockSpec((1, halo, d), lambda bi, i, f: (bi, jnp.minimum((i + 1) * th, nh - 1), 0)),
                  pl.BlockSpec((1, N_META, d), lambda bi, i, f: (bi, 0, 0)),
                  pl.BlockSpec((1, tm, d), lambda bi, i, f: (bi, i, 0)),
                  pl.BlockSpec((d, fc), lambda bi, i, f: (0, f)),
                  pl.BlockSpec((d, fc), lambda bi, i, f: (0, f + nf)),
                  pl.BlockSpec((fc, d), lambda bi, i, f: (f, 0)),
                  pl.BlockSpec((FFN_CONV_W, fc), lambda bi, i, f: (0, f)),
                  pl.BlockSpec((1, fc), lambda bi, i, f: (0, f)),
                  _const_spec((1, d))],
        out_specs=pl.BlockSpec((1, tm, d), lambda bi, i, f: (bi, i, 0)),
        out_shape=jax.ShapeDtypeStruct((b, n, d), F32),
        scratch_shapes=[pltpu.VMEM((tm + 2 * halo, d), BF16)],
        compiler_params=pltpu.CompilerParams(
            dimension_semantics=("parallel", "parallel", "arbitrary"),
            vmem_limit_bytes=VMEM_LIMIT_BYTES),
        name="conv_ffn",
    )(hn, hn, hn, hnm, h1, w_up, w_up, w_down, conv_w, conv_b, out_gain)


def _rope_tables(n):
    rows = n // GRID_W
    t_row = jnp.repeat(jnp.arange(rows), GRID_W).astype(F32)
    t_col = jnp.tile(jnp.arange(GRID_W), rows).astype(F32)
    half = HEAD_DIM // 2
    inv = ROPE_THETA ** (-jnp.arange(0, half, 2, dtype=F32) / half)
    ang = jnp.concatenate([t_row[:, None] * inv, t_col[:, None] * inv], axis=-1)
    cos, sin = jnp.cos(ang), jnp.sin(ang)
    return jnp.concatenate([cos, cos], axis=-1), jnp.concatenate([-sin, sin], axis=-1)


_TILES = dict(tm=512, tq=512, tk=1024, tc=1024, tf=512, fc=512)


def _pick(n, pref):
    t = pref
    while n % t:
        t //= 2
    return t


def _trunk(x, meta, mp, p):
    b, n, d = x.shape
    rows = b * n
    cos_t, sin_t = _rope_tables(n)
    tm = _pick(n, _TILES["tm"])
    nt = n // tm
    q, k, vt, xc, gy, xc_head = _in_proj(x.reshape(rows, d), meta, nt, p["norm_pre_mix"], p["w_in"], p["q_norm"],
                                         p["k_norm"], cos_t, sin_t, p["lru_conv_w"], p["lru_conv_b"], tm)
    head = xc_head.reshape(b, nt, SUBLANES_BF16, LRU_WIDTH)[:, 0, SUBLANES_BF16 - 2:, :]
    xc_m = jnp.concatenate([jnp.broadcast_to(mp["xc"][None, :N_META - 2], (b, N_META - 2, LRU_WIDTH)), head], axis=1)
    tq = _pick(n, _TILES["tq"])
    tk = _pick(n, _TILES["tk"])
    bound = mp["bound"]

    def attend(running_max):
        def run(bound, q, qm, k, vt, km, vmt):
            real = _attention(bound, q, k, vt, km, vmt, b, tq, tk, False, running_max)
            meta_rows = _attention(bound, qm, k, vt, km, vmt, b, META_Q_ROWS, tk, True, running_max)
            return real, meta_rows
        return run

    attn, attn_m = lax.cond(2.0 * bound[0] <= MAX_FIXED_SHIFT_RANGE, attend(False), attend(True),
                            bound, q, mp["q"][:, :META_Q_ROWS], k, vt, mp["km"], mp["vmt"])
    attn_m = attn_m[:, :N_META]
    lru, lru_m = _lru(xc.reshape(b, n, LRU_WIDTH), gy.reshape(b, n, LRU_WIDTH), xc_m, mp["gy"],
                      p["lru_w_r"], p["lru_b_r"], p["lru_w_i"], p["lru_b_i"],
                      p["lru_lambda"], p["lru_out_norm"], _pick(n, _TILES["tc"]))
    op = (p["attn_out_norm"], p["w_out_a"], p["w_out_l"], p["norm_post_mix"], p["norm_pre_ffn"])
    h1, hn = _out_proj(attn.reshape(rows, ATTN_WIDTH), lru.reshape(rows, LRU_WIDTH), x.reshape(rows, d), *op, tm)
    res_m = jnp.broadcast_to(meta[None], (b, N_META, d)).reshape(b * N_META, d)
    _, hn_m = _out_proj(attn_m.reshape(b * N_META, ATTN_WIDTH), lru_m.reshape(b * N_META, LRU_WIDTH), res_m,
                        *op, N_META)
    return _ffn(hn.reshape(b, n, d), hn_m.reshape(b, N_META, d), h1.reshape(b, n, d), p["w_up"], p["w_down"],
                p["ffn_conv_w"], p["ffn_conv_b"], p["norm_post_ffn"], _pick(n, _TILES["tf"]), _TILES["fc"])


def _head_perm():
    q4 = HEAD_DIM // 4
    idx = jnp.arange(HEAD_DIM).reshape(2, 2, q4)
    return idx.transpose(1, 0, 2).reshape(HEAD_DIM)


def kernel(x_prompt, x_sample, meta_tokens, norm_pre_mix, w_in, q_norm, k_norm, lru_conv_w, lru_conv_b,
           lru_w_r, lru_b_r, lru_w_i, lru_b_i, lru_lambda, attn_out_norm, lru_out_norm, w_out,
           norm_post_mix, norm_pre_ffn, w_up, ffn_conv_w, ffn_conv_b, w_down, norm_post_ffn):
    perm = _head_perm()
    n_rot = N_Q_HEADS + N_KV_HEADS
    cols = (jnp.arange(n_rot)[:, None] * HEAD_DIM + perm[None, :]).reshape(-1)
    cols = jnp.concatenate([cols, jnp.arange(n_rot * HEAD_DIM, IN_WIDTH)])
    w_out_b = w_out[0].astype(BF16)
    p = {
        "norm_pre_mix": norm_pre_mix[0][None], "w_in": w_in[0][:, cols].astype(BF16),
        "q_norm": q_norm[0][perm][None], "k_norm": k_norm[0][perm][None],
        "lru_conv_w": lru_conv_w[0], "lru_conv_b": lru_conv_b[0][None],
        "lru_w_r": (0.5 * lru_w_r[0]).astype(BF16), "lru_b_r": 0.5 * lru_b_r[0],
        "lru_w_i": (0.5 * lru_w_i[0]).astype(BF16), "lru_b_i": 0.5 * lru_b_i[0], "lru_lambda": lru_lambda[0],
        "attn_out_norm": attn_out_norm[0][None], "lru_out_norm": lru_out_norm[0][None],
        "w_out_a": w_out_b[:ATTN_WIDTH], "w_out_l": w_out_b[ATTN_WIDTH:],
        "norm_post_mix": norm_post_mix[0][None], "norm_pre_ffn": norm_pre_ffn[0][None],
        "w_up": w_up[0].astype(BF16), "ffn_conv_w": ffn_conv_w[0], "ffn_conv_b": ffn_conv_b[0][None],
        "w_down": w_down[0].astype(BF16), "norm_post_ffn": norm_post_ffn[0][None],
    }
    ones = jnp.ones((LANES, HEAD_DIM), F32)
    meta_pad = jnp.pad(meta_tokens, ((0, LANES - N_META), (0, 0)))
    qm, km, vtm, xcm, gym, _ = _in_proj(meta_pad, jnp.zeros((SUBLANES_BF16, D_MODEL), F32), 1, p["norm_pre_mix"],
                                        p["w_in"], p["q_norm"], p["k_norm"], ones, jnp.zeros_like(ones),
                                        p["lru_conv_w"], p["lru_conv_b"], LANES)
    bound = (HEAD_DIM * Q_PRESCALE * BF16_ROUNDING_SLACK) * jnp.max(jnp.abs(q_norm[0])) * jnp.max(jnp.abs(k_norm[0]))
    mp = {"q": qm, "km": km[:, :N_META], "vmt": vtm[:, :N_META], "xc": xcm[:N_META], "gy": gym[:N_META],
          "bound": bound.reshape(1).astype(F32)}
    return _trunk(x_prompt, meta_tokens, mp, p), _trunk(x_sample, meta_tokens, mp, p)
```

```python
import functools

import jax
import jax.numpy as jnp
from jax import lax
from jax.experimental import pallas as pl
from jax.experimental.pallas import tpu as pltpu

D_MODEL = 2048
N_META = 16
GRID_W = 64
HEAD_DIM = 128
N_Q_HEADS = 8
N_KV_HEADS = 2
Q_PER_KV = N_Q_HEADS // N_KV_HEADS
ATTN_WIDTH = N_Q_HEADS * HEAD_DIM
KV_WIDTH = N_KV_HEADS * HEAD_DIM
LRU_WIDTH = D_MODEL - ATTN_WIDTH
LRU_BLOCKS = 8
LRU_BLOCK = LRU_WIDTH // LRU_BLOCKS
LRU_CONV_W = 4
LRU_C = 8.0
LRU_SUB = 64
OUT_PROJ_PARTS = 2
ATTN_KV_UNROLL = 8
META_Q_ROWS = 32
IN_WIDTH = ATTN_WIDTH + 2 * KV_WIDTH + 2 * LRU_WIDTH
FFN_DIM = 5632
FFN_CONV_W = 3
ROPE_THETA = 10000.0
EPS = 1e-6
ATTN_SCALE = HEAD_DIM ** -0.5
LOG2E = 1.4426950408889634
LN2 = 0.6931471805599453
Q_PRESCALE = ATTN_SCALE * LOG2E
MAX_FIXED_SHIFT_RANGE = 120.0
BF16_ROUNDING_SLACK = 1.0 + 2.0 ** -6

LANES = 128
SUBLANES_F32 = 8
SUBLANES_BF16 = 16
VMEM_LIMIT_BYTES = 56 * 1024 * 1024

F32 = jnp.float32
BF16 = jnp.bfloat16


def _rms(x, g):
    ms = jnp.mean(x * x, axis=-1, keepdims=True)
    return x * lax.rsqrt(ms + EPS) * g


def _dot(a, b):
    return jnp.dot(a, b, preferred_element_type=F32)


def _const_spec(shape):
    zeros = (0,) * len(shape)
    return pl.BlockSpec(shape, lambda *_: zeros)


def _in_proj_kernel(x_ref, xprev_ref, xnext_ref, xhead_ref, g_ref, w_ref, qg_ref, kg_ref, cos_ref, sin_ref,
                    cw_ref, cb_ref, q_ref, k_ref, vt_ref, xc_ref, gy_ref, xch_ref, *, nt):
    tm = x_ref.shape[0]
    halo = SUBLANES_BF16
    pos = pl.program_id(0) % nt
    xprev = jnp.where(pos == 0, xhead_ref[...], xprev_ref[...])
    xnext = jnp.where(pos == nt - 1, jnp.zeros_like(xnext_ref[...]), xnext_ref[...])
    xn = _rms(x_ref[...], g_ref[...]).astype(BF16)
    xn_ext = jnp.concatenate([_rms(xprev, g_ref[...]).astype(BF16), xn, _rms(xnext, g_ref[...]).astype(BF16)], axis=0)
    cos = cos_ref[...]
    sin = sin_ref[...]

    def rope_head(zh, gain):
        y = _rms(zh, gain)
        return y * cos + pltpu.roll(y, HEAD_DIM // 2, axis=1) * sin

    o3 = ATTN_WIDTH + 2 * KV_WIDTH
    zq = _dot(xn, w_ref[:, 0:ATTN_WIDTH])
    for h in range(N_Q_HEADS):
        sl = slice(h * HEAD_DIM, (h + 1) * HEAD_DIM)
        q_ref[h] = (rope_head(zq[:, sl], qg_ref[...]) * Q_PRESCALE).astype(BF16)
    xr_ext = _dot(xn_ext, w_ref[:, o3:o3 + LRU_WIDTH])
    n_ext = tm + 2 * halo
    xc_ext = jnp.broadcast_to(cb_ref[...], (n_ext, LRU_WIDTH))
    for kk in range(LRU_CONV_W):
        shift = (LRU_CONV_W // 2 - kk) % n_ext
        xk = xr_ext if shift == 0 else pltpu.roll(xr_ext, shift, axis=0)
        xc_ext = xc_ext + xk * cw_ref[kk:kk + 1, :]
    xc_ref[...] = xc_ext[halo:halo + tm]
    xch_ref[...] = xc_ext[0:halo]
    zkv = _dot(xn, w_ref[:, ATTN_WIDTH:ATTN_WIDTH + 2 * KV_WIDTH])
    for h in range(N_KV_HEADS):
        sl = slice(h * HEAD_DIM, (h + 1) * HEAD_DIM)
        k_ref[h] = rope_head(zkv[:, sl], kg_ref[...]).astype(BF16)
    vt_ref[...] = zkv[:, KV_WIDTH:2 * KV_WIDTH].T.astype(BF16)
    gy_ref[...] = _dot(xn, w_ref[:, o3 + LRU_WIDTH:o3 + 2 * LRU_WIDTH])


def _in_proj(x2d, xhead, nt, gain, w_in, q_gain, k_gain, cos_t, sin_t, conv_w, conv_b, tm):
    rows = x2d.shape[0]
    n_tab = cos_t.shape[0] // tm
    halo = SUBLANES_BF16
    th = tm // halo
    nh = rows // halo
    row_spec = lambda w: pl.BlockSpec((tm, w), lambda i: (i, 0))
    tab_spec = pl.BlockSpec((tm, HEAD_DIM), lambda i: (i % n_tab, 0))
    return pl.pallas_call(
        functools.partial(_in_proj_kernel, nt=nt),
        grid=(rows // tm,),
        in_specs=[row_spec(D_MODEL),
                  pl.BlockSpec((halo, D_MODEL), lambda i: (jnp.maximum(i * th - 1, 0), 0)),
                  pl.BlockSpec((halo, D_MODEL), lambda i: (jnp.minimum((i + 1) * th, nh - 1), 0)),
                  _const_spec((halo, D_MODEL)), _const_spec((1, D_MODEL)),
                  pl.BlockSpec((D_MODEL, IN_WIDTH), lambda i: (0, 0), pipeline_mode=pl.Buffered(1)),
                  _const_spec((1, HEAD_DIM)), _const_spec((1, HEAD_DIM)), tab_spec, tab_spec,
                  _const_spec((LRU_CONV_W, LRU_WIDTH)), _const_spec((1, LRU_WIDTH))],
        out_specs=[pl.BlockSpec((N_Q_HEADS, tm, HEAD_DIM), lambda i: (0, i, 0)),
                   pl.BlockSpec((N_KV_HEADS, tm, HEAD_DIM), lambda i: (0, i, 0)),
                   pl.BlockSpec((KV_WIDTH, tm), lambda i: (0, i)),
                   row_spec(LRU_WIDTH), row_spec(LRU_WIDTH),
                   pl.BlockSpec((halo, LRU_WIDTH), lambda i: (i, 0))],
        out_shape=[jax.ShapeDtypeStruct((N_Q_HEADS, rows, HEAD_DIM), BF16),
                   jax.ShapeDtypeStruct((N_KV_HEADS, rows, HEAD_DIM), BF16),
                   jax.ShapeDtypeStruct((KV_WIDTH, rows), BF16),
                   jax.ShapeDtypeStruct((rows, LRU_WIDTH), F32),
                   jax.ShapeDtypeStruct((rows, LRU_WIDTH), F32),
                   jax.ShapeDtypeStruct((rows // tm * halo, LRU_WIDTH), F32)],
        compiler_params=pltpu.CompilerParams(dimension_semantics=("parallel",),
                                             vmem_limit_bytes=VMEM_LIMIT_BYTES),
        name="in_proj",
    )(x2d, x2d, x2d, xhead, gain, w_in, q_gain, k_gain, cos_t, sin_t, conv_w, conv_b)


def _attn_kernel(bound_ref, q_ref, k_ref, vt_ref, km_ref, vmt_ref, o_ref, l_ref, acc_ref, *m_scratch,
                 nkv, tk, running_max):
    tq = q_ref.shape[1]
    width = Q_PER_KV * tq
    qs = q_ref[...].reshape(width, HEAD_DIM)
    nt = (((1,), (1,)), ((), ()))

    def block(kb, vtb, first):
        st = lax.dot_general(kb, qs, nt, preferred_element_type=F32)
        if running_max:
            m_ref, = m_scratch
            m_new = jnp.max(st, axis=0, keepdims=True)
            if not first:
                m_prev = m_ref[...]
                m_new = jnp.maximum(m_prev, m_new)
                alpha = jnp.exp2(m_prev - m_new)
            m_ref[...] = m_new
            p = jnp.exp2(st - m_new)
        else:
            p = jnp.exp2(st - bound_ref[0])
        psum = jnp.sum(p.reshape(p.shape[0] // SUBLANES_F32, SUBLANES_F32, width), axis=0)
        pv = _dot(vtb, p.astype(BF16))
        if first:
            l_ref[...] = psum
            acc_ref[...] = pv
        elif running_max:
            l_ref[...] = alpha * l_ref[...] + psum
            acc_ref[...] = alpha * acc_ref[...] + pv
        else:
            l_ref[...] += psum
            acc_ref[...] += pv

    block(km_ref[0], vmt_ref[...], True)

    def body(j, carry):
        start = pl.multiple_of(j * tk, tk)
        block(k_ref[0, pl.ds(start, tk), :], vt_ref[:, pl.ds(start, tk)], False)
        return carry

    unroll = 1 if running_max else next(u for u in (ATTN_KV_UNROLL, 4, 2, 1) if nkv % u == 0)
    lax.fori_loop(0, nkv, body, 0, unroll=unroll)

    o = acc_ref[...] * (1.0 / jnp.sum(l_ref[...], axis=0, keepdims=True))
    for g in range(Q_PER_KV):
        o_ref[0, :, g * HEAD_DIM:(g + 1) * HEAD_DIM] = o[:, g * tq:(g + 1) * tq].T.astype(BF16)


def _attention(bound, q, k, vt, km, vmt, batch, tq, tk, shared_q, running_max):
    n = k.shape[1] // batch
    nq = q.shape[1] if shared_q else q.shape[1] // batch
    nkv = n // tk
    nqt = nq // tq
    width = Q_PER_KV * tq
    if shared_q:
        q_map = lambda bi, h, i: (h, i, 0)
    else:
        q_map = lambda bi, h, i: (h, bi * nqt + i, 0)
    scratch = [pltpu.VMEM((SUBLANES_F32, width), F32), pltpu.VMEM((HEAD_DIM, width), F32)]
    if running_max:
        scratch.append(pltpu.VMEM((1, width), F32))
    return pl.pallas_call(
        functools.partial(_attn_kernel, nkv=nkv, tk=tk, running_max=running_max),
        grid=(batch, N_KV_HEADS, nqt),
        in_specs=[pl.BlockSpec(memory_space=pltpu.SMEM),
                  pl.BlockSpec((Q_PER_KV, tq, HEAD_DIM), q_map),
                  pl.BlockSpec((1, n, HEAD_DIM), lambda bi, h, i: (h, bi, 0)),
                  pl.BlockSpec((HEAD_DIM, n), lambda bi, h, i: (h, bi)),
                  pl.BlockSpec((1, N_META, HEAD_DIM), lambda bi, h, i: (h, 0, 0)),
                  pl.BlockSpec((HEAD_DIM, N_META), lambda bi, h, i: (h, 0))],
        out_specs=pl.BlockSpec((1, tq, Q_PER_KV * HEAD_DIM), lambda bi, h, i: (bi, i, h)),
        out_shape=jax.ShapeDtypeStruct((batch, nq, ATTN_WIDTH), BF16),
        scratch_shapes=scratch,
        compiler_params=pltpu.CompilerParams(
            dimension_semantics=("parallel", "parallel", "arbitrary"),
            vmem_limit_bytes=VMEM_LIMIT_BYTES),
        name="attention",
    )(bound, q, k, vt, km, vmt)


def _softplus(x):
    return jnp.maximum(x, 0.0) + jnp.log1p(jnp.exp(-jnp.abs(x)))


def _lru_prep(xc, rows, wr_ref, br_ref, wi_ref, bi_ref, lam_ref, a_ref, u_ref):
    xcb = xc.astype(BF16)
    r_parts, i_parts = [], []
    for blk in range(LRU_BLOCKS):
        xb = xcb[:, blk * LRU_BLOCK:(blk + 1) * LRU_BLOCK]
        r_parts.append(_dot(xb, wr_ref[blk]))
        i_parts.append(_dot(xb, wi_ref[blk]))
    tr = jnp.tanh(jnp.concatenate(r_parts, axis=-1) + br_ref[...])
    ti = jnp.tanh(jnp.concatenate(i_parts, axis=-1) + bi_ref[...])
    c1 = (-0.5 * LRU_C * LOG2E) * _softplus(-lam_ref[...])
    log2_a = c1 * tr + c1
    a = jnp.exp2(log2_a)
    a_ref[0:rows, :] = a
    gate2 = jnp.tanh((-LN2) * log2_a) * (1.0 + a * a)
    gate = jnp.where(gate2 > 0.0, gate2 * lax.rsqrt(gate2), 0.0)
    half_xc = 0.5 * xc
    u_ref[0:rows, :] = gate * (half_xc * ti + half_xc)


def _lru_scan(rows, a_ref, u_ref, h, out_ref, base, reverse):
    for r in (range(rows - 1, -1, -1) if reverse else range(rows)):
        h = a_ref[r:r + 1, :] * h + u_ref[r:r + 1, :]
        out_ref[pl.ds(base + r, 1), :] = h
    return h


def _lru_fwd_kernel(xc_ref, xcm_ref, wr_ref, br_ref, wi_ref, bi_ref,
                    lam_ref, hf_ref, hfm_ref, a0_ref, u0_ref, a1_ref, u1_ref, h_ref):
    j = pl.program_id(1)
    tc = xc_ref.shape[1]
    wts = (wr_ref, br_ref, wi_ref, bi_ref, lam_ref)
    nsub = tc // LRU_SUB

    @pl.when(j == 0)
    def _():
        _lru_prep(xcm_ref[0], N_META, *wts, a0_ref, u0_ref)
        h_ref[...] = _lru_scan(N_META, a0_ref, u0_ref, jnp.zeros(h_ref.shape, F32), hfm_ref.at[0], 0, False)

    @pl.when(j > 0)
    def _():
        def prep(s, a_ref, u_ref):
            start = pl.multiple_of(s * LRU_SUB, LRU_SUB)
            _lru_prep(xc_ref[0, pl.ds(start, LRU_SUB), :], LRU_SUB, *wts, a_ref, u_ref)

        prep(0, a0_ref, u0_ref)

        def pair(i, h):
            s = 2 * i
            prep(s + 1, a1_ref, u1_ref)
            h = _lru_scan(LRU_SUB, a0_ref, u0_ref, h, hf_ref.at[0], pl.multiple_of(s * LRU_SUB, LRU_SUB), False)
            prep(jnp.minimum(s + 2, nsub - 1), a0_ref, u0_ref)
            return _lru_scan(LRU_SUB, a1_ref, u1_ref, h, hf_ref.at[0],
                             pl.multiple_of((s + 1) * LRU_SUB, LRU_SUB), False)

        h_ref[...] = lax.fori_loop(0, nsub // 2, pair, h_ref[...])


def _lru_bwd_kernel(xc_ref, xcm_ref, gy_ref, gym_ref, hf_ref, hfm_ref,
                    wr_ref, br_ref, wi_ref, bi_ref, lam_ref, og_ref,
                    out_ref, outm_ref, a0_ref, u0_ref, a1_ref, u1_ref, h_ref, hb0_ref, hb1_ref, *, nc):
    j = pl.program_id(1)
    tc = xc_ref.shape[1]
    wts = (wr_ref, br_ref, wi_ref, bi_ref, lam_ref)
    nsub = tc // LRU_SUB

    def finish(hf, hb, gy):
        return _rms((hf + hb) * jax.nn.gelu(gy), og_ref[...]).astype(BF16)

    @pl.when(j == 0)
    def _():
        h_ref[...] = jnp.zeros(h_ref.shape, F32)

    @pl.when(j < nc)
    def _():
        def prep(s, a_ref, u_ref):
            start = pl.multiple_of(s * LRU_SUB, LRU_SUB)
            _lru_prep(xc_ref[0, pl.ds(start, LRU_SUB), :], LRU_SUB, *wts, a_ref, u_ref)

        def piece(s, a_ref, u_ref, hb_ref, h):
            h = _lru_scan(LRU_SUB, a_ref, u_ref, h, hb_ref, 0, True)
            rows = pl.ds(pl.multiple_of(s * LRU_SUB, LRU_SUB), LRU_SUB)
            out_ref[0, rows, :] = finish(hf_ref[0, rows, :], hb_ref[...], gy_ref[0, rows, :])
            return h

        prep(nsub - 1, a0_ref, u0_ref)

        def pair(i, h):
            s = nsub - 1 - 2 * i
            prep(s - 1, a1_ref, u1_ref)
            h = piece(s, a0_ref, u0_ref, hb0_ref, h)
            prep(jnp.maximum(s - 2, 0), a0_ref, u0_ref)
            return piece(s - 1, a1_ref, u1_ref, hb1_ref, h)

        h_ref[...] = lax.fori_loop(0, nsub // 2, pair, h_ref[...], unroll=2 if nsub % 4 == 0 else 1)

    @pl.when(j == nc)
    def _():
        _lru_prep(xcm_ref[0], N_META, *wts, a0_ref, u0_ref)
        h_ref[...] = _lru_scan(N_META, a0_ref, u0_ref, h_ref[...], hb0_ref, 0, True)
        outm_ref[0] = finish(hfm_ref[0], hb0_ref[0:N_META, :], gym_ref[...])


def _lru(xc, gy, xcm, gym, w_r, b_r, w_i, b_i, lam, out_gain, tc):
    b, n, c = xc.shape
    nc = n // tc
    wspec = _const_spec((LRU_BLOCKS, LRU_BLOCK, LRU_BLOCK))
    vspec = _const_spec((1, c))
    pspecs = [wspec, vspec, wspec, vspec, vspec]
    bmspec = pl.BlockSpec((1, N_META, c), lambda bi, j: (bi, 0, 0))
    slots = [pltpu.VMEM((LRU_SUB, c), F32)] * 4
    assert tc % (2 * LRU_SUB) == 0 and LRU_SUB >= N_META

    def chunk_f(j):
        return jnp.maximum(j - 1, 0)

    hf, hfm = pl.pallas_call(
        _lru_fwd_kernel,
        grid=(b, nc + 1),
        in_specs=[pl.BlockSpec((1, tc, c), lambda bi, j: (bi, chunk_f(j), 0)), bmspec] + pspecs,
        out_specs=[pl.BlockSpec((1, tc, c), lambda bi, j: (bi, chunk_f(j), 0)), bmspec],
        out_shape=[jax.ShapeDtypeStruct((b, n, c), F32), jax.ShapeDtypeStruct((b, N_META, c), F32)],
        scratch_shapes=slots + [pltpu.VMEM((1, c), F32)],
        compiler_params=pltpu.CompilerParams(dimension_semantics=("parallel", "arbitrary"),
                                             vmem_limit_bytes=VMEM_LIMIT_BYTES),
        name="lru_fwd",
    )(xc, xcm, w_r[0], b_r[0:1], w_i[0], b_i[0:1], lam[0:1])

    def chunk_b(j):
        return jnp.maximum(nc - 1 - j, 0)

    chunk_spec = pl.BlockSpec((1, tc, c), lambda bi, j: (bi, chunk_b(j), 0))
    out, outm = pl.pallas_call(
        functools.partial(_lru_bwd_kernel, nc=nc),
        grid=(b, nc + 1),
        in_specs=[chunk_spec, bmspec, chunk_spec, _const_spec((N_META, c)), chunk_spec, bmspec] + pspecs + [vspec],
        out_specs=[chunk_spec, bmspec],
        out_shape=[jax.ShapeDtypeStruct((b, n, c), BF16), jax.ShapeDtypeStruct((b, N_META, c), BF16)],
        scratch_shapes=slots + [pltpu.VMEM((1, c), F32),
                                pltpu.VMEM((LRU_SUB, c), F32), pltpu.VMEM((LRU_SUB, c), F32)],
        compiler_params=pltpu.CompilerParams(dimension_semantics=("parallel", "arbitrary"),
                                             vmem_limit_bytes=VMEM_LIMIT_BYTES),
        name="lru_bwd",
    )(xc, xcm, gy, gym, hf, hfm, w_r[1], b_r[1:2], w_i[1], b_i[1:2], lam[1:2], out_gain)
    return out, outm


def _out_proj_kernel(attn_ref, lru_ref, res_ref, ag_ref, wa_ref, wl_ref, pg_ref, fg_ref, h1_ref, hn_ref):
    tm = attn_ref.shape[0]
    parts = OUT_PROJ_PARTS if tm % (OUT_PROJ_PARTS * LANES) == 0 else 1
    part = tm // parts
    for s in range(parts):
        rows = slice(s * part, (s + 1) * part)
        an = _rms(attn_ref[rows, :].astype(F32), ag_ref[...]).astype(BF16)
        mixed = _dot(an, wa_ref[...]) + _dot(lru_ref[rows, :], wl_ref[...])
        h1 = res_ref[rows, :] + _rms(mixed, pg_ref[...])
        h1_ref[rows, :] = h1
        hn_ref[rows, :] = _rms(h1, fg_ref[...]).astype(BF16)


def _out_proj(attn2d, lru2d, res2d, attn_gain, w_a, w_l, post_gain, ffn_gain, tm):
    rows = attn2d.shape[0]
    row_spec = lambda w: pl.BlockSpec((tm, w), lambda i: (i, 0))
    wspec = pl.BlockSpec((ATTN_WIDTH, D_MODEL), lambda i: (0, 0), pipeline_mode=pl.Buffered(1))
    return pl.pallas_call(
        _out_proj_kernel,
        grid=(rows // tm,),
        in_specs=[row_spec(ATTN_WIDTH), row_spec(LRU_WIDTH), row_spec(D_MODEL), _const_spec((1, ATTN_WIDTH)),
                  wspec, wspec, _const_spec((1, D_MODEL)), _const_spec((1, D_MODEL))],
        out_specs=[row_spec(D_MODEL), row_spec(D_MODEL)],
        out_shape=[jax.ShapeDtypeStruct((rows, D_MODEL), F32), jax.ShapeDtypeStruct((rows, D_MODEL), BF16)],
        compiler_params=pltpu.CompilerParams(dimension_semantics=("parallel",),
                                             vmem_limit_bytes=VMEM_LIMIT_BYTES),
        name="out_proj",
    )(attn2d, lru2d, res2d, attn_gain, w_a, w_l, post_gain, ffn_gain)


def _ffn_kernel(hn_ref, prev_ref, next_ref, hm_ref, h1_ref, wg_ref, wv_ref, wd_ref, cw_ref, cb_ref, og_ref,
                out_ref, ext_ref, *, nt, nf):
    i = pl.program_id(1)
    f = pl.program_id(2)
    tm = hn_ref.shape[1]
    halo = SUBLANES_BF16
    acc_ref = out_ref.at[0]

    @pl.when(f == 0)
    def _():
        ext_ref[0:halo, :] = jnp.where(i == 0, hm_ref[0], prev_ref[0])
        ext_ref[halo:halo + tm, :] = hn_ref[0]
        ext_ref[halo + tm:2 * halo + tm, :] = jnp.where(i == nt - 1, jnp.zeros_like(next_ref[0]), next_ref[0])
        acc_ref[...] = jnp.zeros(acc_ref.shape, F32)

    n_ext = tm + 2 * halo
    gfull = _dot(ext_ref[...], wg_ref[...])
    g = jnp.broadcast_to(cb_ref[...], (tm, gfull.shape[1]))
    for kk in range(FFN_CONV_W):
        shift = (FFN_CONV_W // 2 - kk) % n_ext
        gk = gfull if shift == 0 else pltpu.roll(gfull, shift, axis=0)
        g = g + gk[halo:halo + tm] * cw_ref[kk:kk + 1, :]
    val = _dot(ext_ref[halo:halo + tm, :], wv_ref[...])
    act = (jax.nn.silu(g) * val).astype(BF16)
    acc_ref[...] += _dot(act, wd_ref[...])

    @pl.when(f == nf - 1)
    def _():
        acc_ref[...] = h1_ref[0] + _rms(acc_ref[...], og_ref[...])


def _ffn(hn, hnm, h1, w_up, w_down, conv_w, conv_b, out_gain, tm, fc):
    b, n, d = hn.shape
    nt = n // tm
    nf = FFN_DIM // fc
    halo = SUBLANES_BF16
    th = tm // halo
    nh = n // halo
    return pl.pallas_call(
        functools.partial(_ffn_kernel, nt=nt, nf=nf),
        grid=(b, nt, nf),
        in_specs=[pl.BlockSpec((1, tm, d), lambda bi, i, f: (bi, i, 0)),
                  pl.BlockSpec((1, halo, d), lambda bi, i, f: (bi, jnp.maximum(i * th - 1, 0), 0)),
                  pl.BlockSpec((1, halo, d), lambda bi, i, f: (bi, jnp.minimum((i + 1) * th, nh - 1), 0)),
                  pl.BlockSpec((1, N_META, d), lambda bi, i, f: (bi, 0, 0)),
                  pl.BlockSpec((1, tm, d), lambda bi, i, f: (bi, i, 0)),
                  pl.BlockSpec((d, fc), lambda bi, i, f: (0, f)),
                  pl.BlockSpec((d, fc), lambda bi, i, f: (0, f + nf)),
                  pl.BlockSpec((fc, d), lambda bi, i, f: (f, 0)),
                  pl.BlockSpec((FFN_CONV_W, fc), lambda bi, i, f: (0, f)),
                  pl.BlockSpec((1, fc), lambda bi, i, f: (0, f)),
                  _const_spec((1, d))],
        out_specs=pl.BlockSpec((1, tm, d), lambda bi, i, f: (bi, i, 0)),
        out_shape=jax.ShapeDtypeStruct((b, n, d), F32),
        scratch_shapes=[pltpu.VMEM((tm + 2 * halo, d), BF16)],
        compiler_params=pltpu.CompilerParams(
            dimension_semantics=("parallel", "parallel", "arbitrary"),
            vmem_limit_bytes=VMEM_LIMIT_BYTES),
        name="conv_ffn",
    )(hn, hn, hn, hnm, h1, w_up, w_up, w_down, conv_w, conv_b, out_gain)


def _rope_tables(n):
    rows = n // GRID_W
    t_row = jnp.repeat(jnp.arange(rows), GRID_W).astype(F32)
    t_col = jnp.tile(jnp.arange(GRID_W), rows).astype(F32)
    half = HEAD_DIM // 2
    inv = ROPE_THETA ** (-jnp.arange(0, half, 2, dtype=F32) / half)
    ang = jnp.concatenate([t_row[:, None] * inv, t_col[:, None] * inv], axis=-1)
    cos, sin = jnp.cos(ang), jnp.sin(ang)
    return jnp.concatenate([cos, cos], axis=-1), jnp.concatenate([-sin, sin], axis=-1)


_TILES = dict(tm=512, tq=512, tk=1024, tc=1024, tf=512, fc=512)


def _pick(n, pref):
    t = pref
    while n % t:
        t //= 2
    return t


def _trunk(x, meta, mp, p):
    b, n, d = x.shape
    rows = b * n
    cos_t, sin_t = _rope_tables(n)
    tm = _pick(n, _TILES["tm"])
    nt = n // tm
    q, k, vt, xc, gy, xc_head = _in_proj(x.reshape(rows, d), meta, nt, p["norm_pre_mix"], p["w_in"], p["q_norm"],
                                         p["k_norm"], cos_t, sin_t, p["lru_conv_w"], p["lru_conv_b"], tm)
    head = xc_head.reshape(b, nt, SUBLANES_BF16, LRU_WIDTH)[:, 0, SUBLANES_BF16 - 2:, :]
    xc_m = jnp.concatenate([jnp.broadcast_to(mp["xc"][None, :N_META - 2], (b, N_META - 2, LRU_WIDTH)), head], axis=1)
    tq = _pick(n, _TILES["tq"])
    tk = _pick(n, _TILES["tk"])
    bound = mp["bound"]

    def attend(running_max):
        def run(bound, q, qm, k, vt, km, vmt):
            real = _attention(bound, q, k, vt, km, vmt, b, tq, tk, False, running_max)
            meta_rows = _attention(bound, qm, k, vt, km, vmt, b, META_Q_ROWS, tk, True, running_max)
            return real, meta_rows
        return run

    attn, attn_m = lax.cond(2.0 * bound[0] <= MAX_FIXED_SHIFT_RANGE, attend(False), attend(True),
                            bound, q, mp["q"][:, :META_Q_ROWS], k, vt, mp["km"], mp["vmt"])
    attn_m = attn_m[:, :N_META]
    lru, lru_m = _lru(xc.reshape(b, n, LRU_WIDTH), gy.reshape(b, n, LRU_WIDTH), xc_m, mp["gy"],
                      p["lru_w_r"], p["lru_b_r"], p["lru_w_i"], p["lru_b_i"],
                      p["lru_lambda"], p["lru_out_norm"], _pick(n, _TILES["tc"]))
    op = (p["attn_out_norm"], p["w_out_a"], p["w_out_l"], p["norm_post_mix"], p["norm_pre_ffn"])
    h1, hn = _out_proj(attn.reshape(rows, ATTN_WIDTH), lru.reshape(rows, LRU_WIDTH), x.reshape(rows, d), *op, tm)
    res_m = jnp.broadcast_to(meta[None], (b, N_META, d)).reshape(b * N_META, d)
    _, hn_m = _out_proj(attn_m.reshape(b * N_META, ATTN_WIDTH), lru_m.reshape(b * N_META, LRU_WIDTH), res_m,
                        *op, N_META)
    return _ffn(hn.reshape(b, n, d), hn_m.reshape(b, N_META, d), h1.reshape(b, n, d), p["w_up"], p["w_down"],
                p["ffn_conv_w"], p["ffn_conv_b"], p["norm_post_ffn"], _pick(n, _TILES["tf"]), _TILES["fc"])


def _head_perm():
    q4 = HEAD_DIM // 4
    idx = jnp.arange(HEAD_DIM).reshape(2, 2, q4)
    return idx.transpose(1, 0, 2).reshape(HEAD_DIM)


def kernel(x_prompt, x_sample, meta_tokens, norm_pre_mix, w_in, q_norm, k_norm, lru_conv_w, lru_conv_b,
           lru_w_r, lru_b_r, lru_w_i, lru_b_i, lru_lambda, attn_out_norm, lru_out_norm, w_out,
           norm_post_mix, norm_pre_ffn, w_up, ffn_conv_w, ffn_conv_b, w_down, norm_post_ffn):
    perm = _head_perm()
    n_rot = N_Q_HEADS + N_KV_HEADS
    cols = (jnp.arange(n_rot)[:, None] * HEAD_DIM + perm[None, :]).reshape(-1)
    cols = jnp.concatenate([cols, jnp.arange(n_rot * HEAD_DIM, IN_WIDTH)])
    w_out_b = w_out[0].astype(BF16)
    p = {
        "norm_pre_mix": norm_pre_mix[0][None], "w_in": w_in[0][:, cols].astype(BF16),
        "q_norm": q_norm[0][perm][None], "k_norm": k_norm[0][perm][None],
        "lru_conv_w": lru_conv_w[0], "lru_conv_b": lru_conv_b[0][None],
        "lru_w_r": (0.5 * lru_w_r[0]).astype(BF16), "lru_b_r": 0.5 * lru_b_r[0],
        "lru_w_i": (0.5 * lru_w_i[0]).astype(BF16), "lru_b_i": 0.5 * lru_b_i[0], "lru_lambda": lru_lambda[0],
        "attn_out_norm": attn_out_norm[0][None], "lru_out_norm": lru_out_norm[0][None],
        "w_out_a": w_out_b[:ATTN_WIDTH], "w_out_l": w_out_b[ATTN_WIDTH:],
        "norm_post_mix": norm_post_mix[0][None], "norm_pre_ffn": norm_pre_ffn[0][None],
        "w_up": w_up[0].astype(BF16), "ffn_conv_w": ffn_conv_w[0], "ffn_conv_b": ffn_conv_b[0][None],
        "w_down": w_down[0].astype(BF16), "norm_post_ffn": norm_post_ffn[0][None],
    }
    ones = jnp.ones((LANES, HEAD_DIM), F32)
    meta_pad = jnp.pad(meta_tokens, ((0, LANES - N_META), (0, 0)))
    qm, km, vtm, xcm, gym, _ = _in_proj(meta_pad, jnp.zeros((SUBLANES_BF16, D_MODEL), F32), 1, p["norm_pre_mix"],
                                        p["w_in"], p["q_norm"], p["k_norm"], ones, jnp.zeros_like(ones),
                                        p["lru_conv_w"], p["lru_conv_b"], LANES)
    bound = (HEAD_DIM * Q_PRESCALE * BF16_ROUNDING_SLACK) * jnp.max(jnp.abs(q_norm[0])) * jnp.max(jnp.abs(k_norm[0]))
    mp = {"q": qm, "km": km[:, :N_META], "vmt": vtm[:, :N_META], "xc": xcm[:N_META], "gy": gym[:N_META],
          "bound": bound.reshape(1).astype(F32)}
    return _trunk(x_prompt, meta_tokens, mp, p), _trunk(x_sample, meta_tokens, mp, p)
```

```python
import functools

import jax
import jax.numpy as jnp
from jax import lax
from jax.experimental import pallas as pl
from jax.experimental.pallas import tpu as pltpu

D_MODEL = 2048
N_META = 16
GRID_W = 64
HEAD_DIM = 128
N_Q_HEADS = 8
N_KV_HEADS = 2
Q_PER_KV = N_Q_HEADS // N_KV_HEADS
ATTN_WIDTH = N_Q_HEADS * HEAD_DIM
KV_WIDTH = N_KV_HEADS * HEAD_DIM
LRU_WIDTH = D_MODEL - ATTN_WIDTH
LRU_BLOCKS = 8
LRU_BLOCK = LRU_WIDTH // LRU_BLOCKS
LRU_CONV_W = 4
LRU_C = 8.0
LRU_SUB = 64
OUT_PROJ_PARTS = 2
ATTN_KV_UNROLL = 8
META_Q_ROWS = 32
IN_WIDTH = ATTN_WIDTH + 2 * KV_WIDTH + 2 * LRU_WIDTH
FFN_DIM = 5632
FFN_CONV_W = 3
ROPE_THETA = 10000.0
EPS = 1e-6
ATTN_SCALE = HEAD_DIM ** -0.5
LOG2E = 1.4426950408889634
LN2 = 0.6931471805599453
Q_PRESCALE = ATTN_SCALE * LOG2E
MAX_FIXED_SHIFT_RANGE = 120.0
BF16_ROUNDING_SLACK = 1.0 + 2.0 ** -6

LANES = 128
SUBLANES_F32 = 8
SUBLANES_BF16 = 16
VMEM_LIMIT_BYTES = 56 * 1024 * 1024

F32 = jnp.float32
BF16 = jnp.bfloat16


def _rms(x, g):
    ms = jnp.mean(x * x, axis=-1, keepdims=True)
    return x * lax.rsqrt(ms + EPS) * g


def _dot(a, b):
    return jnp.dot(a, b, preferred_element_type=F32)


def _const_spec(shape):
    zeros = (0,) * len(shape)
    return pl.BlockSpec(shape, lambda *_: zeros)


def _in_proj_kernel(x_ref, xprev_ref, xnext_ref, xhead_ref, g_ref, w_ref, qg_ref, kg_ref, cos_ref, sin_ref,
                    cw_ref, cb_ref, q_ref, k_ref, vt_ref, xc_ref, gy_ref, xch_ref, *, nt):
    tm = x_ref.shape[0]
    halo = SUBLANES_BF16
    pos = pl.program_id(0) % nt
    xprev = jnp.where(pos == 0, xhead_ref[...], xprev_ref[...])
    xnext = jnp.where(pos == nt - 1, jnp.zeros_like(xnext_ref[...]), xnext_ref[...])
    xn = _rms(x_ref[...], g_ref[...]).astype(BF16)
    xn_ext = jnp.concatenate([_rms(xprev, g_ref[...]).astype(BF16), xn, _rms(xnext, g_ref[...]).astype(BF16)], axis=0)
    cos = cos_ref[...]
    sin = sin_ref[...]

    def rope_head(zh, gain):
        y = _rms(zh, gain)
        return y * cos + pltpu.roll(y, HEAD_DIM // 2, axis=1) * sin

    o3 = ATTN_WIDTH + 2 * KV_WIDTH
    zq = _dot(xn, w_ref[:, 0:ATTN_WIDTH])
    for h in range(N_Q_HEADS):
        sl = slice(h * HEAD_DIM, (h + 1) * HEAD_DIM)
        q_ref[h] = (rope_head(zq[:, sl], qg_ref[...]) * Q_PRESCALE).astype(BF16)
    xr_ext = _dot(xn_ext, w_ref[:, o3:o3 + LRU_WIDTH])
    n_ext = tm + 2 * halo
    xc_ext = jnp.broadcast_to(cb_ref[...], (n_ext, LRU_WIDTH))
    for kk in range(LRU_CONV_W):
        shift = (LRU_CONV_W // 2 - kk) % n_ext
        xk = xr_ext if shift == 0 else pltpu.roll(xr_ext, shift, axis=0)
        xc_ext = xc_ext + xk * cw_ref[kk:kk + 1, :]
    xc_ref[...] = xc_ext[halo:halo + tm]
    xch_ref[...] = xc_ext[0:halo]
    zkv = _dot(xn, w_ref[:, ATTN_WIDTH:ATTN_WIDTH + 2 * KV_WIDTH])
    for h in range(N_KV_HEADS):
        sl = slice(h * HEAD_DIM, (h + 1) * HEAD_DIM)
        k_ref[h] = rope_head(zkv[:, sl], kg_ref[...]).astype(BF16)
    vt_ref[...] = zkv[:, KV_WIDTH:2 * KV_WIDTH].T.astype(BF16)
    gy_ref[...] = _dot(xn, w_ref[:, o3 + LRU_WIDTH:o3 + 2 * LRU_WIDTH])


def _in_proj(x2d, xhead, nt, gain, w_in, q_gain, k_gain, cos_t, sin_t, conv_w, conv_b, tm):
    rows = x2d.shape[0]
    n_tab = cos_t.shape[0] // tm
    halo = SUBLANES_BF16
    th = tm // halo
    nh = rows // halo
    row_spec = lambda w: pl.BlockSpec((tm, w), lambda i: (i, 0))
    tab_spec = pl.BlockSpec((tm, HEAD_DIM), lambda i: (i % n_tab, 0))
    return pl.pallas_call(
        functools.partial(_in_proj_kernel, nt=nt),
        grid=(rows // tm,),
        in_specs=[row_spec(D_MODEL),
                  pl.BlockSpec((halo, D_MODEL), lambda i: (jnp.maximum(i * th - 1, 0), 0)),
                  pl.BlockSpec((halo, D_MODEL), lambda i: (jnp.minimum((i + 1) * th, nh - 1), 0)),
                  _const_spec((halo, D_MODEL)), _const_spec((1, D_MODEL)),
                  pl.BlockSpec((D_MODEL, IN_WIDTH), lambda i: (0, 0), pipeline_mode=pl.Buffered(1)),
                  _const_spec((1, HEAD_DIM)), _const_spec((1, HEAD_DIM)), tab_spec, tab_spec,
                  _const_spec((LRU_CONV_W, LRU_WIDTH)), _const_spec((1, LRU_WIDTH))],
        out_specs=[pl.BlockSpec((N_Q_HEADS, tm, HEAD_DIM), lambda i: (0, i, 0)),
                   pl.BlockSpec((N_KV_HEADS, tm, HEAD_DIM), lambda i: (0, i, 0)),
                   pl.BlockSpec((KV_WIDTH, tm), lambda i: (0, i)),
                   row_spec(LRU_WIDTH), row_spec(LRU_WIDTH),
                   pl.BlockSpec((halo, LRU_WIDTH), lambda i: (i, 0))],
        out_shape=[jax.ShapeDtypeStruct((N_Q_HEADS, rows, HEAD_DIM), BF16),
                   jax.ShapeDtypeStruct((N_KV_HEADS, rows, HEAD_DIM), BF16),
                   jax.ShapeDtypeStruct((KV_WIDTH, rows), BF16),
                   jax.ShapeDtypeStruct((rows, LRU_WIDTH), F32),
                   jax.ShapeDtypeStruct((rows, LRU_WIDTH), F32),
                   jax.ShapeDtypeStruct((rows // tm * halo, LRU_WIDTH), F32)],
        compiler_params=pltpu.CompilerParams(dimension_semantics=("parallel",),
                                             vmem_limit_bytes=VMEM_LIMIT_BYTES),
        name="in_proj",
    )(x2d, x2d, x2d, xhead, gain, w_in, q_gain, k_gain, cos_t, sin_t, conv_w, conv_b)


def _attn_kernel(bound_ref, q_ref, k_ref, vt_ref, km_ref, vmt_ref, o_ref, l_ref, acc_ref, *m_scratch,
                 nkv, tk, running_max):
    tq = q_ref.shape[1]
    width = Q_PER_KV * tq
    qs = q_ref[...].reshape(width, HEAD_DIM)
    nt = (((1,), (1,)), ((), ()))

    def block(kb, vtb, first):
        st = lax.dot_general(kb, qs, nt, preferred_element_type=F32)
        if running_max:
            m_ref, = m_scratch
            m_new = jnp.max(st, axis=0, keepdims=True)
            if not first:
                m_prev = m_ref[...]
                m_new = jnp.maximum(m_prev, m_new)
                alpha = jnp.exp2(m_prev - m_new)
            m_ref[...] = m_new
            p = jnp.exp2(st - m_new)
        else:
            p = jnp.exp2(st - bound_ref[0])
        psum = jnp.sum(p.reshape(p.shape[0] // SUBLANES_F32, SUBLANES_F32, width), axis=0)
        pv = _dot(vtb, p.astype(BF16))
        if first:
            l_ref[...] = psum
            acc_ref[...] = pv
        elif running_max:
            l_ref[...] = alpha * l_ref[...] + psum
            acc_ref[...] = alpha * acc_ref[...] + pv
        else:
            l_ref[...] += psum
            acc_ref[...] += pv

    block(km_ref[0], vmt_ref[...], True)

    def body(j, carry):
        start = pl.multiple_of(j * tk, tk)
        block(k_ref[0, pl.ds(start, tk), :], vt_ref[:, pl.ds(start, tk)], False)
        return carry

    unroll = 1 if running_max else next(u for u in (ATTN_KV_UNROLL, 4, 2, 1) if nkv % u == 0)
    lax.fori_loop(0, nkv, body, 0, unroll=unroll)

    o = acc_ref[...] * (1.0 / jnp.sum(l_ref[...], axis=0, keepdims=True))
    for g in range(Q_PER_KV):
        o_ref[0, :, g * HEAD_DIM:(g + 1) * HEAD_DIM] = o[:, g * tq:(g + 1) * tq].T.astype(BF16)


def _attention(bound, q, k, vt, km, vmt, batch, tq, tk, shared_q, running_max):
    n = k.shape[1] // batch
    nq = q.shape[1] if shared_q else q.shape[1] // batch
    nkv = n // tk
    nqt = nq // tq
    width = Q_PER_KV * tq
    if shared_q:
        q_map = lambda bi, h, i: (h, i, 0)
    else:
        q_map = lambda bi, h, i: (h, bi * nqt + i, 0)
    scratch = [pltpu.VMEM((SUBLANES_F32, width), F32), pltpu.VMEM((HEAD_DIM, width), F32)]
    if running_max:
        scratch.append(pltpu.VMEM((1, width), F32))
    return pl.pallas_call(
        functools.partial(_attn_kernel, nkv=nkv, tk=tk, running_max=running_max),
        grid=(batch, N_KV_HEADS, nqt),
        in_specs=[pl.BlockSpec(memory_space=pltpu.SMEM),
                  pl.BlockSpec((Q_PER_KV, tq, HEAD_DIM), q_map),
                  pl.BlockSpec((1, n, HEAD_DIM), lambda bi, h, i: (h, bi, 0)),
                  pl.BlockSpec((HEAD_DIM, n), lambda bi, h, i: (h, bi)),
                  pl.BlockSpec((1, N_META, HEAD_DIM), lambda bi, h, i: (h, 0, 0)),
                  pl.BlockSpec((HEAD_DIM, N_META), lambda bi, h, i: (h, 0))],
        out_specs=pl.BlockSpec((1, tq, Q_PER_KV * HEAD_DIM), lambda bi, h, i: (bi, i, h)),
        out_shape=jax.ShapeDtypeStruct((batch, nq, ATTN_WIDTH), BF16),
        scratch_shapes=scratch,
        compiler_params=pltpu.CompilerParams(
            dimension_semantics=("parallel", "parallel", "arbitrary"),
            vmem_limit_bytes=VMEM_LIMIT_BYTES),
        name="attention",
    )(bound, q, k, vt, km, vmt)


def _softplus(x):
    return jnp.maximum(x, 0.0) + jnp.log1p(jnp.exp(-jnp.abs(x)))


def _lru_prep(xc, rows, wr_ref, br_ref, wi_ref, bi_ref, lam_ref, a_ref, u_ref):
    xcb = xc.astype(BF16)
    r_parts, i_parts = [], []
    for blk in range(LRU_BLOCKS):
        xb = xcb[:, blk * LRU_BLOCK:(blk + 1) * LRU_BLOCK]
        r_parts.append(_dot(xb, wr_ref[blk]))
        i_parts.append(_dot(xb, wi_ref[blk]))
    tr = jnp.tanh(jnp.concatenate(r_parts, axis=-1) + br_ref[...])
    ti = jnp.tanh(jnp.concatenate(i_parts, axis=-1) + bi_ref[...])
    c1 = (-0.5 * LRU_C * LOG2E) * _softplus(-lam_ref[...])
    log2_a = c1 * tr + c1
    a = jnp.exp2(log2_a)
    a_ref[0:rows, :] = a
    gate2 = jnp.tanh((-LN2) * log2_a) * (1.0 + a * a)
    gate = jnp.where(gate2 > 0.0, gate2 * lax.rsqrt(gate2), 0.0)
    half_xc = 0.5 * xc
    u_ref[0:rows, :] = gate * (half_xc * ti + half_xc)


def _lru_scan(rows, a_ref, u_ref, h, out_ref, base, reverse):
    for r in (range(rows - 1, -1, -1) if reverse else range(rows)):
        h = a_ref[r:r + 1, :] * h + u_ref[r:r + 1, :]
        out_ref[pl.ds(base + r, 1), :] = h
    return h


def _lru_fwd_kernel(xc_ref, xcm_ref, wr_ref, br_ref, wi_ref, bi_ref,
                    lam_ref, hf_ref, hfm_ref, a0_ref, u0_ref, a1_ref, u1_ref, h_ref):
    j = pl.program_id(1)
    tc = xc_ref.shape[1]
    wts = (wr_ref, br_ref, wi_ref, bi_ref, lam_ref)
    nsub = tc // LRU_SUB

    @pl.when(j == 0)
    def _():
        _lru_prep(xcm_ref[0], N_META, *wts, a0_ref, u0_ref)
        h_ref[...] = _lru_scan(N_META, a0_ref, u0_ref, jnp.zeros(h_ref.shape, F32), hfm_ref.at[0], 0, False)

    @pl.when(j > 0)
    def _():
        def prep(s, a_ref, u_ref):
            start = pl.multiple_of(s * LRU_SUB, LRU_SUB)
            _lru_prep(xc_ref[0, pl.ds(start, LRU_SUB), :], LRU_SUB, *wts, a_ref, u_ref)

        prep(0, a0_ref, u0_ref)

        def pair(i, h):
            s = 2 * i
            prep(s + 1, a1_ref, u1_ref)
            h = _lru_scan(LRU_SUB, a0_ref, u0_ref, h, hf_ref.at[0], pl.multiple_of(s * LRU_SUB, LRU_SUB), False)
            prep(jnp.minimum(s + 2, nsub - 1), a0_ref, u0_ref)
            return _lru_scan(LRU_SUB, a1_ref, u1_ref, h, hf_ref.at[0],
                             pl.multiple_of((s + 1) * LRU_SUB, LRU_SUB), False)

        h_ref[...] = lax.fori_loop(0, nsub // 2, pair, h_ref[...])


def _lru_bwd_kernel(xc_ref, xcm_ref, gy_ref, gym_ref, hf_ref, hfm_ref,
                    wr_ref, br_ref, wi_ref, bi_ref, lam_ref, og_ref,
                    out_ref, outm_ref, a0_ref, u0_ref, a1_ref, u1_ref, h_ref, hb0_ref, hb1_ref, *, nc):
    j = pl.program_id(1)
    tc = xc_ref.shape[1]
    wts = (wr_ref, br_ref, wi_ref, bi_ref, lam_ref)
    nsub = tc // LRU_SUB

    def finish(hf, hb, gy):
        return _rms((hf + hb) * jax.nn.gelu(gy), og_ref[...]).astype(BF16)

    @pl.when(j == 0)
    def _():
        h_ref[...] = jnp.zeros(h_ref.shape, F32)

    @pl.when(j < nc)
    def _():
        def prep(s, a_ref, u_ref):
            start = pl.multiple_of(s * LRU_SUB, LRU_SUB)
            _lru_prep(xc_ref[0, pl.ds(start, LRU_SUB), :], LRU_SUB, *wts, a_ref, u_ref)

        def piece(s, a_ref, u_ref, hb_ref, h):
            h = _lru_scan(LRU_SUB, a_ref, u_ref, h, hb_ref, 0, True)
            rows = pl.ds(pl.multiple_of(s * LRU_SUB, LRU_SUB), LRU_SUB)
            out_ref[0, rows, :] = finish(hf_ref[0, rows, :], hb_ref[...], gy_ref[0, rows, :])
            return h

        prep(nsub - 1, a0_ref, u0_ref)

        def pair(i, h):
            s = nsub - 1 - 2 * i
            prep(s - 1, a1_ref, u1_ref)
            h = piece(s, a0_ref, u0_ref, hb0_ref, h)
            prep(jnp.maximum(s - 2, 0), a0_ref, u0_ref)
            return piece(s - 1, a1_ref, u1_ref, hb1_ref, h)

        h_ref[...] = lax.fori_loop(0, nsub // 2, pair, h_ref[...], unroll=2 if nsub % 4 == 0 else 1)

    @pl.when(j == nc)
    def _():
        _lru_prep(xcm_ref[0], N_META, *wts, a0_ref, u0_ref)
        h_ref[...] = _lru_scan(N_META, a0_ref, u0_ref, h_ref[...], hb0_ref, 0, True)
        outm_ref[0] = finish(hfm_ref[0], hb0_ref[0:N_META, :], gym_ref[...])


def _lru(xc, gy, xcm, gym, w_r, b_r, w_i, b_i, lam, out_gain, tc):
    b, n, c = xc.shape
    nc = n // tc
    wspec = _const_spec((LRU_BLOCKS, LRU_BLOCK, LRU_BLOCK))
    vspec = _const_spec((1, c))
    pspecs = [wspec, vspec, wspec, vspec, vspec]
    bmspec = pl.BlockSpec((1, N_META, c), lambda bi, j: (bi, 0, 0))
    slots = [pltpu.VMEM((LRU_SUB, c), F32)] * 4
    assert tc % (2 * LRU_SUB) == 0 and LRU_SUB >= N_META

    def chunk_f(j):
        return jnp.maximum(j - 1, 0)

    hf, hfm = pl.pallas_call(
        _lru_fwd_kernel,
        grid=(b, nc + 1),
        in_specs=[pl.BlockSpec((1, tc, c), lambda bi, j: (bi, chunk_f(j), 0)), bmspec] + pspecs,
        out_specs=[pl.BlockSpec((1, tc, c), lambda bi, j: (bi, chunk_f(j), 0)), bmspec],
        out_shape=[jax.ShapeDtypeStruct((b, n, c), F32), jax.ShapeDtypeStruct((b, N_META, c), F32)],
        scratch_shapes=slots + [pltpu.VMEM((1, c), F32)],
        compiler_params=pltpu.CompilerParams(dimension_semantics=("parallel", "arbitrary"),
                                             vmem_limit_bytes=VMEM_LIMIT_BYTES),
        name="lru_fwd",
    )(xc, xcm, w_r[0], b_r[0:1], w_i[0], b_i[0:1], lam[0:1])

    def chunk_b(j):
        return jnp.maximum(nc - 1 - j, 0)

    chunk_spec = pl.BlockSpec((1, tc, c), lambda bi, j: (bi, chunk_b(j), 0))
    out, outm = pl.pallas_call(
        functools.partial(_lru_bwd_kernel, nc=nc),
        grid=(b, nc + 1),
        in_specs=[chunk_spec, bmspec, chunk_spec, _const_spec((N_META, c)), chunk_spec, bmspec] + pspecs + [vspec],
        out_specs=[chunk_spec, bmspec],
        out_shape=[jax.ShapeDtypeStruct((b, n, c), BF16), jax.ShapeDtypeStruct((b, N_META, c), BF16)],
        scratch_shapes=slots + [pltpu.VMEM((1, c), F32),
                                pltpu.VMEM((LRU_SUB, c), F32), pltpu.VMEM((LRU_SUB, c), F32)],
        compiler_params=pltpu.CompilerParams(dimension_semantics=("parallel", "arbitrary"),
                                             vmem_limit_bytes=VMEM_LIMIT_BYTES),
        name="lru_bwd",
    )(xc, xcm, gy, gym, hf, hfm, w_r[1], b_r[1:2], w_i[1], b_i[1:2], lam[1:2], out_gain)
    return out, outm


def _out_proj_kernel(attn_ref, lru_ref, res_ref, ag_ref, wa_ref, wl_ref, pg_ref, fg_ref, h1_ref, hn_ref):
    tm = attn_ref.shape[0]
    parts = OUT_PROJ_PARTS if tm % (OUT_PROJ_PARTS * LANES) == 0 else 1
    part = tm // parts
    for s in range(parts):
        rows = slice(s * part, (s + 1) * part)
        an = _rms(attn_ref[rows, :].astype(F32), ag_ref[...]).astype(BF16)
        mixed = _dot(an, wa_ref[...]) + _dot(lru_ref[rows, :], wl_ref[...])
        h1 = res_ref[rows, :] + _rms(mixed, pg_ref[...])
        h1_ref[rows, :] = h1
        hn_ref[rows, :] = _rms(h1, fg_ref[...]).astype(BF16)


def _out_proj(attn2d, lru2d, res2d, attn_gain, w_a, w_l, post_gain, ffn_gain, tm):
    rows = attn2d.shape[0]
    row_spec = lambda w: pl.BlockSpec((tm, w), lambda i: (i, 0))
    wspec_a = pl.BlockSpec((ATTN_WIDTH, D_MODEL), lambda i: (0, 0), pipeline_mode=pl.Buffered(1))
    wspec_l = pl.BlockSpec((LRU_WIDTH, D_MODEL), lambda i: (1, 0), pipeline_mode=pl.Buffered(1))
    return pl.pallas_call(
        _out_proj_kernel,
        grid=(rows // tm,),
        in_specs=[row_spec(ATTN_WIDTH), row_spec(LRU_WIDTH), row_spec(D_MODEL), _const_spec((1, ATTN_WIDTH)),
                  wspec_a, wspec_l, _const_spec((1, D_MODEL)), _const_spec((1, D_MODEL))],
        out_specs=[row_spec(D_MODEL), row_spec(D_MODEL)],
        out_shape=[jax.ShapeDtypeStruct((rows, D_MODEL), F32), jax.ShapeDtypeStruct((rows, D_MODEL), BF16)],
        compiler_params=pltpu.CompilerParams(dimension_semantics=("parallel",),
                                             vmem_limit_bytes=VMEM_LIMIT_BYTES),
        name="out_proj",
    )(attn2d, lru2d, res2d, attn_gain, w_a, w_l, post_gain, ffn_gain)


def _ffn_kernel(hn_ref, prev_ref, next_ref, hm_ref, h1_ref, wg_ref, wv_ref, wd_ref, cw_ref, cb_ref, og_ref,
                out_ref, ext_ref, *, nt, nf):
    i = pl.program_id(1)
    f = pl.program_id(2)
    tm = hn_ref.shape[1]
    halo = SUBLANES_BF16
    acc_ref = out_ref.at[0]

    @pl.when(f == 0)
    def _():
        ext_ref[0:halo, :] = jnp.where(i == 0, hm_ref[0], prev_ref[0])
        ext_ref[halo:halo + tm, :] = hn_ref[0]
        ext_ref[halo + tm:2 * halo + tm, :] = jnp.where(i == nt - 1, jnp.zeros_like(next_ref[0]), next_ref[0])
        acc_ref[...] = jnp.zeros(acc_ref.shape, F32)

    n_ext = tm + 2 * halo
    gfull = _dot(ext_ref[...], wg_ref[...])
    g = jnp.broadcast_to(cb_ref[...], (tm, gfull.shape[1]))
    for kk in range(FFN_CONV_W):
        shift = (FFN_CONV_W // 2 - kk) % n_ext
        gk = gfull if shift == 0 else pltpu.roll(gfull, shift, axis=0)
        g = g + gk[halo:halo + tm] * cw_ref[kk:kk + 1, :]
    val = _dot(ext_ref[halo:halo + tm, :], wv_ref[...])
    act = (jax.nn.silu(g) * val).astype(BF16)
    acc_ref[...] += _dot(act, wd_ref[...])

    @pl.when(f == nf - 1)
    def _():
        acc_ref[...] = h1_ref[0] + _rms(acc_ref[...], og_ref[...])


def _ffn(hn, hnm, h1, w_up, w_down, conv_w, conv_b, out_gain, tm, fc):
    b, n, d = hn.shape
    nt = n // tm
    nf = FFN_DIM // fc
    halo = SUBLANES_BF16
    th = tm // halo
    nh = n // halo
    return pl.pallas_call(
        functools.partial(_ffn_kernel, nt=nt, nf=nf),
        grid=(b, nt, nf),
        in_specs=[pl.BlockSpec((1, tm, d), lambda bi, i, f: (bi, i, 0)),
                  pl.BlockSpec((1, halo, d), lambda bi, i, f: (bi, jnp.maximum(i * th - 1, 0), 0)),
                  pl.BlockSpec((1, halo, d), lambda bi, i, f: (bi, jnp.minimum((i + 1) * th, nh - 1), 0)),
                  pl.BlockSpec((1, N_META, d), lambda bi, i, f: (bi, 0, 0)),
                  pl.BlockSpec((1, tm, d), lambda bi, i, f: (bi, i, 0)),
                  pl.BlockSpec((d, fc), lambda bi, i, f: (0, f)),
                  pl.BlockSpec((d, fc), lambda bi, i, f: (0, f + nf)),
                  pl.BlockSpec((fc, d), lambda bi, i, f: (f, 0)),
                  pl.BlockSpec((FFN_CONV_W, fc), lambda bi, i, f: (0, f)),
                  pl.BlockSpec((1, fc), lambda bi, i, f: (0, f)),
                  _const_spec((1, d))],
        out_specs=pl.BlockSpec((1, tm, d), lambda bi, i, f: (bi, i, 0)),
        out_shape=jax.ShapeDtypeStruct((b, n, d), F32),
        scratch_shapes=[pltpu.VMEM((tm + 2 * halo, d), BF16)],
        compiler_params=pltpu.CompilerParams(
            dimension_semantics=("parallel", "parallel", "arbitrary"),
            vmem_limit_bytes=VMEM_LIMIT_BYTES),
        name="conv_ffn",
    )(hn, hn, hn, hnm, h1, w_up, w_up, w_down, conv_w, conv_b, out_gain)


def _rope_tables(n):
    rows = n // GRID_W
    t_row = jnp.repeat(jnp.arange(rows), GRID_W).astype(F32)
    t_col = jnp.tile(jnp.arange(GRID_W), rows).astype(F32)
    half = HEAD_DIM // 2
    inv = ROPE_THETA ** (-jnp.arange(0, half, 2, dtype=F32) / half)
    ang = jnp.concatenate([t_row[:, None] * inv, t_col[:, None] * inv], axis=-1)
    cos, sin = jnp.cos(ang), jnp.sin(ang)
    return jnp.concatenate([cos, cos], axis=-1), jnp.concatenate([-sin, sin], axis=-1)


_TILES = dict(tm=512, tq=512, tk=1024, tc=1024, tf=512, fc=512)


def _pick(n, pref):
    t = pref
    while n % t:
        t //= 2
    return t


def _trunk(x, meta, mp, p):
    b, n, d = x.shape
    rows = b * n
    cos_t, sin_t = _rope_tables(n)
    tm = _pick(n, _TILES["tm"])
    nt = n // tm
    q, k, vt, xc, gy, xc_head = _in_proj(x.reshape(rows, d), meta, nt, p["norm_pre_mix"], p["w_in"], p["q_norm"],
                                         p["k_norm"], cos_t, sin_t, p["lru_conv_w"], p["lru_conv_b"], tm)
    head = xc_head.reshape(b, nt, SUBLANES_BF16, LRU_WIDTH)[:, 0, SUBLANES_BF16 - 2:, :]
    xc_m = jnp.concatenate([jnp.broadcast_to(mp["xc"][None, :N_META - 2], (b, N_META - 2, LRU_WIDTH)), head], axis=1)
    tq = _pick(n, _TILES["tq"])
    tk = _pick(n, _TILES["tk"])
    bound = mp["bound"]

    def attend(running_max):
        def run(bound, q, qm, k, vt, km, vmt):
            real = _attention(bound, q, k, vt, km, vmt, b, tq, tk, False, running_max)
            meta_rows = _attention(bound, qm, k, vt, km, vmt, b, META_Q_ROWS, tk, True, running_max)
            return real, meta_rows
        return run

    attn, attn_m = lax.cond(2.0 * bound[0] <= MAX_FIXED_SHIFT_RANGE, attend(False), attend(True),
                            bound, q, mp["q"][:, :META_Q_ROWS], k, vt, mp["km"], mp["vmt"])
    attn_m = attn_m[:, :N_META]
    lru, lru_m = _lru(xc.reshape(b, n, LRU_WIDTH), gy.reshape(b, n, LRU_WIDTH), xc_m, mp["gy"],
                      p["lru_w_r"], p["lru_b_r"], p["lru_w_i"], p["lru_b_i"],
                      p["lru_lambda"], p["lru_out_norm"], _pick(n, _TILES["tc"]))
    op = (p["attn_out_norm"], p["w_out_a"], p["w_out_l"], p["norm_post_mix"], p["norm_pre_ffn"])
    h1, hn = _out_proj(attn.reshape(rows, ATTN_WIDTH), lru.reshape(rows, LRU_WIDTH), x.reshape(rows, d), *op, tm)
    res_m = jnp.broadcast_to(meta[None], (b, N_META, d)).reshape(b * N_META, d)
    _, hn_m = _out_proj(attn_m.reshape(b * N_META, ATTN_WIDTH), lru_m.reshape(b * N_META, LRU_WIDTH), res_m,
                        *op, N_META)
    return _ffn(hn.reshape(b, n, d), hn_m.reshape(b, N_META, d), h1.reshape(b, n, d), p["w_up"], p["w_down"],
                p["ffn_conv_w"], p["ffn_conv_b"], p["norm_post_ffn"], _pick(n, _TILES["tf"]), _TILES["fc"])


def _head_perm():
    q4 = HEAD_DIM // 4
    idx = jnp.arange(HEAD_DIM).reshape(2, 2, q4)
    return idx.transpose(1, 0, 2).reshape(HEAD_DIM)


def kernel(x_prompt, x_sample, meta_tokens, norm_pre_mix, w_in, q_norm, k_norm, lru_conv_w, lru_conv_b,
           lru_w_r, lru_b_r, lru_w_i, lru_b_i, lru_lambda, attn_out_norm, lru_out_norm, w_out,
           norm_post_mix, norm_pre_ffn, w_up, ffn_conv_w, ffn_conv_b, w_down, norm_post_ffn):
    perm = _head_perm()
    n_rot = N_Q_HEADS + N_KV_HEADS
    cols = (jnp.arange(n_rot)[:, None] * HEAD_DIM + perm[None, :]).reshape(-1)
    cols = jnp.concatenate([cols, jnp.arange(n_rot * HEAD_DIM, IN_WIDTH)])
    w_out_b = w_out[0].astype(BF16)
    p = {
        "norm_pre_mix": norm_pre_mix[0][None], "w_in": w_in[0][:, cols].astype(BF16),
        "q_norm": q_norm[0][perm][None], "k_norm": k_norm[0][perm][None],
        "lru_conv_w": lru_conv_w[0], "lru_conv_b": lru_conv_b[0][None],
        "lru_w_r": (0.5 * lru_w_r[0]).astype(BF16), "lru_b_r": 0.5 * lru_b_r[0],
        "lru_w_i": (0.5 * lru_w_i[0]).astype(BF16), "lru_b_i": 0.5 * lru_b_i[0], "lru_lambda": lru_lambda[0],
        "attn_out_norm": attn_out_norm[0][None], "lru_out_norm": lru_out_norm[0][None],
        "w_out_a": w_out_b, "w_out_l": w_out_b,
        "norm_post_mix": norm_post_mix[0][None], "norm_pre_ffn": norm_pre_ffn[0][None],
        "w_up": w_up[0].astype(BF16), "ffn_conv_w": ffn_conv_w[0], "ffn_conv_b": ffn_conv_b[0][None],
        "w_down": w_down[0].astype(BF16), "norm_post_ffn": norm_post_ffn[0][None],
    }
    ones = jnp.ones((LANES, HEAD_DIM), F32)
    meta_pad = jnp.pad(meta_tokens, ((0, LANES - N_META), (0, 0)))
    qm, km, vtm, xcm, gym, _ = _in_proj(meta_pad, jnp.zeros((SUBLANES_BF16, D_MODEL), F32), 1, p["norm_pre_mix"],
                                        p["w_in"], p["q_norm"], p["k_norm"], ones, jnp.zeros_like(ones),
                                        p["lru_conv_w"], p["lru_conv_b"], LANES)
    bound = (HEAD_DIM * Q_PRESCALE * BF16_ROUNDING_SLACK) * jnp.max(jnp.abs(q_norm[0])) * jnp.max(jnp.abs(k_norm[0]))
    mp = {"q": qm, "km": km[:, :N_META], "vmt": vtm[:, :N_META], "xc": xcm[:N_META], "gy": gym[:N_META],
          "bound": bound.reshape(1).astype(F32)}
    return _trunk(x_prompt, meta_tokens, mp, p), _trunk(x_sample, meta_tokens, mp, p)
```
